```python
import numpy as np
import jax
import jax.numpy as jnp
from jax import lax


D_MODEL = 1024
BATCH = 2
SEQ = 8192
DEPTH = 2

GRID_W = 64
CTX_LEN = 256
EPS = 1e-6
NEG = -1e30
TINY = 1e-30
N_MOD = 6
N_BRANCHES = 4
CONV_CH = 256
CONV_K = 31
HG_HEADS = 4
HG_DK = 64
HG_DV = 64
HG_W = HG_HEADS * HG_DK
HG_V = HG_HEADS * HG_DV
GLA_HEADS = 4
GLA_DK = 32
GLA_DV = 64
GLA_K = GLA_HEADS * GLA_DK
GLA_V = GLA_HEADS * GLA_DV
GLA_RANK = 16
GLA_TAU = 16.0
CHUNK = 64
ATT_HEADS = 4
ATT_KV_HEADS = 2
ATT_HD = 64
ATT_Q = ATT_HEADS * ATT_HD
ATT_KV = ATT_KV_HEADS * ATT_HD
WINDOW = 128
ATT_BLOCK = 128
ROPE_BASE = 10000.0
N_EXPERTS = 16
N_GROUPS = 4
TOP_K = 2
EXPERT_FF = 1024
MOE_BLOCK = 256

IN_NAMES = ('conv_u', 'hg_q', 'hg_f_fwd', 'hg_f_bwd', 'hg_i', 'hg_gate',
            'gla_q', 'gla_k', 'gla_v', 'gla_a_fwd', 'gla_a_bwd', 'gla_gate',
            'att_q', 'att_k', 'att_v', 'branch_gate')
IN_SPLITS = (2 * CONV_CH, HG_W, HG_W, HG_W, HG_V, HG_V,
             GLA_K, GLA_K, GLA_V, GLA_RANK, GLA_RANK, GLA_V,
             ATT_Q, ATT_KV, ATT_KV, N_BRANCHES * D_MODEL)
W_IN_COLS = sum(IN_SPLITS)

kernel_name = 'hybrid_flow_backbone_block'


def rmsnorm(x, g):
    xf = x.astype(jnp.float32)
    y = xf * lax.rsqrt(jnp.mean(xf * xf, axis=-1, keepdims=True) + EPS)
    return (y * g.astype(jnp.float32)).astype(x.dtype)


def head_rmsnorm(o, g, n_heads):
    B, T, W = o.shape
    of = o.astype(jnp.float32).reshape(B, T, n_heads, W // n_heads)
    of = of * lax.rsqrt(jnp.mean(of * of, axis=-1, keepdims=True) + EPS)
    return of.reshape(B, T, W) * g.astype(jnp.float32)


def split_cols(p):
    idx = np.cumsum(IN_SPLITS)[:-1].tolist()
    return dict(zip(IN_NAMES, jnp.split(p, idx, axis=-1)))


def to_heads(a, n_heads):
    B, T, W = a.shape
    return a.reshape(B, T, n_heads, W // n_heads).transpose(0, 2, 1, 3)


def from_heads(a):
    B, H, T, d = a.shape
    return a.transpose(0, 2, 1, 3).reshape(B, T, H * d)


def axial_rope_tables(n_tokens):
    rows = n_tokens // GRID_W
    row = jnp.broadcast_to(jnp.arange(rows, dtype=jnp.float32)[:, None], (rows, GRID_W)).reshape(-1)
    col = jnp.broadcast_to(jnp.arange(GRID_W, dtype=jnp.float32)[None, :], (rows, GRID_W)).reshape(-1)
    half = ATT_HD // 4
    inv = ROPE_BASE ** (-jnp.arange(half, dtype=jnp.float32) / half)
    ang = jnp.stack([row[:, None] * inv, col[:, None] * inv], axis=1)
    return jnp.cos(ang), jnp.sin(ang)


def apply_rope(x, cos, sin):
    B, T, H, hd = x.shape
    xa = x.astype(jnp.float32).reshape(B, T, H, 2, 2, hd // 4)
    x1, x2 = xa[..., 0, :], xa[..., 1, :]
    c = cos[None, :, None]
    s = sin[None, :, None]
    out = jnp.stack([x1 * c - x2 * s, x1 * s + x2 * c], axis=-2)
    return out.reshape(B, T, H, hd).astype(x.dtype)


def conformer_conv(u, dw_w, dw_b, ln_g, ln_b):
    a, b = jnp.split(u, 2, axis=-1)
    z = a * jax.nn.sigmoid(b)
    ch = z.shape[-1]
    z = lax.conv_general_dilated(z, dw_w[:, None, :].astype(z.dtype), window_strides=(1,),
                                 padding=((CONV_K // 2, CONV_K // 2),),
                                 dimension_numbers=('NWC', 'WIO', 'NWC'),
                                 feature_group_count=ch) + dw_b
    zf = z.astype(jnp.float32)
    mu = jnp.mean(zf, axis=-1, keepdims=True)
    var = jnp.mean(jnp.square(zf - mu), axis=-1, keepdims=True)
    zn = (zf - mu) * lax.rsqrt(var + EPS) * ln_g + ln_b
    return jax.nn.silu(zn)


def chunk_gated_scan(q, k, v, g, s0):
    B, H, T, dk = q.shape
    dv = v.shape[-1]
    n = T // CHUNK

    def chunks(a):
        return jnp.moveaxis(a.astype(jnp.float32).reshape(B, H, n, CHUNK, a.shape[-1]), 2, 0)

    tri = jnp.tril(jnp.ones((CHUNK, CHUNK), dtype=bool))[:, :, None]

    def step(state, inp):
        qc, kc, vc, gc = inp
        b = jnp.cumsum(gc, axis=2)
        o = jnp.einsum('bhtk,bhkv->bhtv', qc * jnp.exp(b), state)
        diff = b[:, :, :, None, :] - b[:, :, None, :, :]
        decay = jnp.where(tri, jnp.exp(jnp.minimum(diff, 0.0)), 0.0)
        scores = jnp.einsum('bhtk,bhsk,bhtsk->bhts', qc, kc, decay)
        o = o + jnp.einsum('bhts,bhsv->bhtv', scores, vc)
        b_last = b[:, :, -1:, :]
        state = (jnp.exp(b_last[:, :, 0, :, None]) * state
                 + jnp.einsum('bhsk,bhsv->bhkv', kc * jnp.exp(b_last - b), vc))
        return state, o

    state, o = lax.scan(step, s0.astype(jnp.float32), (chunks(q), chunks(k), chunks(v), chunks(g)))
    return jnp.moveaxis(o, 0, 2).reshape(B, H, T, dv), state


def bidir_gated_scan(ctx_in, lat_in):
    qc, kc, vc, gc = ctx_in
    ql, kl, vl, gl = lat_in
    B, H, _, dk = qc.shape
    s0 = jnp.zeros((B, H, dk, vc.shape[-1]), jnp.float32)

    def flip(a):
        return jnp.flip(a, axis=2)

    oc_f, sc_f = chunk_gated_scan(qc, kc[0], vc, gc[0], s0)
    ol_f, _ = chunk_gated_scan(ql, kl[0], vl, gl[0], sc_f)
    oc_b, sc_b = chunk_gated_scan(flip(qc), flip(kc[1]), flip(vc), flip(gc[1]), s0)
    ol_b, _ = chunk_gated_scan(flip(ql), flip(kl[1]), flip(vl), flip(gl[1]), sc_b)
    return oc_f + flip(oc_b), ol_f + flip(ol_b)


def hgrn2_inputs(p, lb):
    q = to_heads(p['hg_q'], HG_HEADS)
    ks, gs = [], []
    for name, lb_d in (('hg_f_fwd', lb[0]), ('hg_f_bwd', lb[1])):
        z = p[name].astype(jnp.float32)
        f = lb_d + (1.0 - lb_d) * jax.nn.sigmoid(z)
        ks.append(to_heads((1.0 - lb_d) * jax.nn.sigmoid(-z), HG_HEADS))
        gs.append(to_heads(jnp.log(jnp.maximum(f, TINY)), HG_HEADS))
    return q, (ks[0], ks[1]), to_heads(p['hg_i'], HG_HEADS), (gs[0], gs[1])


def gla_inputs(p, w2, b2):
    q = to_heads(p['gla_q'], GLA_HEADS) * GLA_DK ** -0.5
    k = to_heads(p['gla_k'], GLA_HEADS)
    v = to_heads(p['gla_v'], GLA_HEADS)
    g_f = to_heads(jax.nn.log_sigmoid((p['gla_a_fwd'] @ w2[0] + b2[0]).astype(jnp.float32)) / GLA_TAU, GLA_HEADS)
    g_b = to_heads(jax.nn.log_sigmoid((p['gla_a_bwd'] @ w2[1] + b2[1]).astype(jnp.float32)) / GLA_TAU, GLA_HEADS)
    return q, (k, k), v, (g_f, g_b)


def attn_heads(p):
    B, T, _ = p['att_q'].shape
    return (p['att_q'].reshape(B, T, ATT_HEADS, ATT_HD),
            p['att_k'].reshape(B, T, ATT_KV_HEADS, ATT_HD),
            p['att_v'].reshape(B, T, ATT_KV_HEADS, ATT_HD))


def window_sink_attention(q, k, v, kc, vc, sink):
    B, T, Hq, hd = q.shape
    Hkv = k.shape[2]
    G = Hq // Hkv
    nb = T // ATT_BLOCK
    L = kc.shape[1]
    qb = q.reshape(B, nb, ATT_BLOCK, Hkv, G, hd) * hd ** -0.5

    def band(a):
        pad = jnp.zeros((B, ATT_BLOCK, Hkv, hd), a.dtype)
        ap = jnp.concatenate([pad, a, pad], axis=1)
        return jnp.concatenate([ap[:, i * ATT_BLOCK:i * ATT_BLOCK + T].reshape(B, nb, ATT_BLOCK, Hkv, hd)
                                for i in range(3)], axis=2)

    kb, vb = band(k), band(v)
    q_pos = jnp.arange(nb)[:, None] * ATT_BLOCK + jnp.arange(ATT_BLOCK)[None, :]
    k_pos = jnp.arange(nb)[:, None] * ATT_BLOCK - ATT_BLOCK + jnp.arange(3 * ATT_BLOCK)[None, :]
    valid = ((jnp.abs(q_pos[:, :, None] - k_pos[:, None, :]) <= WINDOW)
             & (k_pos >= 0)[:, None, :] & (k_pos < T)[:, None, :])
    s_band = jnp.einsum('bnqhgd,bnshd->bnhgqs', qb, kb).astype(jnp.float32)
    s_band = jnp.where(valid[None, :, None, None], s_band, NEG)
    s_ctx = jnp.einsum('bnqhgd,bchd->bnhgqc', qb, kc).astype(jnp.float32)
    s_sink = jnp.broadcast_to(sink.astype(jnp.float32).reshape(1, 1, Hkv, G, 1, 1), s_band.shape[:-1] + (1,))
    p = jax.nn.softmax(jnp.concatenate([s_band, s_ctx, s_sink], axis=-1), axis=-1)
    nk = 3 * ATT_BLOCK
    out = (jnp.einsum('bnhgqs,bnshd->bnqhgd', p[..., :nk].astype(vb.dtype), vb)
           + jnp.einsum('bnhgqc,bchd->bnqhgd', p[..., nk:nk + L].astype(vc.dtype), vc))
    return out.reshape(B, T, Hq * hd)


def context_sink_attention(qc, kc, vc, sink):
    B, L, Hq, hd = qc.shape
    Hkv = kc.shape[2]
    G = Hq // Hkv
    qg = qc.reshape(B, L, Hkv, G, hd) * hd ** -0.5
    s = jnp.einsum('bqhgd,bchd->bhgqc', qg, kc).astype(jnp.float32)
    s_sink = jnp.broadcast_to(sink.astype(jnp.float32).reshape(1, Hkv, G, 1, 1), s.shape[:-1] + (1,))
    p = jax.nn.softmax(jnp.concatenate([s, s_sink], axis=-1), axis=-1)[..., :L]
    return jnp.einsum('bhgqc,bchd->bqhgd', p.astype(vc.dtype), vc).reshape(B, L, Hq * hd)


def moe_ffn(h, router_w, router_b, w1, w3, w2):
    n_tok, d = h.shape
    epg = N_EXPERTS // N_GROUPS
    aff = jax.nn.sigmoid(h.astype(jnp.float32) @ router_w.astype(jnp.float32))
    grouped = (aff + router_b.astype(jnp.float32)).reshape(n_tok, N_GROUPS, epg)
    best_group = jnp.argmax(lax.top_k(grouped, TOP_K)[0].sum(-1), axis=-1)
    _, local = lax.top_k(grouped[jnp.arange(n_tok), best_group], TOP_K)
    expert = best_group[:, None] * epg + local
    gate = jnp.take_along_axis(aff, expert, axis=1)
    gate = gate / jnp.sum(gate, axis=-1, keepdims=True)

    n_slot = n_tok * TOP_K
    e_flat = expert.reshape(-1)
    order = jnp.argsort(e_flat)
    e_sorted = e_flat[order]
    tok_sorted = (order // TOP_K).astype(jnp.int32)
    gate_sorted = gate.reshape(-1)[order]
    counts = jnp.zeros((N_EXPERTS,), jnp.int32).at[e_flat].add(1)
    starts = jnp.cumsum(counts) - counts
    padded = (counts + MOE_BLOCK - 1) // MOE_BLOCK * MOE_BLOCK
    padded_end = jnp.cumsum(padded)
    padded_start = padded_end - padded
    dest = padded_start[e_sorted] + jnp.arange(n_slot, dtype=jnp.int32) - starts[e_sorted]
    n_blocks = -(-n_slot // MOE_BLOCK) + N_EXPERTS
    n_rows = n_blocks * MOE_BLOCK
    row_tok = jnp.full((n_rows,), n_tok, jnp.int32).at[dest].set(tok_sorted)
    row_gate = jnp.zeros((n_rows,), jnp.float32).at[dest].set(gate_sorted)
    block_expert = jnp.minimum(
        jnp.searchsorted(padded_end, jnp.arange(n_blocks, dtype=jnp.int32) * MOE_BLOCK, side='right'),
        N_EXPERTS - 1)
    xs = jnp.concatenate([h, jnp.zeros((1, d), h.dtype)], axis=0)[row_tok].reshape(n_blocks, MOE_BLOCK, d)

    def expert_block(args):
        xb, e = args
        return (jax.nn.silu(xb @ w1[e]) * (xb @ w3[e])) @ w2[e]

    ys = lax.map(expert_block, (xs, block_expert)).reshape(n_rows, d)
    out = jax.ops.segment_sum(ys.astype(jnp.float32) * row_gate[:, None], row_tok, num_segments=n_tok + 1)
    return out[:n_tok].astype(h.dtype)


def mixing_sublayer(hc, hl, cos, sin, lb, w_in, conv_w, conv_b, conv_ln_g, conv_ln_b,
                    hg_norm_g, gla_w2, gla_b2, gla_norm_g, att_sink,
                    w_br_conv, w_br_hg, w_br_gla, w_br_att, w_out, need_ctx_out):
    pc = split_cols(hc @ w_in)
    pl = split_cols(hl @ w_in)
    hg_c, hg_l = bidir_gated_scan(hgrn2_inputs(pc, lb), hgrn2_inputs(pl, lb))
    gla_c, gla_l = bidir_gated_scan(gla_inputs(pc, gla_w2, gla_b2), gla_inputs(pl, gla_w2, gla_b2))
    qc, kc, vc = attn_heads(pc)
    ql, kl, vl = attn_heads(pl)
    att_l = window_sink_attention(apply_rope(ql, cos, sin), apply_rope(kl, cos, sin), vl, kc, vc, att_sink)

    def branch_merge(p, hg_o, gla_o, att_o):
        conv_y = conformer_conv(p['conv_u'], conv_w, conv_b, conv_ln_g, conv_ln_b)
        hg_y = head_rmsnorm(from_heads(hg_o), hg_norm_g, HG_HEADS) * jax.nn.silu(p['hg_gate'].astype(jnp.float32))
        gla_y = head_rmsnorm(from_heads(gla_o), gla_norm_g, GLA_HEADS) * jax.nn.silu(p['gla_gate'].astype(jnp.float32))
        branches = (conv_y @ w_br_conv, hg_y @ w_br_hg, gla_y @ w_br_gla, att_o @ w_br_att)
        gates = jnp.split(jax.nn.sigmoid(p['branch_gate'].astype(jnp.float32)), N_BRANCHES, axis=-1)
        merged = gates[0] * branches[0]
        for gt, br in zip(gates[1:], branches[1:]):
            merged = merged + gt * br
        return merged @ w_out

    mix_l = branch_merge(pl, hg_l, gla_l, att_l)
    mix_c = branch_merge(pc, hg_c, gla_c, context_sink_attention(qc, kc, vc, att_sink)) if need_ctx_out else None
    return mix_c, mix_l


def hybrid_layer(xc, xl, c, c_ctx, cos, sin, lb, router_w, router_b, w_mod, b_mod, g_mix, g_ffn, w_in,
                 conv_w, conv_b, conv_ln_g, conv_ln_b, hg_norm_g, gla_w2, gla_b2, gla_norm_g, att_sink,
                 w_br_conv, w_br_hg, w_br_gla, w_br_att, w_out, moe_w1, moe_w3, moe_w2, need_ctx_out):
    mod_l = jnp.split((jax.nn.silu(c) @ w_mod + b_mod)[:, None], N_MOD, axis=-1)
    mod_c = jnp.split((jax.nn.silu(c_ctx) @ w_mod + b_mod)[None, None], N_MOD, axis=-1)

    def modulated_norm(x, g, shift, scale):
        return rmsnorm(x, g) * (1 + scale) + shift

    mix_c, mix_l = mixing_sublayer(
        modulated_norm(xc, g_mix, mod_c[0], mod_c[1]), modulated_norm(xl, g_mix, mod_l[0], mod_l[1]),
        cos, sin, lb, w_in, conv_w, conv_b, conv_ln_g, conv_ln_b, hg_norm_g, gla_w2, gla_b2, gla_norm_g,
        att_sink, w_br_conv, w_br_hg, w_br_gla, w_br_att, w_out, need_ctx_out)
    xl = xl + mod_l[2] * mix_l
    h2l = modulated_norm(xl, g_ffn, mod_l[3], mod_l[4])
    d = xl.shape[-1]
    if need_ctx_out:
        xc = xc + mod_c[2] * mix_c
        h2c = modulated_norm(xc, g_ffn, mod_c[3], mod_c[4])
        n_ctx = xc.shape[1]
        h2 = jnp.concatenate([h2c, h2l], axis=1)
        y = moe_ffn(h2.reshape(-1, d), router_w, router_b, moe_w1, moe_w3, moe_w2).reshape(h2.shape)
        return xc + mod_c[5] * y[:, :n_ctx], xl + mod_l[5] * y[:, n_ctx:]
    y = moe_ffn(h2l.reshape(-1, d), router_w, router_b, moe_w1, moe_w3, moe_w2).reshape(h2l.shape)
    return None, xl + mod_l[5] * y


def setup_inputs(seed: int = 0) -> dict:
    key = jax.random.key(seed)
    keys = iter(jax.random.split(key, 32))

    def nrm(shape, scale):
        return jax.random.normal(next(keys), shape, jnp.float32) * scale

    D = D_MODEL
    L = DEPTH
    return {
        'x': nrm((BATCH, SEQ, D), 1.0),
        'c': nrm((BATCH, D), 1.0),
        'ctx': nrm((BATCH, CTX_LEN, D), 1.0),
        'c_ctx': nrm((D,), 1.0),
        'hg_lb_logits': nrm((L, 2, HG_W), 0.5),
        'router_w': nrm((D, N_EXPERTS), D ** -0.5),
        'router_b': nrm((N_EXPERTS,), 0.01),
        'final_g': 1.0 + nrm((D,), 0.05),
        'w_mod': nrm((L, D, N_MOD * D), 0.5 * D ** -0.5),
        'b_mod': nrm((L, N_MOD * D), 0.02),
        'g_mix': 1.0 + nrm((L, D), 0.05),
        'g_ffn': 1.0 + nrm((L, D), 0.05),
        'w_in': nrm((L, D, W_IN_COLS), D ** -0.5),
        'conv_w': nrm((L, CONV_K, CONV_CH), CONV_K ** -0.5),
        'conv_b': nrm((L, CONV_CH), 0.02),
        'conv_ln_g': 1.0 + nrm((L, CONV_CH), 0.05),
        'conv_ln_b': nrm((L, CONV_CH), 0.02),
        'hg_norm_g': 1.0 + nrm((L, HG_V), 0.05),
        'gla_w2': nrm((L, 2, GLA_RANK, GLA_K), GLA_RANK ** -0.5),
        'gla_b2': nrm((L, 2, GLA_K), 0.1),
        'gla_norm_g': 1.0 + nrm((L, GLA_V), 0.05),
        'att_sink': nrm((L, ATT_HEADS), 0.5),
        'w_br_conv': nrm((L, CONV_CH, D), CONV_CH ** -0.5),
        'w_br_hg': nrm((L, HG_V, D), HG_V ** -0.5),
        'w_br_gla': nrm((L, GLA_V, D), GLA_V ** -0.5),
        'w_br_att': nrm((L, ATT_Q, D), ATT_Q ** -0.5),
        'w_out': nrm((L, D, D), D ** -0.5),
        'moe_w1': nrm((L, N_EXPERTS, D, EXPERT_FF), D ** -0.5),
        'moe_w3': nrm((L, N_EXPERTS, D, EXPERT_FF), D ** -0.5),
        'moe_w2': nrm((L, N_EXPERTS, EXPERT_FF, D), EXPERT_FF ** -0.5),
    }


def reference(x, c, ctx, c_ctx, hg_lb_logits, router_w, router_b, final_g, w_mod, b_mod, g_mix, g_ffn,
              w_in, conv_w, conv_b, conv_ln_g, conv_ln_b, hg_norm_g, gla_w2, gla_b2, gla_norm_g, att_sink,
              w_br_conv, w_br_hg, w_br_gla, w_br_att, w_out, moe_w1, moe_w3, moe_w2):
    cos, sin = axial_rope_tables(x.shape[1])
    lb_sm = jax.nn.softmax(hg_lb_logits.astype(jnp.float32), axis=0)
    lower_bounds = jnp.cumsum(lb_sm, axis=0) - lb_sm[0]
    xc, xl = ctx, x
    for l in range(DEPTH):
        xc, xl = hybrid_layer(
            xc, xl, c, c_ctx, cos, sin, lower_bounds[l], router_w, router_b,
            w_mod[l], b_mod[l], g_mix[l], g_ffn[l], w_in[l], conv_w[l], conv_b[l], conv_ln_g[l], conv_ln_b[l],
            hg_norm_g[l], gla_w2[l], gla_b2[l], gla_norm_g[l], att_sink[l],
            w_br_conv[l], w_br_hg[l], w_br_gla[l], w_br_att[l], w_out[l],
            moe_w1[l], moe_w3[l], moe_w2[l], need_ctx_out=(l < DEPTH - 1))
    return rmsnorm(xl, final_g)
```

```python
import functools

import numpy as np
import jax
import jax.numpy as jnp
from jax import lax
from jax.experimental import pallas as pl
from jax.experimental.pallas import tpu as pltpu

F32 = jnp.float32
BF16 = jnp.bfloat16

EPS = 1e-6
NEG = -1e30
TINY = 1e-30
N_MOD = 6
CONV_CH = 256
CONV_K = 31
HG_HEADS = 4
HG_W = 256
GLA_HEADS = 4
GLA_K = 128
GLA_V = 256
GLA_RANK = 16
GLA_TAU = 16.0
ATT_HEADS = 4
ATT_KV_HEADS = 2
ATT_HD = 64
ATT_BLOCK = 128
GRID_W = 64
ROPE_BASE = 10000.0
N_EXPERTS = 16
N_GROUPS = 4
TOP_K = 2
MOE_BLOCK = 256

LANES = 128
TOK_TILE = 256
SCAN_BLOCK = 16
HALO = 16
VMEM_LIMIT = 56 * 1024 * 1024

W_CONV = (0, 512)
W_HG = (512, 1792)
W_GLA = (1792, 2688)
W_ATT = (2688, 3200)
W_BG = (3200, 7296)
W_IN_PACKED = 7296


def _cparams(sem):
    return pltpu.CompilerParams(dimension_semantics=sem, vmem_limit_bytes=VMEM_LIMIT)


def _dot(a, b):
    return jnp.dot(a, b, preferred_element_type=F32)


def _split(a):
    hi = a.astype(BF16)
    lo = (a - hi.astype(F32)).astype(BF16)
    return hi, lo


def _dot_f32(a, b):
    ah, al = _split(a)
    bh, bl = _split(b)
    return _dot(ah, bh) + _dot(ah, bl) + _dot(al, bh)


def _sigmoid(x):
    return 1.0 / (1.0 + jnp.exp(-x))


def _silu(x):
    return x * _sigmoid(x)


def _mod_kernel(c_ref, w_ref, b_ref, o_ref):
    c = c_ref[...]
    o_ref[0] = _dot_f32(_silu(c), w_ref[0]) + b_ref[0]


def modulation(c_rows, w_mod, b_mod):
    depth, d, six_d = w_mod.shape
    nblk = six_d // d
    return pl.pallas_call(
        _mod_kernel,
        grid=(depth, nblk),
        in_specs=[pl.BlockSpec((8, d), lambda l, j: (0, 0)),
                  pl.BlockSpec((1, d, d), lambda l, j: (l, 0, j)),
                  pl.BlockSpec((1, 1, d), lambda l, j: (l, 0, j))],
        out_specs=pl.BlockSpec((1, 8, d), lambda l, j: (l, 0, j)),
        out_shape=jax.ShapeDtypeStruct((depth, 8, six_d), F32),
        compiler_params=_cparams(("parallel", "parallel")),
        name="modulation",
    )(c_rows, w_mod, b_mod.reshape(depth, 1, six_d))


def _modnorm(x, g, shift, scale):
    y = x * lax.rsqrt(jnp.mean(x * x, axis=-1, keepdims=True) + EPS)
    return (y * g) * (1.0 + scale) + shift


def _inproj_kernel(x_ref, mod_ref, g_ref, w_ref, lb_ref, w2_ref, b2_ref, cos_ref, sin_ref,
                   conv_ref, hg_ref, gla_ref, att_ref, bg_ref):
    x = x_ref[...]
    h = _modnorm(x, g_ref[...], mod_ref[0, 0:1, :], mod_ref[0, 1:2, :]).astype(BF16)

    p = _dot(h, w_ref[:, W_CONV[0]:W_CONV[1]])
    conv_ref[...] = p[:, :CONV_CH] * _sigmoid(p[:, CONV_CH:])

    p = _dot(h, w_ref[:, W_HG[0]:W_HG[1]])
    hg_ref[:, 0:256] = p[:, 0:256]
    hg_ref[:, 768:1024] = p[:, 768:1024]
    hg_ref[:, 1536:1792] = _silu(p[:, 1024:1280])
    for d in range(2):
        z = p[:, 256 * (d + 1):256 * (d + 2)]
        lb = lb_ref[d:d + 1, :]
        hg_ref[:, 256 * (d + 1):256 * (d + 2)] = (1.0 - lb) * _sigmoid(-z)
        hg_ref[:, 256 * (d + 4):256 * (d + 5)] = jnp.maximum(lb + (1.0 - lb) * _sigmoid(z), TINY)

    p = _dot(h, w_ref[:, W_GLA[0]:W_GLA[1]])
    gla_ref[:, 0:128] = p[:, 0:128] * (float(GLA_K // GLA_HEADS) ** -0.5)
    gla_ref[:, 128:256] = p[:, 128:256]
    gla_ref[:, 512:768] = p[:, 256:512]
    gla_ref[:, 768:1024] = _silu(p[:, 512:768])
    u = _dot_f32(p[:, 768:896], w2_ref[...]) + b2_ref[...]
    log_sig = jnp.minimum(u, 0.0) - jnp.log(1.0 + jnp.exp(-jnp.abs(u)))
    gla_ref[:, 256:512] = jnp.exp(log_sig * (1.0 / GLA_TAU))

    p = _dot(h, w_ref[:, W_ATT[0]:W_ATT[1]])
    cos = cos_ref[...]
    sin = sin_ref[...]
    def rope(v, width):
        lane = lax.broadcasted_iota(jnp.int32, v.shape, 1)
        partner = jnp.where((lane & 31) < 16,
                            pltpu.roll(v, width - 16, axis=1), pltpu.roll(v, 16, axis=1))
        return v * cos[:, :width] + partner * sin[:, :width]

    att_ref[:, 0:256] = rope(p[:, 0:256], 256) * (float(ATT_HD) ** -0.5)
    att_ref[:, 256:384] = rope(p[:, 256:384], 128)
    att_ref[:, 384:512] = p[:, 384:512]

    p = _dot(h, w_ref[:, W_BG[0]:W_BG[1]])
    bg_ref[...] = _sigmoid(p).astype(BF16)


def inproj(x, modv, g_mix, w_in_p, lb, w2p, b2p, cos_t, sin_t, n_batch, n_ctx_tiles):
    n, d = x.shape
    nt = n // TOK_TILE
    tiles_per_batch = nt // n_batch

    def mod_idx(i):
        b = i // tiles_per_batch
        return b * 2 + ((i % tiles_per_batch) >= n_ctx_tiles).astype(jnp.int32)

    const = lambda i: (0, 0)
    row = lambda i: (i, 0)
    seq = lambda i: (i % tiles_per_batch, 0)
    outs = pl.pallas_call(
        _inproj_kernel,
        grid=(nt,),
        in_specs=[pl.BlockSpec((TOK_TILE, d), row),
                  pl.BlockSpec((1, N_MOD, d), lambda i: (mod_idx(i), 0, 0)),
                  pl.BlockSpec((1, d), const),
                  pl.BlockSpec((d, W_IN_PACKED), const, pipeline_mode=pl.Buffered(1)),
                  pl.BlockSpec((2, HG_W), const),
                  pl.BlockSpec((LANES, 2 * GLA_K), const),
                  pl.BlockSpec((1, 2 * GLA_K), const),
                  pl.BlockSpec((TOK_TILE, 256), seq),
                  pl.BlockSpec((TOK_TILE, 256), seq)],
        out_specs=[pl.BlockSpec((TOK_TILE, 256), row),
                   pl.BlockSpec((TOK_TILE, 1792), row),
                   pl.BlockSpec((TOK_TILE, 1024), row),
                   pl.BlockSpec((TOK_TILE, 512), row),
                   pl.BlockSpec((TOK_TILE, 4096), row)],
        out_shape=[jax.ShapeDtypeStruct((n, 256), F32),
                   jax.ShapeDtypeStruct((n, 1792), F32),
                   jax.ShapeDtypeStruct((n, 1024), F32),
                   jax.ShapeDtypeStruct((n, 512), F32),
                   jax.ShapeDtypeStruct((n, 4096), BF16)],
        compiler_params=_cparams(("parallel",)),
        name="inproj",
    )(x, modv, g_mix, w_in_p, lb, w2p, b2p, cos_t, sin_t)
    return outs


def _scan_kernel(q_ref, k_ref, v_ref, f_ref, sel_ref, msk_ref, o_ref, s_ref, fp_ref, kp_ref, vp_ref,
                 *, rev):
    c = SCAN_BLOCK
    tt, kl = q_ref.shape
    nb = tt // c
    pad = c

    @pl.when(pl.program_id(1) == 0)
    def _():
        s_ref[...] = jnp.zeros_like(s_ref)
        fp_ref[...] = jnp.ones_like(fp_ref)
        kp_ref[...] = jnp.zeros_like(kp_ref)
        vp_ref[...] = jnp.zeros_like(vp_ref)

    q = q_ref[...]
    f = f_ref[...]
    fp_ref[pad:pad + tt, :] = f
    kp_ref[pad:pad + tt, :] = k_ref[...]
    vp_ref[pad:pad + tt, :] = v_ref[...]

    def shifted(ref, d):
        start = pad + d if rev else pad - d
        return ref[start:start + tt, :]

    r = lax.broadcasted_iota(jnp.int32, (tt, kl), 0) & (c - 1)
    if rev:
        r = (c - 1) - r
    rr = (c - 1) - r

    def roll_rows(x, d):
        d = -d if rev else d
        return pltpu.roll(x, d % tt, axis=0)

    a = f
    g = f
    s = 1
    while s < c:
        a = a * jnp.where(r >= s, roll_rows(a, s), 1.0)
        g = g * jnp.where(rr >= s, roll_rows(g, -s), 1.0)
        s *= 2
    suffix = jnp.where(rr >= 1, roll_rows(g, -1), 1.0)

    sel = sel_ref[...]
    p = q
    acc = jnp.zeros((tt, 256), F32)
    for d in range(c):
        if d > 0:
            p = p * shifted(fp_ref, d - 1)
        term = jnp.where(r >= d, p * shifted(kp_ref, d), 0.0).astype(BF16)
        w = _dot(term, sel)
        acc = acc + w * shifted(vp_ref, d)

    qt = (q * a).astype(BF16)
    kh = (k_ref[...] * suffix).astype(BF16)
    vb = v_ref[...].astype(BF16)
    last = 0 if rev else c - 1
    gam = jnp.concatenate([a[j * c + last:j * c + last + 1, :] for j in range(nb)], axis=0)
    gam_t = gam.T
    msk = msk_ref[...]
    st = s_ref[...]
    for jj in range(nb):
        j = nb - 1 - jj if rev else jj
        lo = j * c
        o_ref[lo:lo + c, :] = acc[lo:lo + c, :] + _dot(qt[lo:lo + c, :], st.astype(BF16))
        kv = lax.dot_general(kh[lo:lo + c, :], vb[lo:lo + c, :], (((0,), (0,)), ((), ())),
                             preferred_element_type=F32)
        st = gam_t[:, j:j + 1] * st + kv * msk
    s_ref[...] = st


def gated_scan(pack, cols, kl, n_batch, n_ctx_tiles, rev):
    n = pack.shape[0]
    nt = n // TOK_TILE
    tpb = nt // n_batch
    qc, kc, vc, fc = cols

    def tile(i):
        if not rev:
            return i
        return jnp.where(i < n_ctx_tiles, n_ctx_tiles - 1 - i, tpb - 1 - (i - n_ctx_tiles))

    heads = 4
    dk = kl // heads
    sel = (np.arange(kl)[:, None] // dk == np.arange(256)[None, :] // 64)
    sel_b = jnp.asarray(sel, BF16)
    msk = jnp.asarray(sel, F32)
    rows = lambda col: (lambda b, i: (b * tpb + tile(i), col))
    const = lambda b, i: (0, 0)
    pad_rows = TOK_TILE + 2 * SCAN_BLOCK
    return pl.pallas_call(
        functools.partial(_scan_kernel, rev=rev),
        grid=(n_batch, tpb),
        in_specs=[pl.BlockSpec((TOK_TILE, kl), rows(qc)),
                  pl.BlockSpec((TOK_TILE, kl), rows(kc)),
                  pl.BlockSpec((TOK_TILE, 256), rows(vc)),
                  pl.BlockSpec((TOK_TILE, kl), rows(fc)),
                  pl.BlockSpec((kl, 256), const),
                  pl.BlockSpec((kl, 256), const)],
        out_specs=pl.BlockSpec((TOK_TILE, 256), rows(0)),
        out_shape=jax.ShapeDtypeStruct((n, 256), F32),
        scratch_shapes=[pltpu.VMEM((kl, 256), F32),
                        pltpu.VMEM((pad_rows, kl), F32),
                        pltpu.VMEM((pad_rows, kl), F32),
                        pltpu.VMEM((pad_rows, 256), F32)],
        compiler_params=_cparams(("parallel", "arbitrary")),
        name="scan_rev" if rev else "scan_fwd",
    )(pack, pack, pack, pack, sel_b, msk)


def _attn_kernel(sink_ref, q_ref, kc_ref, vc_ref, k0_ref, k1_ref, k2_ref, v0_ref, v1_ref, v2_ref,
                 o_ref, *, ctx_blocks, blocks_per_batch):
    n = pl.program_id(1)
    blk = ATT_BLOCK
    lane = lax.broadcasted_iota(jnp.int32, (blk, LANES), 1)
    low = lane < ATT_HD
    qi = lax.broadcasted_iota(jnp.int32, (blk, blk), 0)
    ki = lax.broadcasted_iota(jnp.int32, (blk, blk), 1)
    is_lat = n >= ctx_blocks
    band_ok = [
        jnp.logical_and(jnp.logical_and(is_lat, n - 1 >= ctx_blocks), qi <= ki),
        jnp.logical_and(is_lat, qi >= 0),
        jnp.logical_and(jnp.logical_and(is_lat, n + 1 < blocks_per_batch), ki <= qi),
    ]

    def dup(x, hk):
        xr = pltpu.roll(x, ATT_HD, axis=1)
        lo_x = lax.broadcasted_iota(jnp.int32, x.shape, 1) < ATT_HD
        return (jnp.where(lo_x, x, xr) if hk == 0 else jnp.where(lo_x, xr, x)).astype(BF16)

    q = q_ref[...]
    keys = [kc_ref[...], k0_ref[...], k1_ref[...], k2_ref[...]]
    vals = [vc_ref[...], v0_ref[...], v1_ref[...], v2_ref[...]]
    nt_dims = (((1,), (1,)), ((), ()))
    halves = []
    for hk in range(ATT_KV_HEADS):
        kd = [dup(x, hk) for x in keys]
        vd = [dup(x, hk) for x in vals]
        qh = q[:, hk * LANES:(hk + 1) * LANES]
        outs = []
        for g in range(2):
            qg = jnp.where(low if g == 0 else jnp.logical_not(low), qh, 0.0).astype(BF16)
            s = [lax.dot_general(qg, kx, nt_dims, preferred_element_type=F32) for kx in kd]
            for t in range(3):
                s[t + 1] = jnp.where(band_ok[t], s[t + 1], NEG)
            sink = sink_ref[hk * 2 + g]
            m = jnp.maximum(sink, s[0].max(axis=-1, keepdims=True))
            for t in range(3):
                m = jnp.maximum(m, s[t + 1].max(axis=-1, keepdims=True))
            den = jnp.exp(sink - m)
            o = jnp.zeros((blk, LANES), F32)
            for t in range(4):
                e = jnp.exp(s[t] - m)
                den = den + e.sum(axis=-1, keepdims=True)
                o = o + _dot(e.astype(BF16), vd[t])
            outs.append(o / den)
        halves.append(jnp.where(low, outs[0], outs[1]))
    o_ref[:, 0:LANES] = halves[0]
    o_ref[:, LANES:2 * LANES] = halves[1]


def attention(att_pack, sink, n_batch, ctx_len):
    n = att_pack.shape[0]
    bpb = n // n_batch // ATT_BLOCK
    cb = ctx_len // ATT_BLOCK

    def band(off, col):
        def idx(b, i, *_):
            m = jnp.clip(i + off, cb, bpb - 1)
            return (b * bpb + m, col)
        return pl.BlockSpec((ATT_BLOCK, LANES), idx)

    def ctx(col):
        return pl.BlockSpec((ctx_len, LANES), lambda b, i, *_: (b * (bpb * ATT_BLOCK // ctx_len), col))

    grid_spec = pltpu.PrefetchScalarGridSpec(
        num_scalar_prefetch=1,
        grid=(n_batch, bpb),
        in_specs=[pl.BlockSpec((ATT_BLOCK, 256), lambda b, i, *_: (b * bpb + i, 0)),
                  ctx(2), ctx(3),
                  band(-1, 2), band(0, 2), band(1, 2),
                  band(-1, 3), band(0, 3), band(1, 3)],
        out_specs=pl.BlockSpec((ATT_BLOCK, 256), lambda b, i, *_: (b * bpb + i, 0)),
    )
    return pl.pallas_call(
        functools.partial(_attn_kernel, ctx_blocks=cb, blocks_per_batch=bpb),
        grid_spec=grid_spec,
        out_shape=jax.ShapeDtypeStruct((n, 256), F32),
        compiler_params=_cparams(("parallel", "parallel")),
        name="attention",
    )(sink, *([att_pack] * 9))


def _head_norm(o, ones_bd, g):
    sq = o * o
    hi, lo = _split(sq)
    ms = (_dot(hi, ones_bd) + _dot(lo, ones_bd)) * (1.0 / 64.0)
    return o * lax.rsqrt(ms + EPS) * g


def _merge_kernel(x_ref, mod_ref, zp_ref, z_ref, zn_ref, hgf_ref, hgb_ref, hgg_ref, glf_ref, glb_ref,
                  glg_ref, att_ref, bg_ref, cw_ref, cb_ref, lng_ref, lnb_ref, hgn_ref, gln_ref,
                  ones_ref, wbc_ref, wbh_ref, wbg_ref, wba_ref, wo_ref, gf_ref, rw_ref,
                  xo_ref, h2_ref, aff_ref, zs_ref, *, tiles_per_batch, n_ctx_tiles):
    i = pl.program_id(0)
    tt = x_ref.shape[0]
    ti = i % tiles_per_batch
    has_prev = jnp.logical_and(ti != 0, ti != n_ctx_tiles)
    has_next = jnp.logical_and(ti != n_ctx_tiles - 1, ti != tiles_per_batch - 1)

    zs_ref[0:HALO, :] = jnp.where(has_prev, zp_ref[...], 0.0)
    zs_ref[HALO:HALO + tt, :] = z_ref[...]
    zs_ref[HALO + tt:HALO + tt + HALO, :] = jnp.where(has_next, zn_ref[...], 0.0)
    acc = jnp.zeros((tt, CONV_CH), F32) + cb_ref[...]
    for j in range(CONV_K):
        off = HALO - CONV_K // 2 + j
        acc = acc + zs_ref[off:off + tt, :] * cw_ref[j:j + 1, :]
    mu = jnp.mean(acc, axis=-1, keepdims=True)
    cen = acc - mu
    var = jnp.mean(cen * cen, axis=-1, keepdims=True)
    conv_y = _silu(cen * lax.rsqrt(var + EPS) * lng_ref[...] + lnb_ref[...])

    ones_bd = ones_ref[...]
    hg_y = _head_norm(hgf_ref[...] + hgb_ref[...], ones_bd, hgn_ref[...]) * hgg_ref[...]
    gla_y = _head_norm(glf_ref[...] + glb_ref[...], ones_bd, gln_ref[...]) * glg_ref[...]

    d = x_ref.shape[1]
    merged = bg_ref[:, 0:d].astype(F32) * _dot(conv_y.astype(BF16), wbc_ref[...])
    merged = merged + bg_ref[:, d:2 * d].astype(F32) * _dot(hg_y.astype(BF16), wbh_ref[...])
    merged = merged + bg_ref[:, 2 * d:3 * d].astype(F32) * _dot(gla_y.astype(BF16), wbg_ref[...])
    merged = merged + bg_ref[:, 3 * d:4 * d].astype(F32) * _dot(att_ref[...].astype(BF16), wba_ref[...])
    mix = _dot(merged.astype(BF16), wo_ref[...])

    x_new = x_ref[...] + mod_ref[0, 2:3, :] * mix
    xo_ref[...] = x_new
    h2 = _modnorm(x_new, gf_ref[...], mod_ref[0, 3:4, :], mod_ref[0, 4:5, :])
    h2_ref[...] = h2
    aff_ref[...] = _sigmoid(_dot_f32(h2, rw_ref[...]))


def merge(x, modv, conv_z, hg_pack, hg_f, hg_b, gla_pack, gla_f, gla_b, att_o, bgate, lw, n_batch,
          n_ctx_tiles):
    n, d = x.shape
    nt = n // TOK_TILE
    tpb = nt // n_batch
    hpt = TOK_TILE // HALO
    n_halo = n // HALO

    def mod_idx(i):
        return (i // tpb) * 2 + ((i % tpb) >= n_ctx_tiles).astype(jnp.int32)

    row = lambda i: (i, 0)
    const = lambda i: (0, 0)
    col = lambda c: (lambda i: (i, c))
    full = lambda a: pl.BlockSpec(a.shape, const)
    weights = [lw["conv_w"], lw["conv_b"], lw["conv_ln_g"], lw["conv_ln_b"], lw["hg_norm_g"],
               lw["gla_norm_g"], lw["ones_bd"], lw["w_br_conv"], lw["w_br_hg"], lw["w_br_gla"],
               lw["w_br_att"], lw["w_out"], lw["g_ffn"], lw["router_w"]]
    return pl.pallas_call(
        functools.partial(_merge_kernel, tiles_per_batch=tpb, n_ctx_tiles=n_ctx_tiles),
        grid=(nt,),
        in_specs=[pl.BlockSpec((TOK_TILE, d), row),
                  pl.BlockSpec((1, N_MOD, d), lambda i: (mod_idx(i), 0, 0)),
                  pl.BlockSpec((HALO, CONV_CH), lambda i: (jnp.maximum(i * hpt - 1, 0), 0)),
                  pl.BlockSpec((TOK_TILE, CONV_CH), row),
                  pl.BlockSpec((HALO, CONV_CH), lambda i: (jnp.minimum((i + 1) * hpt, n_halo - 1), 0)),
                  pl.BlockSpec((TOK_TILE, 256), row),
                  pl.BlockSpec((TOK_TILE, 256), row),
                  pl.BlockSpec((TOK_TILE, 256), col(6)),
                  pl.BlockSpec((TOK_TILE, 256), row),
                  pl.BlockSpec((TOK_TILE, 256), row),
                  pl.BlockSpec((TOK_TILE, 256), col(3)),
                  pl.BlockSpec((TOK_TILE, 256), row),
                  pl.BlockSpec((TOK_TILE, 4 * d), row)] + [full(w) for w in weights],
        out_specs=[pl.BlockSpec((TOK_TILE, d), row),
                   pl.BlockSpec((TOK_TILE, d), row),
                   pl.BlockSpec((TOK_TILE, LANES), row)],
        out_shape=[jax.ShapeDtypeStruct((n, d), F32),
                   jax.ShapeDtypeStruct((n, d), F32),
                   jax.ShapeDtypeStruct((n, LANES), F32)],
        scratch_shapes=[pltpu.VMEM((TOK_TILE + 2 * HALO, CONV_CH), F32)],
        compiler_params=_cparams(("parallel",)),
        name="merge",
    )(x, modv, conv_z, conv_z, conv_z, hg_f, hg_b, hg_pack, gla_f, gla_b, gla_pack, att_o, bgate,
      *weights)


def _gather_rows(idx_ref, n_rows, src_hbm, dst, sem):
    def body(r, carry):
        pltpu.make_async_copy(src_hbm.at[pl.ds(idx_ref[0, 0, r], 1)], dst.at[pl.ds(r, 1)], sem).start()
        return carry
    lax.fori_loop(0, n_rows, body, 0)


def _wait_rows(n_rows, src_hbm, dst, sem):
    def body(r, carry):
        pltpu.make_async_copy(src_hbm.at[pl.ds(0, 1)], dst.at[pl.ds(r, 1)], sem).wait()
        return carry
    lax.fori_loop(0, n_rows, body, 0)


def _expert_kernel(be_ref, nu_ref, rt_ref, rtn_ref, h_hbm, gate_ref, w1_ref, w3_ref, w2_ref, y_ref,
                   xbuf, sem, w1b, w3b, w2b):
    i = pl.program_id(0)
    nb = pl.num_programs(0)
    n_used = nu_ref[0]
    slot = i % 2
    rows = xbuf.shape[1]

    @pl.when(i == 0)
    def _():
        _gather_rows(rt_ref, rows, h_hbm, xbuf.at[0], sem.at[0])

    @pl.when(jnp.logical_and(i + 1 < nb, i + 1 < n_used))
    def _():
        _gather_rows(rtn_ref, rows, h_hbm, xbuf.at[1 - slot], sem.at[1 - slot])

    @pl.when(i < n_used)
    def _():
        first = jnp.logical_or(i == 0, be_ref[i] != be_ref[jnp.maximum(i - 1, 0)])

        @pl.when(first)
        def _():
            w1b[...] = w1_ref[0].astype(BF16)
            w3b[...] = w3_ref[0].astype(BF16)
            w2b[...] = w2_ref[0].astype(BF16)

        _wait_rows(rows, h_hbm, xbuf.at[slot], sem.at[slot])
        x = xbuf[slot].astype(BF16)
        hid = _silu(_dot(x, w1b[...])) * _dot(x, w3b[...])
        y_ref[...] = _dot(hid.astype(BF16), w2b[...]) * gate_ref[...]

    @pl.when(i >= n_used)
    def _():
        y_ref[...] = jnp.zeros_like(y_ref)


def experts(h2, row_tok, row_gate, block_expert, n_used, w1, w3, w2):
    n, d = h2.shape
    n_blocks = block_expert.shape[0]
    ff = w1.shape[-1]
    rt3 = row_tok.reshape(n_blocks, 1, MOE_BLOCK)
    grid_spec = pltpu.PrefetchScalarGridSpec(
        num_scalar_prefetch=2,
        grid=(n_blocks,),
        in_specs=[pl.BlockSpec((1, 1, MOE_BLOCK), lambda i, be, nu: (i, 0, 0), memory_space=pltpu.SMEM),
                  pl.BlockSpec((1, 1, MOE_BLOCK), lambda i, be, nu: (jnp.minimum(i + 1, n_blocks - 1), 0, 0),
                               memory_space=pltpu.SMEM),
                  pl.BlockSpec(memory_space=pl.ANY),
                  pl.BlockSpec((MOE_BLOCK, 1), lambda i, be, nu: (i, 0)),
                  pl.BlockSpec((1, d, ff), lambda i, be, nu: (be[i], 0, 0)),
                  pl.BlockSpec((1, d, ff), lambda i, be, nu: (be[i], 0, 0)),
                  pl.BlockSpec((1, ff, d), lambda i, be, nu: (be[i], 0, 0))],
        out_specs=pl.BlockSpec((MOE_BLOCK, d), lambda i, be, nu: (i, 0)),
        scratch_shapes=[pltpu.VMEM((2, MOE_BLOCK, d), F32),
                        pltpu.SemaphoreType.DMA((2,)),
                        pltpu.VMEM((d, ff), BF16),
                        pltpu.VMEM((d, ff), BF16),
                        pltpu.VMEM((ff, d), BF16)],
    )
    return pl.pallas_call(
        _expert_kernel,
        grid_spec=grid_spec,
        out_shape=jax.ShapeDtypeStruct((n_blocks * MOE_BLOCK, d), F32),
        compiler_params=_cparams(("arbitrary",)),
        name="experts",
    )(block_expert, n_used, rt3, rt3, h2, row_gate, w1, w3, w2)


def _combine_kernel(d_ref, dn_ref, x_ref, mod_ref, g_ref, y_hbm, o_ref, ybuf, sem, *, final):
    i = pl.program_id(0)
    nb = pl.num_programs(0)
    slot = i % 2
    rows = ybuf.shape[1]

    @pl.when(i == 0)
    def _():
        _gather_rows(d_ref, rows, y_hbm, ybuf.at[0], sem.at[0])

    @pl.when(i + 1 < nb)
    def _():
        _gather_rows(dn_ref, rows, y_hbm, ybuf.at[1 - slot], sem.at[1 - slot])

    _wait_rows(rows, y_hbm, ybuf.at[slot], sem.at[slot])
    tt = x_ref.shape[0]
    y = ybuf[slot, 0:tt, :] + ybuf[slot, tt:2 * tt, :]
    x = x_ref[...] + mod_ref[0, 5:6, :] * y
    if final:
        x = x * lax.rsqrt(jnp.mean(x * x, axis=-1, keepdims=True) + EPS) * g_ref[...]
    o_ref[...] = x


def combine(x, modv, ys, dest2, final_g, n_batch, n_ctx_tiles, final):
    n, d = x.shape
    nt = n // TOK_TILE
    tpb = nt // n_batch
    if final:
        lat = tpb - n_ctx_tiles
        steps = n_batch * lat
        tile = lambda s: (s // lat) * tpb + n_ctx_tiles + s % lat
        out_rows = steps * TOK_TILE
    else:
        steps = nt
        tile = lambda s: s
        out_rows = n

    def mod_idx(s):
        t = tile(s)
        return (t // tpb) * 2 + ((t % tpb) >= n_ctx_tiles).astype(jnp.int32)

    return pl.pallas_call(
        functools.partial(_combine_kernel, final=final),
        grid=(steps,),
        in_specs=[pl.BlockSpec((1, 1, 2 * TOK_TILE), lambda s: (tile(s), 0, 0), memory_space=pltpu.SMEM),
                  pl.BlockSpec((1, 1, 2 * TOK_TILE), lambda s: (tile(jnp.minimum(s + 1, steps - 1)), 0, 0),
                               memory_space=pltpu.SMEM),
                  pl.BlockSpec((TOK_TILE, d), lambda s: (tile(s), 0)),
                  pl.BlockSpec((1, N_MOD, d), lambda s: (mod_idx(s), 0, 0)),
                  pl.BlockSpec((1, d), lambda s: (0, 0)),
                  pl.BlockSpec(memory_space=pl.ANY)],
        out_specs=pl.BlockSpec((TOK_TILE, d), lambda s: (s, 0)),
        out_shape=jax.ShapeDtypeStruct((out_rows, d), F32),
        scratch_shapes=[pltpu.VMEM((2, 2 * TOK_TILE, d), F32),
                        pltpu.SemaphoreType.DMA((2,))],
        compiler_params=_cparams(("arbitrary",)),
        name="combine_final" if final else "combine",
    )(dest2, dest2, x, modv, final_g, ys)


def route(aff, router_b):
    n_tok = aff.shape[0]
    epg = N_EXPERTS // N_GROUPS
    aff = aff[:, :N_EXPERTS]
    grouped = (aff + router_b.astype(F32)).reshape(n_tok, N_GROUPS, epg)
    best_group = jnp.argmax(lax.top_k(grouped, TOP_K)[0].sum(-1), axis=-1)
    _, local = lax.top_k(jnp.take_along_axis(grouped, best_group[:, None, None], axis=1)[:, 0], TOP_K)
    expert = (best_group[:, None] * epg + local).astype(jnp.int32)
    gate = jnp.take_along_axis(aff, expert, axis=1)
    gate = gate / jnp.sum(gate, axis=-1, keepdims=True)

    n_slot = n_tok * TOP_K
    e_flat = expert.reshape(-1)
    onehot = (e_flat[:, None] == jnp.arange(N_EXPERTS, dtype=jnp.int32)[None, :]).astype(jnp.int32)
    csum = jnp.cumsum(onehot, axis=0)
    counts = csum[-1]
    rank = jnp.sum(onehot * csum, axis=1) - 1
    padded = (counts + MOE_BLOCK - 1) // MOE_BLOCK * MOE_BLOCK
    padded_end = jnp.cumsum(padded)
    padded_start = padded_end - padded
    dest = (padded_start[e_flat] + rank).astype(jnp.int32)
    n_blocks = -(-n_slot // MOE_BLOCK) + N_EXPERTS
    n_rows = n_blocks * MOE_BLOCK
    tok = (jnp.arange(n_slot, dtype=jnp.int32) // TOP_K)
    row_tok = jnp.zeros((n_rows,), jnp.int32).at[dest].set(tok)
    row_gate = jnp.zeros((n_rows,), F32).at[dest].set(gate.reshape(-1))
    block_expert = jnp.minimum(
        jnp.searchsorted(padded_end, jnp.arange(n_blocks, dtype=jnp.int32) * MOE_BLOCK, side='right'),
        N_EXPERTS - 1).astype(jnp.int32)
    n_used = (padded_end[-1] // MOE_BLOCK).astype(jnp.int32).reshape(1)
    return dest.reshape(n_tok, TOP_K), row_tok, row_gate.reshape(n_rows, 1), block_expert, n_used


def _rope_tables(ctx_len, n_lat):
    rows = n_lat // GRID_W
    row = np.repeat(np.arange(rows, dtype=np.float32), GRID_W)
    col = np.tile(np.arange(GRID_W, dtype=np.float32), rows)
    half = ATT_HD // 4
    inv = jnp.asarray(ROPE_BASE, F32) ** (-jnp.arange(half, dtype=F32) / half)
    ang_r = jnp.asarray(row)[:, None] * inv
    ang_c = jnp.asarray(col)[:, None] * inv
    cos64 = jnp.concatenate([jnp.cos(ang_r)] * 2 + [jnp.cos(ang_c)] * 2, axis=1)
    sin64 = jnp.concatenate([-jnp.sin(ang_r), jnp.sin(ang_r), -jnp.sin(ang_c), jnp.sin(ang_c)], axis=1)
    cos64 = jnp.concatenate([jnp.ones((ctx_len, ATT_HD), F32), cos64], axis=0)
    sin64 = jnp.concatenate([jnp.zeros((ctx_len, ATT_HD), F32), sin64], axis=0)
    return jnp.tile(cos64, (1, ATT_HEADS)), jnp.tile(sin64, (1, ATT_HEADS))


def _pack_w_in(w):
    gla = w[:, 1792:2592]
    gla = jnp.concatenate([gla[:, 0:512], gla[:, 544:800], gla[:, 512:544],
                           jnp.zeros((w.shape[0], 96), w.dtype)], axis=1)
    return jnp.concatenate([w[:, :1792], gla, w[:, 2592:]], axis=1).astype(BF16)


def kernel(x, c, ctx, c_ctx, hg_lb_logits, router_w, router_b, final_g, w_mod, b_mod, g_mix, g_ffn,
           w_in, conv_w, conv_b, conv_ln_g, conv_ln_b, hg_norm_g, gla_w2, gla_b2, gla_norm_g, att_sink,
           w_br_conv, w_br_hg, w_br_gla, w_br_att, w_out, moe_w1, moe_w3, moe_w2):
    n_batch, n_lat, d = x.shape
    ctx_len = ctx.shape[1]
    depth = w_in.shape[0]
    assert ctx_len % TOK_TILE == 0 and n_lat % TOK_TILE == 0 and n_lat % GRID_W == 0
    seq = ctx_len + n_lat
    n_ctx_tiles = ctx_len // TOK_TILE

    xs = jnp.concatenate([ctx, x], axis=1).reshape(n_batch * seq, d)

    c_rows = jnp.zeros((8, d), F32).at[:n_batch].set(c).at[n_batch].set(c_ctx)
    mods = modulation(c_rows, w_mod, b_mod).reshape(depth, 8, N_MOD, d)
    modv = jnp.stack([mods[:, n_batch] if j % 2 == 0 else mods[:, j // 2] for j in range(2 * n_batch)], axis=1)

    lb_sm = jax.nn.softmax(hg_lb_logits.astype(F32), axis=0)
    lower = jnp.cumsum(lb_sm, axis=0) - lb_sm[0]
    cos_t, sin_t = _rope_tables(ctx_len, n_lat)
    ones_bd = jnp.asarray(np.arange(256)[:, None] // 64 == np.arange(256)[None, :] // 64, BF16)
    rw = jnp.zeros((d, LANES), F32).at[:, :N_EXPERTS].set(router_w.astype(F32))

    for l in range(depth):
        w2p = jnp.zeros((LANES, 2 * GLA_K), F32)
        w2p = w2p.at[0:GLA_RANK, 0:GLA_K].set(gla_w2[l, 0]).at[GLA_RANK:2 * GLA_RANK, GLA_K:].set(gla_w2[l, 1])
        b2p = gla_b2[l].reshape(1, 2 * GLA_K)
        conv_z, hg_pack, gla_pack, att_pack, bgate = inproj(
            xs, modv[l], g_mix[l].reshape(1, d), _pack_w_in(w_in[l]), lower[l], w2p, b2p, cos_t, sin_t, n_batch, n_ctx_tiles)

        hg_f = gated_scan(hg_pack, (0, 1, 3, 4), HG_W, n_batch, n_ctx_tiles, rev=False)
        hg_b = gated_scan(hg_pack, (0, 2, 3, 5), HG_W, n_batch, n_ctx_tiles, rev=True)
        gla_f = gated_scan(gla_pack, (0, 1, 2, 2), GLA_K, n_batch, n_ctx_tiles, rev=False)
        gla_b = gated_scan(gla_pack, (0, 1, 2, 3), GLA_K, n_batch, n_ctx_tiles, rev=True)
        att_o = attention(att_pack, att_sink[l].astype(F32), n_batch, ctx_len)

        lw = dict(conv_w=jnp.zeros((32, CONV_CH), F32).at[:CONV_K].set(conv_w[l]),
                  conv_b=conv_b[l].reshape(1, -1), conv_ln_g=conv_ln_g[l].reshape(1, -1),
                  conv_ln_b=conv_ln_b[l].reshape(1, -1), hg_norm_g=hg_norm_g[l].reshape(1, -1),
                  gla_norm_g=gla_norm_g[l].reshape(1, -1), ones_bd=ones_bd,
                  w_br_conv=w_br_conv[l].astype(BF16), w_br_hg=w_br_hg[l].astype(BF16),
                  w_br_gla=w_br_gla[l].astype(BF16), w_br_att=w_br_att[l].astype(BF16),
                  w_out=w_out[l].astype(BF16), g_ffn=g_ffn[l].reshape(1, d), router_w=rw)
        x_new, h2, aff = merge(xs, modv[l], conv_z, hg_pack, hg_f, hg_b, gla_pack, gla_f, gla_b, att_o,
                               bgate, lw, n_batch, n_ctx_tiles)

        dest, row_tok, row_gate, block_expert, n_used = route(aff, router_b)
        ys = experts(h2, row_tok, row_gate, block_expert, n_used, moe_w1[l], moe_w3[l], moe_w2[l])
        nt = n_batch * seq // TOK_TILE
        dest2 = dest.reshape(nt, TOK_TILE, TOP_K).transpose(0, 2, 1).reshape(nt, 1, TOP_K * TOK_TILE)
        final = l == depth - 1
        xs = combine(x_new, modv[l], ys, dest2, final_g.reshape(1, d), n_batch, n_ctx_tiles, final)

    return xs.reshape(n_batch, n_lat, d)
```

```python
import functools

import numpy as np
import jax
import jax.numpy as jnp
from jax import lax
from jax.experimental import pallas as pl
from jax.experimental.pallas import tpu as pltpu

F32 = jnp.float32
BF16 = jnp.bfloat16

EPS = 1e-6
NEG = -1e30
TINY = 1e-30
N_MOD = 6
CONV_CH = 256
CONV_K = 31
HG_HEADS = 4
HG_W = 256
GLA_HEADS = 4
GLA_K = 128
GLA_V = 256
GLA_RANK = 16
GLA_TAU = 16.0
ATT_HEADS = 4
ATT_KV_HEADS = 2
ATT_HD = 64
ATT_BLOCK = 128
GRID_W = 64
ROPE_BASE = 10000.0
N_EXPERTS = 16
N_GROUPS = 4
TOP_K = 2
MOE_BLOCK = 256

LANES = 128
TOK_TILE = 256
SCAN_BLOCK = 16
HALO = 16
VMEM_LIMIT = 56 * 1024 * 1024

W_CONV = (0, 512)
W_HG = (512, 1792)
W_GLA = (1792, 2688)
W_ATT = (2688, 3200)
W_BG = (3200, 7296)
W_IN_PACKED = 7296


def _cparams(sem):
    return pltpu.CompilerParams(dimension_semantics=sem, vmem_limit_bytes=VMEM_LIMIT)


def _dot(a, b):
    return jnp.dot(a, b, preferred_element_type=F32)


def _split(a):
    hi = a.astype(BF16)
    lo = (a - hi.astype(F32)).astype(BF16)
    return hi, lo


def _dot_f32(a, b):
    ah, al = _split(a)
    bh, bl = _split(b)
    return _dot(ah, bh) + _dot(ah, bl) + _dot(al, bh)


def _sigmoid(x):
    return 1.0 / (1.0 + jnp.exp(-x))


def _silu(x):
    return x * _sigmoid(x)


def _mod_kernel(c_ref, w_ref, b_ref, o_ref):
    c = c_ref[...]
    o_ref[0] = _dot_f32(_silu(c), w_ref[0]) + b_ref[0]


def modulation(c_rows, w_mod, b_mod):
    depth, d, six_d = w_mod.shape
    nblk = six_d // d
    return pl.pallas_call(
        _mod_kernel,
        grid=(depth, nblk),
        in_specs=[pl.BlockSpec((8, d), lambda l, j: (0, 0)),
                  pl.BlockSpec((1, d, d), lambda l, j: (l, 0, j)),
                  pl.BlockSpec((1, 1, d), lambda l, j: (l, 0, j))],
        out_specs=pl.BlockSpec((1, 8, d), lambda l, j: (l, 0, j)),
        out_shape=jax.ShapeDtypeStruct((depth, 8, six_d), F32),
        compiler_params=_cparams(("parallel", "parallel")),
        name="modulation",
    )(c_rows, w_mod, b_mod.reshape(depth, 1, six_d))


def _modnorm(x, g, shift, scale):
    y = x * lax.rsqrt(jnp.mean(x * x, axis=-1, keepdims=True) + EPS)
    return (y * g) * (1.0 + scale) + shift


def _inproj_kernel(x_ref, mod_ref, g_ref, w_ref, lb_ref, w2_ref, b2_ref, cos_ref, sin_ref,
                   conv_ref, hg_ref, gla_ref, att_ref, bg_ref):
    x = x_ref[...]
    h = _modnorm(x, g_ref[...], mod_ref[0, 0:1, :], mod_ref[0, 1:2, :]).astype(BF16)

    p = _dot(h, w_ref[:, W_CONV[0]:W_CONV[1]])
    conv_ref[...] = p[:, :CONV_CH] * _sigmoid(p[:, CONV_CH:])

    p = _dot(h, w_ref[:, W_HG[0]:W_HG[1]])
    hg_ref[:, 0:256] = p[:, 0:256]
    hg_ref[:, 768:1024] = p[:, 768:1024]
    hg_ref[:, 1536:1792] = _silu(p[:, 1024:1280])
    for d in range(2):
        z = p[:, 256 * (d + 1):256 * (d + 2)]
        lb = lb_ref[d:d + 1, :]
        hg_ref[:, 256 * (d + 1):256 * (d + 2)] = (1.0 - lb) * _sigmoid(-z)
        hg_ref[:, 256 * (d + 4):256 * (d + 5)] = jnp.maximum(lb + (1.0 - lb) * _sigmoid(z), TINY)

    p = _dot(h, w_ref[:, W_GLA[0]:W_GLA[1]])
    gla_ref[:, 0:128] = p[:, 0:128] * (float(GLA_K // GLA_HEADS) ** -0.5)
    gla_ref[:, 128:256] = p[:, 128:256]
    gla_ref[:, 512:768] = p[:, 256:512]
    gla_ref[:, 768:1024] = _silu(p[:, 512:768])
    u = _dot_f32(p[:, 768:896], w2_ref[...]) + b2_ref[...]
    log_sig = jnp.minimum(u, 0.0) - jnp.log(1.0 + jnp.exp(-jnp.abs(u)))
    gla_ref[:, 256:512] = jnp.exp(log_sig * (1.0 / GLA_TAU))

    p = _dot(h, w_ref[:, W_ATT[0]:W_ATT[1]])
    cos = cos_ref[...]
    sin = sin_ref[...]
    def rope(v, width):
        lane = lax.broadcasted_iota(jnp.int32, v.shape, 1)
        partner = jnp.where((lane & 31) < 16,
                            pltpu.roll(v, width - 16, axis=1), pltpu.roll(v, 16, axis=1))
        return v * cos[:, :width] + partner * sin[:, :width]

    att_ref[:, 0:256] = rope(p[:, 0:256], 256) * (float(ATT_HD) ** -0.5)
    att_ref[:, 256:384] = rope(p[:, 256:384], 128)
    att_ref[:, 384:512] = p[:, 384:512]

    p = _dot(h, w_ref[:, W_BG[0]:W_BG[1]])
    bg_ref[...] = _sigmoid(p).astype(BF16)


def inproj(x, modv, g_mix, w_in_p, lb, w2p, b2p, cos_t, sin_t, n_batch, n_ctx_tiles):
    n, d = x.shape
    nt = n // TOK_TILE
    tiles_per_batch = nt // n_batch

    def mod_idx(i):
        b = i // tiles_per_batch
        return b * 2 + ((i % tiles_per_batch) >= n_ctx_tiles).astype(jnp.int32)

    const = lambda i: (0, 0)
    row = lambda i: (i, 0)
    seq = lambda i: (i % tiles_per_batch, 0)
    outs = pl.pallas_call(
        _inproj_kernel,
        grid=(nt,),
        in_specs=[pl.BlockSpec((TOK_TILE, d), row),
                  pl.BlockSpec((1, N_MOD, d), lambda i: (mod_idx(i), 0, 0)),
                  pl.BlockSpec((1, d), const),
                  pl.BlockSpec((d, W_IN_PACKED), const, pipeline_mode=pl.Buffered(1)),
                  pl.BlockSpec((2, HG_W), const),
                  pl.BlockSpec((LANES, 2 * GLA_K), const),
                  pl.BlockSpec((1, 2 * GLA_K), const),
                  pl.BlockSpec((TOK_TILE, 256), seq),
                  pl.BlockSpec((TOK_TILE, 256), seq)],
        out_specs=[pl.BlockSpec((TOK_TILE, 256), row),
                   pl.BlockSpec((TOK_TILE, 1792), row),
                   pl.BlockSpec((TOK_TILE, 1024), row),
                   pl.BlockSpec((TOK_TILE, 512), row),
                   pl.BlockSpec((TOK_TILE, 4096), row)],
        out_shape=[jax.ShapeDtypeStruct((n, 256), F32),
                   jax.ShapeDtypeStruct((n, 1792), F32),
                   jax.ShapeDtypeStruct((n, 1024), F32),
                   jax.ShapeDtypeStruct((n, 512), F32),
                   jax.ShapeDtypeStruct((n, 4096), BF16)],
        compiler_params=_cparams(("parallel",)),
        name="inproj",
    )(x, modv, g_mix, w_in_p, lb, w2p, b2p, cos_t, sin_t)
    return outs


def _scan_kernel(q_ref, k_ref, v_ref, f_ref, sel_ref, msk_ref, o_ref, s_ref, fp_ref, kp_ref, vp_ref,
                 *, rev):
    c = SCAN_BLOCK
    tt, kl = q_ref.shape
    nb = tt // c
    pad = c

    @pl.when(pl.program_id(1) == 0)
    def _():
        s_ref[...] = jnp.zeros_like(s_ref)
        fp_ref[...] = jnp.ones_like(fp_ref)
        kp_ref[...] = jnp.zeros_like(kp_ref)
        vp_ref[...] = jnp.zeros_like(vp_ref)

    q = q_ref[...]
    f = f_ref[...]
    fp_ref[pad:pad + tt, :] = f
    kp_ref[pad:pad + tt, :] = k_ref[...]
    vp_ref[pad:pad + tt, :] = v_ref[...]

    def shifted(ref, d):
        start = pad + d if rev else pad - d
        return ref[start:start + tt, :]

    r = lax.broadcasted_iota(jnp.int32, (tt, kl), 0) & (c - 1)
    if rev:
        r = (c - 1) - r
    rr = (c - 1) - r

    def roll_rows(x, d):
        d = -d if rev else d
        return pltpu.roll(x, d % tt, axis=0)

    a = f
    g = f
    s = 1
    while s < c:
        a = a * jnp.where(r >= s, roll_rows(a, s), 1.0)
        g = g * jnp.where(rr >= s, roll_rows(g, -s), 1.0)
        s *= 2
    suffix = jnp.where(rr >= 1, roll_rows(g, -1), 1.0)

    sel = sel_ref[...]
    p = q
    acc = jnp.zeros((tt, 256), F32)
    for d in range(c):
        if d > 0:
            p = p * shifted(fp_ref, d - 1)
        term = jnp.where(r >= d, p * shifted(kp_ref, d), 0.0).astype(BF16)
        w = _dot(term, sel)
        acc = acc + w * shifted(vp_ref, d)

    qt = (q * a).astype(BF16)
    kh = (k_ref[...] * suffix).astype(BF16)
    vb = v_ref[...].astype(BF16)
    last = 0 if rev else c - 1
    gam = jnp.concatenate([a[j * c + last:j * c + last + 1, :] for j in range(nb)], axis=0)
    gam_t = gam.T
    msk = msk_ref[...]
    st = s_ref[...]
    for jj in range(nb):
        j = nb - 1 - jj if rev else jj
        lo = j * c
        o_ref[lo:lo + c, :] = acc[lo:lo + c, :] + _dot(qt[lo:lo + c, :], st.astype(BF16))
        kv = lax.dot_general(kh[lo:lo + c, :], vb[lo:lo + c, :], (((0,), (0,)), ((), ())),
                             preferred_element_type=F32)
        st = gam_t[:, j:j + 1] * st + kv * msk
    s_ref[...] = st


def gated_scan(pack, cols, kl, n_batch, n_ctx_tiles, rev):
    n = pack.shape[0]
    nt = n // TOK_TILE
    tpb = nt // n_batch
    qc, kc, vc, fc = cols

    def tile(i):
        if not rev:
            return i
        return jnp.where(i < n_ctx_tiles, n_ctx_tiles - 1 - i, tpb - 1 - (i - n_ctx_tiles))

    heads = 4
    dk = kl // heads
    sel = (np.arange(kl)[:, None] // dk == np.arange(256)[None, :] // 64)
    sel_b = jnp.asarray(sel, BF16)
    msk = jnp.asarray(sel, F32)
    rows = lambda col: (lambda b, i: (b * tpb + tile(i), col))
    const = lambda b, i: (0, 0)
    pad_rows = TOK_TILE + 2 * SCAN_BLOCK
    return pl.pallas_call(
        functools.partial(_scan_kernel, rev=rev),
        grid=(n_batch, tpb),
        in_specs=[pl.BlockSpec((TOK_TILE, kl), rows(qc)),
                  pl.BlockSpec((TOK_TILE, kl), rows(kc)),
                  pl.BlockSpec((TOK_TILE, 256), rows(vc)),
                  pl.BlockSpec((TOK_TILE, kl), rows(fc)),
                  pl.BlockSpec((kl, 256), const),
                  pl.BlockSpec((kl, 256), const)],
        out_specs=pl.BlockSpec((TOK_TILE, 256), rows(0)),
        out_shape=jax.ShapeDtypeStruct((n, 256), F32),
        scratch_shapes=[pltpu.VMEM((kl, 256), F32),
                        pltpu.VMEM((pad_rows, kl), F32),
                        pltpu.VMEM((pad_rows, kl), F32),
                        pltpu.VMEM((pad_rows, 256), F32)],
        compiler_params=_cparams(("parallel", "arbitrary")),
        name="scan_rev" if rev else "scan_fwd",
    )(pack, pack, pack, pack, sel_b, msk)


def _attn_kernel(sink_ref, q_ref, kc_ref, vc_ref, k0_ref, k1_ref, k2_ref, v0_ref, v1_ref, v2_ref,
                 o_ref, *, ctx_blocks, blocks_per_batch):
    n = pl.program_id(1)
    blk = ATT_BLOCK
    lane = lax.broadcasted_iota(jnp.int32, (blk, LANES), 1)
    low = lane < ATT_HD
    qi = lax.broadcasted_iota(jnp.int32, (blk, blk), 0)
    ki = lax.broadcasted_iota(jnp.int32, (blk, blk), 1)
    is_lat = n >= ctx_blocks
    band_ok = [
        jnp.logical_and(jnp.logical_and(is_lat, n - 1 >= ctx_blocks), qi <= ki),
        jnp.logical_and(is_lat, qi >= 0),
        jnp.logical_and(jnp.logical_and(is_lat, n + 1 < blocks_per_batch), ki <= qi),
    ]

    def dup(x, hk):
        xr = pltpu.roll(x, ATT_HD, axis=1)
        lo_x = lax.broadcasted_iota(jnp.int32, x.shape, 1) < ATT_HD
        return (jnp.where(lo_x, x, xr) if hk == 0 else jnp.where(lo_x, xr, x)).astype(BF16)

    q = q_ref[...]
    keys = [kc_ref[...], k0_ref[...], k1_ref[...], k2_ref[...]]
    vals = [vc_ref[...], v0_ref[...], v1_ref[...], v2_ref[...]]
    nt_dims = (((1,), (1,)), ((), ()))
    halves = []
    for hk in range(ATT_KV_HEADS):
        kd = [dup(x, hk) for x in keys]
        vd = [dup(x, hk) for x in vals]
        qh = q[:, hk * LANES:(hk + 1) * LANES]
        outs = []
        for g in range(2):
            qg = jnp.where(low if g == 0 else jnp.logical_not(low), qh, 0.0).astype(BF16)
            s = [lax.dot_general(qg, kx, nt_dims, preferred_element_type=F32) for kx in kd]
            for t in range(3):
                s[t + 1] = jnp.where(band_ok[t], s[t + 1], NEG)
            sink = sink_ref[hk * 2 + g]
            m = jnp.maximum(sink, s[0].max(axis=-1, keepdims=True))
            for t in range(3):
                m = jnp.maximum(m, s[t + 1].max(axis=-1, keepdims=True))
            den = jnp.exp(sink - m)
            o = jnp.zeros((blk, LANES), F32)
            for t in range(4):
                e = jnp.exp(s[t] - m)
                den = den + e.sum(axis=-1, keepdims=True)
                o = o + _dot(e.astype(BF16), vd[t])
            outs.append(o / den)
        halves.append(jnp.where(low, outs[0], outs[1]))
    o_ref[:, 0:LANES] = halves[0]
    o_ref[:, LANES:2 * LANES] = halves[1]


def attention(att_pack, sink, n_batch, ctx_len):
    n = att_pack.shape[0]
    bpb = n // n_batch // ATT_BLOCK
    cb = ctx_len // ATT_BLOCK

    def band(off, col):
        def idx(b, i, *_):
            m = jnp.clip(i + off, cb, bpb - 1)
            return (b * bpb + m, col)
        return pl.BlockSpec((ATT_BLOCK, LANES), idx)

    def ctx(col):
        return pl.BlockSpec((ctx_len, LANES), lambda b, i, *_: (b * (bpb * ATT_BLOCK // ctx_len), col))

    grid_spec = pltpu.PrefetchScalarGridSpec(
        num_scalar_prefetch=1,
        grid=(n_batch, bpb),
        in_specs=[pl.BlockSpec((ATT_BLOCK, 256), lambda b, i, *_: (b * bpb + i, 0)),
                  ctx(2), ctx(3),
                  band(-1, 2), band(0, 2), band(1, 2),
                  band(-1, 3), band(0, 3), band(1, 3)],
        out_specs=pl.BlockSpec((ATT_BLOCK, 256), lambda b, i, *_: (b * bpb + i, 0)),
    )
    return pl.pallas_call(
        functools.partial(_attn_kernel, ctx_blocks=cb, blocks_per_batch=bpb),
        grid_spec=grid_spec,
        out_shape=jax.ShapeDtypeStruct((n, 256), F32),
        compiler_params=_cparams(("parallel", "parallel")),
        name="attention",
    )(sink, *([att_pack] * 9))


def _head_norm(o, ones_bd, g):
    sq = o * o
    hi, lo = _split(sq)
    ms = (_dot(hi, ones_bd) + _dot(lo, ones_bd)) * (1.0 / 64.0)
    return o * lax.rsqrt(ms + EPS) * g


def _route_select(aff, bias):
    lane = lax.broadcasted_iota(jnp.int32, aff.shape, 1)
    epg = N_EXPERTS // N_GROUPS
    pos = lane & (epg - 1)
    v = aff + bias

    def nxt(x, o):
        return pltpu.roll(x, LANES - o, axis=1)

    def prv(x, o):
        return pltpu.roll(x, o, axis=1)

    beaten = jnp.zeros(aff.shape, jnp.int32)
    for o in range(1, epg):
        beaten = beaten + jnp.where(jnp.logical_and(pos + o < epg, nxt(v, o) > v), 1, 0)
        beaten = beaten + jnp.where(jnp.logical_and(pos >= o, prv(v, o) >= v), 1, 0)
    top2 = beaten < TOP_K
    t = jnp.where(top2, v, 0.0)
    score = t
    for o in range(1, epg):
        score = score + jnp.where(pos + o < epg, nxt(t, o), 0.0) + jnp.where(pos >= o, prv(t, o), 0.0)
    worse = jnp.zeros(aff.shape, jnp.int32)
    for o in range(epg, N_EXPERTS, epg):
        worse = worse + jnp.where(jnp.logical_and(lane + o < N_EXPERTS, nxt(score, o) > score), 1, 0)
        worse = worse + jnp.where(jnp.logical_and(lane >= o, prv(score, o) >= score), 1, 0)
    sel = jnp.logical_and(jnp.logical_and(lane < N_EXPERTS, worse == 0), top2)
    picked = jnp.where(sel, aff, 0.0)
    gate = picked / jnp.sum(picked, axis=-1, keepdims=True)
    return gate, sel


def _merge_kernel(x_ref, mod_ref, zp_ref, z_ref, zn_ref, hgf_ref, hgb_ref, hgg_ref, glf_ref, glb_ref,
                  glg_ref, att_ref, bg_ref, cw_ref, cb_ref, lng_ref, lnb_ref, hgn_ref, gln_ref,
                  ones_ref, wbc_ref, wbh_ref, wbg_ref, wba_ref, wo_ref, gf_ref, rw_ref, rb_ref, tri_ref,
                  xo_ref, h2_ref, gate_ref, rank_ref, zs_ref, cnt_ref, *, tiles_per_batch, n_ctx_tiles):
    i = pl.program_id(0)
    tt = x_ref.shape[0]
    ti = i % tiles_per_batch
    has_prev = jnp.logical_and(ti != 0, ti != n_ctx_tiles)
    has_next = jnp.logical_and(ti != n_ctx_tiles - 1, ti != tiles_per_batch - 1)

    zs_ref[0:HALO, :] = jnp.where(has_prev, zp_ref[...], 0.0)
    zs_ref[HALO:HALO + tt, :] = z_ref[...]
    zs_ref[HALO + tt:HALO + tt + HALO, :] = jnp.where(has_next, zn_ref[...], 0.0)
    acc = jnp.zeros((tt, CONV_CH), F32) + cb_ref[...]
    for j in range(CONV_K):
        off = HALO - CONV_K // 2 + j
        acc = acc + zs_ref[off:off + tt, :] * cw_ref[j:j + 1, :]
    mu = jnp.mean(acc, axis=-1, keepdims=True)
    cen = acc - mu
    var = jnp.mean(cen * cen, axis=-1, keepdims=True)
    conv_y = _silu(cen * lax.rsqrt(var + EPS) * lng_ref[...] + lnb_ref[...])

    ones_bd = ones_ref[...]
    hg_y = _head_norm(hgf_ref[...] + hgb_ref[...], ones_bd, hgn_ref[...]) * hgg_ref[...]
    gla_y = _head_norm(glf_ref[...] + glb_ref[...], ones_bd, gln_ref[...]) * glg_ref[...]

    d = x_ref.shape[1]
    merged = bg_ref[:, 0:d].astype(F32) * _dot(conv_y.astype(BF16), wbc_ref[...])
    merged = merged + bg_ref[:, d:2 * d].astype(F32) * _dot(hg_y.astype(BF16), wbh_ref[...])
    merged = merged + bg_ref[:, 2 * d:3 * d].astype(F32) * _dot(gla_y.astype(BF16), wbg_ref[...])
    merged = merged + bg_ref[:, 3 * d:4 * d].astype(F32) * _dot(att_ref[...].astype(BF16), wba_ref[...])
    mix = _dot(merged.astype(BF16), wo_ref[...])

    x_new = x_ref[...] + mod_ref[0, 2:3, :] * mix
    xo_ref[...] = x_new
    h2 = _modnorm(x_new, gf_ref[...], mod_ref[0, 3:4, :], mod_ref[0, 4:5, :])
    h2_ref[...] = h2
    gate, sel = _route_select(_sigmoid(_dot_f32(h2, rw_ref[...])), rb_ref[...])
    gate_ref[...] = gate

    @pl.when(i == 0)
    def _():
        cnt_ref[...] = jnp.zeros_like(cnt_ref)

    sel_f = sel.astype(F32)
    incl = _dot(tri_ref[...], sel_f.astype(BF16)) + cnt_ref[...]
    rank_ref[...] = jnp.where(sel, incl - 1.0, -1.0).astype(jnp.int32)
    cnt_ref[...] = incl[tt - 1:tt, :]


def merge(x, modv, conv_z, hg_pack, hg_f, hg_b, gla_pack, gla_f, gla_b, att_o, bgate, lw, n_batch,
          n_ctx_tiles):
    n, d = x.shape
    nt = n // TOK_TILE
    tpb = nt // n_batch
    hpt = TOK_TILE // HALO
    n_halo = n // HALO

    def mod_idx(i):
        return (i // tpb) * 2 + ((i % tpb) >= n_ctx_tiles).astype(jnp.int32)

    row = lambda i: (i, 0)
    const = lambda i: (0, 0)
    col = lambda c: (lambda i: (i, c))
    full = lambda a: pl.BlockSpec(a.shape, const)
    weights = [lw["conv_w"], lw["conv_b"], lw["conv_ln_g"], lw["conv_ln_b"], lw["hg_norm_g"],
               lw["gla_norm_g"], lw["ones_bd"], lw["w_br_conv"], lw["w_br_hg"], lw["w_br_gla"],
               lw["w_br_att"], lw["w_out"], lw["g_ffn"], lw["router_w"], lw["router_b"], lw["tri"]]
    return pl.pallas_call(
        functools.partial(_merge_kernel, tiles_per_batch=tpb, n_ctx_tiles=n_ctx_tiles),
        grid=(nt,),
        in_specs=[pl.BlockSpec((TOK_TILE, d), row),
                  pl.BlockSpec((1, N_MOD, d), lambda i: (mod_idx(i), 0, 0)),
                  pl.BlockSpec((HALO, CONV_CH), lambda i: (jnp.maximum(i * hpt - 1, 0), 0)),
                  pl.BlockSpec((TOK_TILE, CONV_CH), row),
                  pl.BlockSpec((HALO, CONV_CH), lambda i: (jnp.minimum((i + 1) * hpt, n_halo - 1), 0)),
                  pl.BlockSpec((TOK_TILE, 256), row),
                  pl.BlockSpec((TOK_TILE, 256), row),
                  pl.BlockSpec((TOK_TILE, 256), col(6)),
                  pl.BlockSpec((TOK_TILE, 256), row),
                  pl.BlockSpec((TOK_TILE, 256), row),
                  pl.BlockSpec((TOK_TILE, 256), col(3)),
                  pl.BlockSpec((TOK_TILE, 256), row),
                  pl.BlockSpec((TOK_TILE, 4 * d), row)] + [full(w) for w in weights],
        out_specs=[pl.BlockSpec((TOK_TILE, d), row),
                   pl.BlockSpec((TOK_TILE, d), row),
                   pl.BlockSpec((TOK_TILE, LANES), row),
                   pl.BlockSpec((TOK_TILE, LANES), row)],
        out_shape=[jax.ShapeDtypeStruct((n, d), F32),
                   jax.ShapeDtypeStruct((n, d), F32),
                   jax.ShapeDtypeStruct((n, LANES), F32),
                   jax.ShapeDtypeStruct((n, LANES), jnp.int32)],
        scratch_shapes=[pltpu.VMEM((TOK_TILE + 2 * HALO, CONV_CH), F32),
                        pltpu.VMEM((1, LANES), F32)],
        compiler_params=_cparams(("arbitrary",)),
        name="merge",
    )(x, modv, conv_z, conv_z, conv_z, hg_f, hg_b, hg_pack, gla_f, gla_b, gla_pack, att_o, bgate,
      *weights)


def _gather_rows(idx_ref, n_rows, src_hbm, dst, sem):
    def body(r, carry):
        pltpu.make_async_copy(src_hbm.at[pl.ds(idx_ref[0, 0, r], 1)], dst.at[pl.ds(r, 1)], sem).start()
        return carry
    lax.fori_loop(0, n_rows, body, 0)


def _wait_rows(n_rows, src_hbm, dst, sem):
    def body(r, carry):
        pltpu.make_async_copy(src_hbm.at[pl.ds(0, 1)], dst.at[pl.ds(r, 1)], sem).wait()
        return carry
    lax.fori_loop(0, n_rows, body, 0)


def _expert_kernel(be_ref, nu_ref, rt_ref, rtn_ref, h_hbm, gate_ref, w1_ref, w3_ref, w2_ref, y_ref,
                   xbuf, sem, w1b, w3b, w2b):
    i = pl.program_id(0)
    nb = pl.num_programs(0)
    n_used = nu_ref[0]
    slot = i % 2
    rows = xbuf.shape[1]

    @pl.when(i == 0)
    def _():
        _gather_rows(rt_ref, rows, h_hbm, xbuf.at[0], sem.at[0])

    @pl.when(jnp.logical_and(i + 1 < nb, i + 1 < n_used))
    def _():
        _gather_rows(rtn_ref, rows, h_hbm, xbuf.at[1 - slot], sem.at[1 - slot])

    @pl.when(i < n_used)
    def _():
        first = jnp.logical_or(i == 0, be_ref[i] != be_ref[jnp.maximum(i - 1, 0)])

        @pl.when(first)
        def _():
            w1b[...] = w1_ref[0].astype(BF16)
            w3b[...] = w3_ref[0].astype(BF16)
            w2b[...] = w2_ref[0].astype(BF16)

        _wait_rows(rows, h_hbm, xbuf.at[slot], sem.at[slot])
        x = xbuf[slot].astype(BF16)
        hid = _silu(_dot(x, w1b[...])) * _dot(x, w3b[...])
        y_ref[...] = _dot(hid.astype(BF16), w2b[...]) * gate_ref[...]

    @pl.when(i >= n_used)
    def _():
        y_ref[...] = jnp.zeros_like(y_ref)


def experts(h2, row_tok, row_gate, block_expert, n_used, w1, w3, w2):
    n, d = h2.shape
    n_blocks = block_expert.shape[0]
    ff = w1.shape[-1]
    rt3 = row_tok.reshape(n_blocks, 1, MOE_BLOCK)
    grid_spec = pltpu.PrefetchScalarGridSpec(
        num_scalar_prefetch=2,
        grid=(n_blocks,),
        in_specs=[pl.BlockSpec((1, 1, MOE_BLOCK), lambda i, be, nu: (i, 0, 0), memory_space=pltpu.SMEM),
                  pl.BlockSpec((1, 1, MOE_BLOCK), lambda i, be, nu: (jnp.minimum(i + 1, n_blocks - 1), 0, 0),
                               memory_space=pltpu.SMEM),
                  pl.BlockSpec(memory_space=pl.ANY),
                  pl.BlockSpec((MOE_BLOCK, 1), lambda i, be, nu: (i, 0)),
                  pl.BlockSpec((1, d, ff), lambda i, be, nu: (be[i], 0, 0)),
                  pl.BlockSpec((1, d, ff), lambda i, be, nu: (be[i], 0, 0)),
                  pl.BlockSpec((1, ff, d), lambda i, be, nu: (be[i], 0, 0))],
        out_specs=pl.BlockSpec((MOE_BLOCK, d), lambda i, be, nu: (i, 0)),
        scratch_shapes=[pltpu.VMEM((2, MOE_BLOCK, d), F32),
                        pltpu.SemaphoreType.DMA((2,)),
                        pltpu.VMEM((d, ff), BF16),
                        pltpu.VMEM((d, ff), BF16),
                        pltpu.VMEM((ff, d), BF16)],
    )
    return pl.pallas_call(
        _expert_kernel,
        grid_spec=grid_spec,
        out_shape=jax.ShapeDtypeStruct((n_blocks * MOE_BLOCK, d), F32),
        compiler_params=_cparams(("arbitrary",)),
        name="experts",
    )(block_expert, n_used, rt3, rt3, h2, row_gate, w1, w3, w2)


def _combine_kernel(d_ref, dn_ref, x_ref, mod_ref, g_ref, y_hbm, o_ref, ybuf, sem, *, final):
    i = pl.program_id(0)
    nb = pl.num_programs(0)
    slot = i % 2
    rows = ybuf.shape[1]

    @pl.when(i == 0)
    def _():
        _gather_rows(d_ref, rows, y_hbm, ybuf.at[0], sem.at[0])

    @pl.when(i + 1 < nb)
    def _():
        _gather_rows(dn_ref, rows, y_hbm, ybuf.at[1 - slot], sem.at[1 - slot])

    _wait_rows(rows, y_hbm, ybuf.at[slot], sem.at[slot])
    tt = x_ref.shape[0]
    y = ybuf[slot, 0:tt, :] + ybuf[slot, tt:2 * tt, :]
    x = x_ref[...] + mod_ref[0, 5:6, :] * y
    if final:
        x = x * lax.rsqrt(jnp.mean(x * x, axis=-1, keepdims=True) + EPS) * g_ref[...]
    o_ref[...] = x


def combine(x, modv, ys, dest2, final_g, n_batch, n_ctx_tiles, final):
    n, d = x.shape
    nt = n // TOK_TILE
    tpb = nt // n_batch
    if final:
        lat = tpb - n_ctx_tiles
        steps = n_batch * lat
        tile = lambda s: (s // lat) * tpb + n_ctx_tiles + s % lat
        out_rows = steps * TOK_TILE
    else:
        steps = nt
        tile = lambda s: s
        out_rows = n

    def mod_idx(s):
        t = tile(s)
        return (t // tpb) * 2 + ((t % tpb) >= n_ctx_tiles).astype(jnp.int32)

    return pl.pallas_call(
        functools.partial(_combine_kernel, final=final),
        grid=(steps,),
        in_specs=[pl.BlockSpec((1, 1, 2 * TOK_TILE), lambda s: (tile(s), 0, 0), memory_space=pltpu.SMEM),
                  pl.BlockSpec((1, 1, 2 * TOK_TILE), lambda s: (tile(jnp.minimum(s + 1, steps - 1)), 0, 0),
                               memory_space=pltpu.SMEM),
                  pl.BlockSpec((TOK_TILE, d), lambda s: (tile(s), 0)),
                  pl.BlockSpec((1, N_MOD, d), lambda s: (mod_idx(s), 0, 0)),
                  pl.BlockSpec((1, d), lambda s: (0, 0)),
                  pl.BlockSpec(memory_space=pl.ANY)],
        out_specs=pl.BlockSpec((TOK_TILE, d), lambda s: (s, 0)),
        out_shape=jax.ShapeDtypeStruct((out_rows, d), F32),
        scratch_shapes=[pltpu.VMEM((2, 2 * TOK_TILE, d), F32),
                        pltpu.SemaphoreType.DMA((2,))],
        compiler_params=_cparams(("arbitrary",)),
        name="combine_final" if final else "combine",
    )(dest2, dest2, x, modv, final_g, ys)


def route(gate, rank):
    n_tok = gate.shape[0]
    gate = gate[:, :N_EXPERTS]
    rank = rank[:, :N_EXPERTS]
    sel = rank >= 0
    counts = jnp.max(rank, axis=0) + 1
    padded = (counts + MOE_BLOCK - 1) // MOE_BLOCK * MOE_BLOCK
    padded_end = jnp.cumsum(padded)
    padded_start = padded_end - padded
    n_slot = n_tok * TOP_K
    n_blocks = -(-n_slot // MOE_BLOCK) + N_EXPERTS
    n_rows = n_blocks * MOE_BLOCK
    row16 = padded_start[None, :] + rank
    d0 = jnp.min(jnp.where(sel, row16, n_rows), axis=1)
    d1 = jnp.max(jnp.where(sel, row16, -1), axis=1)
    g0 = jnp.sum(jnp.where(row16 == d0[:, None], gate, 0.0), axis=1)
    g1 = jnp.sum(jnp.where(row16 == d1[:, None], gate, 0.0), axis=1)
    dest = jnp.stack([d0, d1], axis=1).astype(jnp.int32).reshape(-1)
    tok = (jnp.arange(n_slot, dtype=jnp.int32) // TOP_K)
    row_tok = jnp.zeros((n_rows,), jnp.int32).at[dest].set(tok)
    row_gate = jnp.zeros((n_rows,), F32).at[dest].set(jnp.stack([g0, g1], axis=1).reshape(-1))
    block_expert = jnp.minimum(
        jnp.searchsorted(padded_end, jnp.arange(n_blocks, dtype=jnp.int32) * MOE_BLOCK, side='right'),
        N_EXPERTS - 1).astype(jnp.int32)
    n_used = (padded_end[-1] // MOE_BLOCK).astype(jnp.int32).reshape(1)
    return dest.reshape(n_tok, TOP_K), row_tok, row_gate.reshape(n_rows, 1), block_expert, n_used


def _rope_tables(ctx_len, n_lat):
    rows = n_lat // GRID_W
    row = np.repeat(np.arange(rows, dtype=np.float32), GRID_W)
    col = np.tile(np.arange(GRID_W, dtype=np.float32), rows)
    half = ATT_HD // 4
    inv = jnp.asarray(ROPE_BASE, F32) ** (-jnp.arange(half, dtype=F32) / half)
    ang_r = jnp.asarray(row)[:, None] * inv
    ang_c = jnp.asarray(col)[:, None] * inv
    cos64 = jnp.concatenate([jnp.cos(ang_r)] * 2 + [jnp.cos(ang_c)] * 2, axis=1)
    sin64 = jnp.concatenate([-jnp.sin(ang_r), jnp.sin(ang_r), -jnp.sin(ang_c), jnp.sin(ang_c)], axis=1)
    cos64 = jnp.concatenate([jnp.ones((ctx_len, ATT_HD), F32), cos64], axis=0)
    sin64 = jnp.concatenate([jnp.zeros((ctx_len, ATT_HD), F32), sin64], axis=0)
    return jnp.tile(cos64, (1, ATT_HEADS)), jnp.tile(sin64, (1, ATT_HEADS))


def _pack_w_in(w):
    gla = w[:, 1792:2592]
    gla = jnp.concatenate([gla[:, 0:512], gla[:, 544:800], gla[:, 512:544],
                           jnp.zeros((w.shape[0], 96), w.dtype)], axis=1)
    return jnp.concatenate([w[:, :1792], gla, w[:, 2592:]], axis=1).astype(BF16)


def kernel(x, c, ctx, c_ctx, hg_lb_logits, router_w, router_b, final_g, w_mod, b_mod, g_mix, g_ffn,
           w_in, conv_w, conv_b, conv_ln_g, conv_ln_b, hg_norm_g, gla_w2, gla_b2, gla_norm_g, att_sink,
           w_br_conv, w_br_hg, w_br_gla, w_br_att, w_out, moe_w1, moe_w3, moe_w2):
    n_batch, n_lat, d = x.shape
    ctx_len = ctx.shape[1]
    depth = w_in.shape[0]
    assert ctx_len % TOK_TILE == 0 and n_lat % TOK_TILE == 0 and n_lat % GRID_W == 0
    seq = ctx_len + n_lat
    n_ctx_tiles = ctx_len // TOK_TILE

    xs = jnp.concatenate([ctx, x], axis=1).reshape(n_batch * seq, d)

    c_rows = jnp.zeros((8, d), F32).at[:n_batch].set(c).at[n_batch].set(c_ctx)
    mods = modulation(c_rows, w_mod, b_mod).reshape(depth, 8, N_MOD, d)
    modv = jnp.stack([mods[:, n_batch] if j % 2 == 0 else mods[:, j // 2] for j in range(2 * n_batch)], axis=1)

    lb_sm = jax.nn.softmax(hg_lb_logits.astype(F32), axis=0)
    lower = jnp.cumsum(lb_sm, axis=0) - lb_sm[0]
    cos_t, sin_t = _rope_tables(ctx_len, n_lat)
    ones_bd = jnp.asarray(np.arange(256)[:, None] // 64 == np.arange(256)[None, :] // 64, BF16)
    rw = jnp.zeros((d, LANES), F32).at[:, :N_EXPERTS].set(router_w.astype(F32))
    rb = jnp.zeros((1, LANES), F32).at[0, :N_EXPERTS].set(router_b.astype(F32))
    tri = jnp.asarray(np.tril(np.ones((TOK_TILE, TOK_TILE))), BF16)

    for l in range(depth):
        w2p = jnp.zeros((LANES, 2 * GLA_K), F32)
        w2p = w2p.at[0:GLA_RANK, 0:GLA_K].set(gla_w2[l, 0]).at[GLA_RANK:2 * GLA_RANK, GLA_K:].set(gla_w2[l, 1])
        b2p = gla_b2[l].reshape(1, 2 * GLA_K)
        conv_z, hg_pack, gla_pack, att_pack, bgate = inproj(
            xs, modv[l], g_mix[l].reshape(1, d), _pack_w_in(w_in[l]), lower[l], w2p, b2p, cos_t, sin_t, n_batch, n_ctx_tiles)

        hg_f = gated_scan(hg_pack, (0, 1, 3, 4), HG_W, n_batch, n_ctx_tiles, rev=False)
        hg_b = gated_scan(hg_pack, (0, 2, 3, 5), HG_W, n_batch, n_ctx_tiles, rev=True)
        gla_f = gated_scan(gla_pack, (0, 1, 2, 2), GLA_K, n_batch, n_ctx_tiles, rev=False)
        gla_b = gated_scan(gla_pack, (0, 1, 2, 3), GLA_K, n_batch, n_ctx_tiles, rev=True)
        att_o = attention(att_pack, att_sink[l].astype(F32), n_batch, ctx_len)

        lw = dict(conv_w=jnp.zeros((32, CONV_CH), F32).at[:CONV_K].set(conv_w[l]),
                  conv_b=conv_b[l].reshape(1, -1), conv_ln_g=conv_ln_g[l].reshape(1, -1),
                  conv_ln_b=conv_ln_b[l].reshape(1, -1), hg_norm_g=hg_norm_g[l].reshape(1, -1),
                  gla_norm_g=gla_norm_g[l].reshape(1, -1), ones_bd=ones_bd,
                  w_br_conv=w_br_conv[l].astype(BF16), w_br_hg=w_br_hg[l].astype(BF16),
                  w_br_gla=w_br_gla[l].astype(BF16), w_br_att=w_br_att[l].astype(BF16),
                  w_out=w_out[l].astype(BF16), g_ffn=g_ffn[l].reshape(1, d), router_w=rw, router_b=rb,
                  tri=tri)
        x_new, h2, gate, rank = merge(xs, modv[l], conv_z, hg_pack, hg_f, hg_b, gla_pack, gla_f, gla_b,
                                      att_o, bgate, lw, n_batch, n_ctx_tiles)

        dest, row_tok, row_gate, block_expert, n_used = route(gate, rank)
        ys = experts(h2, row_tok, row_gate, block_expert, n_used, moe_w1[l], moe_w3[l], moe_w2[l])
        nt = n_batch * seq // TOK_TILE
        dest2 = dest.reshape(nt, TOK_TILE, TOP_K).transpose(0, 2, 1).reshape(nt, 1, TOP_K * TOK_TILE)
        final = l == depth - 1
        xs = combine(x_new, modv[l], ys, dest2, final_g.reshape(1, d), n_batch, n_ctx_tiles, final)

    return xs.reshape(n_batch, n_lat, d)
```

```python
import functools

import numpy as np
import jax
import jax.numpy as jnp
from jax import lax
from jax.experimental import pallas as pl
from jax.experimental.pallas import tpu as pltpu

F32 = jnp.float32
BF16 = jnp.bfloat16

EPS = 1e-6
NEG = -1e30
TINY = 1e-30
N_MOD = 6
CONV_CH = 256
CONV_K = 31
HG_HEADS = 4
HG_W = 256
GLA_HEADS = 4
GLA_K = 128
GLA_V = 256
GLA_RANK = 16
GLA_TAU = 16.0
ATT_HEADS = 4
ATT_KV_HEADS = 2
ATT_HD = 64
ATT_BLOCK = 128
GRID_W = 64
ROPE_BASE = 10000.0
N_EXPERTS = 16
N_GROUPS = 4
TOP_K = 2
MOE_BLOCK = 256

LANES = 128
TOK_TILE = 256
SCAN_BLOCK = 16
HALO = 16
GRANULE = 8
SORT_ROWS = -(-(TOP_K * TOK_TILE + N_EXPERTS * (GRANULE - 1)) // LANES) * LANES
VMEM_LIMIT = 56 * 1024 * 1024

W_CONV = (0, 512)
W_HG = (512, 1792)
W_GLA = (1792, 2688)
W_ATT = (2688, 3200)
W_BG = (3200, 7296)
W_IN_PACKED = 7296


def _cparams(sem):
    return pltpu.CompilerParams(dimension_semantics=sem, vmem_limit_bytes=VMEM_LIMIT)


def _dot(a, b):
    return jnp.dot(a, b, preferred_element_type=F32)


def _split(a):
    hi = a.astype(BF16)
    lo = (a - hi.astype(F32)).astype(BF16)
    return hi, lo


def _dot_f32(a, b):
    ah, al = _split(a)
    bh, bl = _split(b)
    return _dot(ah, bh) + _dot(ah, bl) + _dot(al, bh)


def _sigmoid(x):
    return 1.0 / (1.0 + jnp.exp(-x))


def _silu(x):
    return x * _sigmoid(x)


def _mod_kernel(c_ref, w_ref, b_ref, o_ref):
    c = c_ref[...]
    o_ref[0] = _dot_f32(_silu(c), w_ref[0]) + b_ref[0]


def modulation(c_rows, w_mod, b_mod):
    depth, d, six_d = w_mod.shape
    nblk = six_d // d
    return pl.pallas_call(
        _mod_kernel,
        grid=(depth, nblk),
        in_specs=[pl.BlockSpec((8, d), lambda l, j: (0, 0)),
                  pl.BlockSpec((1, d, d), lambda l, j: (l, 0, j)),
                  pl.BlockSpec((1, 1, d), lambda l, j: (l, 0, j))],
        out_specs=pl.BlockSpec((1, 8, d), lambda l, j: (l, 0, j)),
        out_shape=jax.ShapeDtypeStruct((depth, 8, six_d), F32),
        compiler_params=_cparams(("parallel", "parallel")),
        name="modulation",
    )(c_rows, w_mod, b_mod.reshape(depth, 1, six_d))


def _modnorm(x, g, shift, scale):
    y = x * lax.rsqrt(jnp.mean(x * x, axis=-1, keepdims=True) + EPS)
    return (y * g) * (1.0 + scale) + shift


def _inproj_kernel(x_ref, mod_ref, g_ref, w_ref, lb_ref, w2_ref, b2_ref, cos_ref, sin_ref,
                   conv_ref, hg_ref, gla_ref, att_ref, bg_ref):
    x = x_ref[...]
    h = _modnorm(x, g_ref[...], mod_ref[0, 0:1, :], mod_ref[0, 1:2, :]).astype(BF16)

    p = _dot(h, w_ref[:, W_CONV[0]:W_CONV[1]])
    conv_ref[...] = p[:, :CONV_CH] * _sigmoid(p[:, CONV_CH:])

    p = _dot(h, w_ref[:, W_HG[0]:W_HG[1]])
    hg_ref[:, 0:256] = p[:, 0:256]
    hg_ref[:, 768:1024] = p[:, 768:1024]
    hg_ref[:, 1536:1792] = _silu(p[:, 1024:1280])
    for d in range(2):
        z = p[:, 256 * (d + 1):256 * (d + 2)]
        lb = lb_ref[d:d + 1, :]
        hg_ref[:, 256 * (d + 1):256 * (d + 2)] = (1.0 - lb) * _sigmoid(-z)
        hg_ref[:, 256 * (d + 4):256 * (d + 5)] = jnp.maximum(lb + (1.0 - lb) * _sigmoid(z), TINY)

    p = _dot(h, w_ref[:, W_GLA[0]:W_GLA[1]])
    gla_ref[:, 0:128] = p[:, 0:128] * (float(GLA_K // GLA_HEADS) ** -0.5)
    gla_ref[:, 128:256] = p[:, 128:256]
    gla_ref[:, 512:768] = p[:, 256:512]
    gla_ref[:, 768:1024] = _silu(p[:, 512:768])
    u = _dot_f32(p[:, 768:896], w2_ref[...]) + b2_ref[...]
    log_sig = jnp.minimum(u, 0.0) - jnp.log(1.0 + jnp.exp(-jnp.abs(u)))
    gla_ref[:, 256:512] = jnp.exp(log_sig * (1.0 / GLA_TAU))

    p = _dot(h, w_ref[:, W_ATT[0]:W_ATT[1]])
    cos = cos_ref[...]
    sin = sin_ref[...]
    def rope(v, width):
        lane = lax.broadcasted_iota(jnp.int32, v.shape, 1)
        partner = jnp.where((lane & 31) < 16,
                            pltpu.roll(v, width - 16, axis=1), pltpu.roll(v, 16, axis=1))
        return v * cos[:, :width] + partner * sin[:, :width]

    att_ref[:, 0:256] = rope(p[:, 0:256], 256) * (float(ATT_HD) ** -0.5)
    att_ref[:, 256:384] = rope(p[:, 256:384], 128)
    att_ref[:, 384:512] = p[:, 384:512]

    p = _dot(h, w_ref[:, W_BG[0]:W_BG[1]])
    bg_ref[...] = _sigmoid(p).astype(BF16)


def inproj(x, modv, g_mix, w_in_p, lb, w2p, b2p, cos_t, sin_t, n_batch, n_ctx_tiles):
    n, d = x.shape
    nt = n // TOK_TILE
    tiles_per_batch = nt // n_batch

    def mod_idx(i):
        b = i // tiles_per_batch
        return b * 2 + ((i % tiles_per_batch) >= n_ctx_tiles).astype(jnp.int32)

    const = lambda i: (0, 0)
    row = lambda i: (i, 0)
    seq = lambda i: (i % tiles_per_batch, 0)
    outs = pl.pallas_call(
        _inproj_kernel,
        grid=(nt,),
        in_specs=[pl.BlockSpec((TOK_TILE, d), row),
                  pl.BlockSpec((1, N_MOD, d), lambda i: (mod_idx(i), 0, 0)),
                  pl.BlockSpec((1, d), const),
                  pl.BlockSpec((d, W_IN_PACKED), const, pipeline_mode=pl.Buffered(1)),
                  pl.BlockSpec((2, HG_W), const),
                  pl.BlockSpec((LANES, 2 * GLA_K), const),
                  pl.BlockSpec((1, 2 * GLA_K), const),
                  pl.BlockSpec((TOK_TILE, 256), seq),
                  pl.BlockSpec((TOK_TILE, 256), seq)],
        out_specs=[pl.BlockSpec((TOK_TILE, 256), row),
                   pl.BlockSpec((TOK_TILE, 1792), row),
                   pl.BlockSpec((TOK_TILE, 1024), row),
                   pl.BlockSpec((TOK_TILE, 512), row),
                   pl.BlockSpec((TOK_TILE, 4096), row)],
        out_shape=[jax.ShapeDtypeStruct((n, 256), F32),
                   jax.ShapeDtypeStruct((n, 1792), F32),
                   jax.ShapeDtypeStruct((n, 1024), F32),
                   jax.ShapeDtypeStruct((n, 512), F32),
                   jax.ShapeDtypeStruct((n, 4096), BF16)],
        compiler_params=_cparams(("parallel",)),
        name="inproj",
    )(x, modv, g_mix, w_in_p, lb, w2p, b2p, cos_t, sin_t)
    return outs


def _scan_kernel(q_ref, k_ref, v_ref, f_ref, sel_ref, msk_ref, o_ref, s_ref, fp_ref, kp_ref, vp_ref,
                 *, rev):
    c = SCAN_BLOCK
    tt, kl = q_ref.shape
    nb = tt // c
    pad = c

    @pl.when(pl.program_id(1) == 0)
    def _():
        s_ref[...] = jnp.zeros_like(s_ref)
        fp_ref[...] = jnp.ones_like(fp_ref)
        kp_ref[...] = jnp.zeros_like(kp_ref)
        vp_ref[...] = jnp.zeros_like(vp_ref)

    q = q_ref[...]
    f = f_ref[...]
    fp_ref[pad:pad + tt, :] = f
    kp_ref[pad:pad + tt, :] = k_ref[...]
    vp_ref[pad:pad + tt, :] = v_ref[...]

    def shifted(ref, d):
        start = pad + d if rev else pad - d
        return ref[start:start + tt, :]

    r = lax.broadcasted_iota(jnp.int32, (tt, kl), 0) & (c - 1)
    if rev:
        r = (c - 1) - r
    rr = (c - 1) - r

    def roll_rows(x, d):
        d = -d if rev else d
        return pltpu.roll(x, d % tt, axis=0)

    a = f
    g = f
    s = 1
    while s < c:
        a = a * jnp.where(r >= s, roll_rows(a, s), 1.0)
        g = g * jnp.where(rr >= s, roll_rows(g, -s), 1.0)
        s *= 2
    suffix = jnp.where(rr >= 1, roll_rows(g, -1), 1.0)

    sel = sel_ref[...]
    p = q
    acc = jnp.zeros((tt, 256), F32)
    for d in range(c):
        if d > 0:
            p = p * shifted(fp_ref, d - 1)
        term = jnp.where(r >= d, p * shifted(kp_ref, d), 0.0).astype(BF16)
        w = _dot(term, sel)
        acc = acc + w * shifted(vp_ref, d)

    qt = (q * a).astype(BF16)
    kh = (k_ref[...] * suffix).astype(BF16)
    vb = v_ref[...].astype(BF16)
    last = 0 if rev else c - 1
    gam = jnp.concatenate([a[j * c + last:j * c + last + 1, :] for j in range(nb)], axis=0)
    gam_t = gam.T
    msk = msk_ref[...]
    st = s_ref[...]
    for jj in range(nb):
        j = nb - 1 - jj if rev else jj
        lo = j * c
        o_ref[lo:lo + c, :] = acc[lo:lo + c, :] + _dot(qt[lo:lo + c, :], st.astype(BF16))
        kv = lax.dot_general(kh[lo:lo + c, :], vb[lo:lo + c, :], (((0,), (0,)), ((), ())),
                             preferred_element_type=F32)
        st = gam_t[:, j:j + 1] * st + kv * msk
    s_ref[...] = st


def gated_scan(pack, cols, kl, n_batch, n_ctx_tiles, rev):
    n = pack.shape[0]
    nt = n // TOK_TILE
    tpb = nt // n_batch
    qc, kc, vc, fc = cols

    def tile(i):
        if not rev:
            return i
        return jnp.where(i < n_ctx_tiles, n_ctx_tiles - 1 - i, tpb - 1 - (i - n_ctx_tiles))

    heads = 4
    dk = kl // heads
    sel = (np.arange(kl)[:, None] // dk == np.arange(256)[None, :] // 64)
    sel_b = jnp.asarray(sel, BF16)
    msk = jnp.asarray(sel, F32)
    rows = lambda col: (lambda b, i: (b * tpb + tile(i), col))
    const = lambda b, i: (0, 0)
    pad_rows = TOK_TILE + 2 * SCAN_BLOCK
    return pl.pallas_call(
        functools.partial(_scan_kernel, rev=rev),
        grid=(n_batch, tpb),
        in_specs=[pl.BlockSpec((TOK_TILE, kl), rows(qc)),
                  pl.BlockSpec((TOK_TILE, kl), rows(kc)),
                  pl.BlockSpec((TOK_TILE, 256), rows(vc)),
                  pl.BlockSpec((TOK_TILE, kl), rows(fc)),
                  pl.BlockSpec((kl, 256), const),
                  pl.BlockSpec((kl, 256), const)],
        out_specs=pl.BlockSpec((TOK_TILE, 256), rows(0)),
        out_shape=jax.ShapeDtypeStruct((n, 256), F32),
        scratch_shapes=[pltpu.VMEM((kl, 256), F32),
                        pltpu.VMEM((pad_rows, kl), F32),
                        pltpu.VMEM((pad_rows, kl), F32),
                        pltpu.VMEM((pad_rows, 256), F32)],
        compiler_params=_cparams(("parallel", "arbitrary")),
        name="scan_rev" if rev else "scan_fwd",
    )(pack, pack, pack, pack, sel_b, msk)


def _attn_kernel(sink_ref, q_ref, kc_ref, vc_ref, k0_ref, k1_ref, k2_ref, v0_ref, v1_ref, v2_ref,
                 o_ref, *, ctx_blocks, blocks_per_batch):
    n = pl.program_id(1)
    blk = ATT_BLOCK
    lane = lax.broadcasted_iota(jnp.int32, (blk, LANES), 1)
    low = lane < ATT_HD
    qi = lax.broadcasted_iota(jnp.int32, (blk, blk), 0)
    ki = lax.broadcasted_iota(jnp.int32, (blk, blk), 1)
    is_lat = n >= ctx_blocks
    band_ok = [
        jnp.logical_and(jnp.logical_and(is_lat, n - 1 >= ctx_blocks), qi <= ki),
        jnp.logical_and(is_lat, qi >= 0),
        jnp.logical_and(jnp.logical_and(is_lat, n + 1 < blocks_per_batch), ki <= qi),
    ]

    def dup(x, hk):
        xr = pltpu.roll(x, ATT_HD, axis=1)
        lo_x = lax.broadcasted_iota(jnp.int32, x.shape, 1) < ATT_HD
        return (jnp.where(lo_x, x, xr) if hk == 0 else jnp.where(lo_x, xr, x)).astype(BF16)

    q = q_ref[...]
    keys = [kc_ref[...], k0_ref[...], k1_ref[...], k2_ref[...]]
    vals = [vc_ref[...], v0_ref[...], v1_ref[...], v2_ref[...]]
    nt_dims = (((1,), (1,)), ((), ()))
    halves = []
    for hk in range(ATT_KV_HEADS):
        kd = [dup(x, hk) for x in keys]
        vd = [dup(x, hk) for x in vals]
        qh = q[:, hk * LANES:(hk + 1) * LANES]
        outs = []
        for g in range(2):
            qg = jnp.where(low if g == 0 else jnp.logical_not(low), qh, 0.0).astype(BF16)
            s = [lax.dot_general(qg, kx, nt_dims, preferred_element_type=F32) for kx in kd]
            for t in range(3):
                s[t + 1] = jnp.where(band_ok[t], s[t + 1], NEG)
            sink = sink_ref[hk * 2 + g]
            m = jnp.maximum(sink, s[0].max(axis=-1, keepdims=True))
            for t in range(3):
                m = jnp.maximum(m, s[t + 1].max(axis=-1, keepdims=True))
            den = jnp.exp(sink - m)
            o = jnp.zeros((blk, LANES), F32)
            for t in range(4):
                e = jnp.exp(s[t] - m)
                den = den + e.sum(axis=-1, keepdims=True)
                o = o + _dot(e.astype(BF16), vd[t])
            outs.append(o / den)
        halves.append(jnp.where(low, outs[0], outs[1]))
    o_ref[:, 0:LANES] = halves[0]
    o_ref[:, LANES:2 * LANES] = halves[1]


def attention(att_pack, sink, n_batch, ctx_len):
    n = att_pack.shape[0]
    bpb = n // n_batch // ATT_BLOCK
    cb = ctx_len // ATT_BLOCK

    def band(off, col):
        def idx(b, i, *_):
            m = jnp.clip(i + off, cb, bpb - 1)
            return (b * bpb + m, col)
        return pl.BlockSpec((ATT_BLOCK, LANES), idx)

    def ctx(col):
        return pl.BlockSpec((ctx_len, LANES), lambda b, i, *_: (b * (bpb * ATT_BLOCK // ctx_len), col))

    grid_spec = pltpu.PrefetchScalarGridSpec(
        num_scalar_prefetch=1,
        grid=(n_batch, bpb),
        in_specs=[pl.BlockSpec((ATT_BLOCK, 256), lambda b, i, *_: (b * bpb + i, 0)),
                  ctx(2), ctx(3),
                  band(-1, 2), band(0, 2), band(1, 2),
                  band(-1, 3), band(0, 3), band(1, 3)],
        out_specs=pl.BlockSpec((ATT_BLOCK, 256), lambda b, i, *_: (b * bpb + i, 0)),
    )
    return pl.pallas_call(
        functools.partial(_attn_kernel, ctx_blocks=cb, blocks_per_batch=bpb),
        grid_spec=grid_spec,
        out_shape=jax.ShapeDtypeStruct((n, 256), F32),
        compiler_params=_cparams(("parallel", "parallel")),
        name="attention",
    )(sink, *([att_pack] * 9))


def _head_norm(o, ones_bd, g):
    sq = o * o
    hi, lo = _split(sq)
    ms = (_dot(hi, ones_bd) + _dot(lo, ones_bd)) * (1.0 / 64.0)
    return o * lax.rsqrt(ms + EPS) * g


def _route_select(aff, bias):
    lane = lax.broadcasted_iota(jnp.int32, aff.shape, 1)
    epg = N_EXPERTS // N_GROUPS
    pos = lane & (epg - 1)
    v = aff + bias

    def nxt(x, o):
        return pltpu.roll(x, LANES - o, axis=1)

    def prv(x, o):
        return pltpu.roll(x, o, axis=1)

    beaten = jnp.zeros(aff.shape, jnp.int32)
    for o in range(1, epg):
        beaten = beaten + jnp.where(jnp.logical_and(pos + o < epg, nxt(v, o) > v), 1, 0)
        beaten = beaten + jnp.where(jnp.logical_and(pos >= o, prv(v, o) >= v), 1, 0)
    top2 = beaten < TOP_K
    t = jnp.where(top2, v, 0.0)
    score = t
    for o in range(1, epg):
        score = score + jnp.where(pos + o < epg, nxt(t, o), 0.0) + jnp.where(pos >= o, prv(t, o), 0.0)
    worse = jnp.zeros(aff.shape, jnp.int32)
    for o in range(epg, N_EXPERTS, epg):
        worse = worse + jnp.where(jnp.logical_and(lane + o < N_EXPERTS, nxt(score, o) > score), 1, 0)
        worse = worse + jnp.where(jnp.logical_and(lane >= o, prv(score, o) >= score), 1, 0)
    sel = jnp.logical_and(jnp.logical_and(lane < N_EXPERTS, worse == 0), top2)
    picked = jnp.where(sel, aff, 0.0)
    gate = picked / jnp.sum(picked, axis=-1, keepdims=True)
    return gate, sel


def _merge_kernel(x_ref, mod_ref, zp_ref, z_ref, zn_ref, hgf_ref, hgb_ref, hgg_ref, glf_ref, glb_ref,
                  glg_ref, att_ref, bg_ref, cw_ref, cb_ref, lng_ref, lnb_ref, hgn_ref, gln_ref,
                  ones_ref, wbc_ref, wbh_ref, wbg_ref, wba_ref, wo_ref, gf_ref, rw_ref, rb_ref, tri_ref,
                  upper_ref, xo_ref, h2_ref, slot_ref, gran_ref, zs_ref, *, tiles_per_batch, n_ctx_tiles):
    i = pl.program_id(0)
    tt = x_ref.shape[0]
    ti = i % tiles_per_batch
    has_prev = jnp.logical_and(ti != 0, ti != n_ctx_tiles)
    has_next = jnp.logical_and(ti != n_ctx_tiles - 1, ti != tiles_per_batch - 1)

    zs_ref[0:HALO, :] = jnp.where(has_prev, zp_ref[...], 0.0)
    zs_ref[HALO:HALO + tt, :] = z_ref[...]
    zs_ref[HALO + tt:HALO + tt + HALO, :] = jnp.where(has_next, zn_ref[...], 0.0)
    acc = jnp.zeros((tt, CONV_CH), F32) + cb_ref[...]
    for j in range(CONV_K):
        off = HALO - CONV_K // 2 + j
        acc = acc + zs_ref[off:off + tt, :] * cw_ref[j:j + 1, :]
    mu = jnp.mean(acc, axis=-1, keepdims=True)
    cen = acc - mu
    var = jnp.mean(cen * cen, axis=-1, keepdims=True)
    conv_y = _silu(cen * lax.rsqrt(var + EPS) * lng_ref[...] + lnb_ref[...])

    ones_bd = ones_ref[...]
    hg_y = _head_norm(hgf_ref[...] + hgb_ref[...], ones_bd, hgn_ref[...]) * hgg_ref[...]
    gla_y = _head_norm(glf_ref[...] + glb_ref[...], ones_bd, gln_ref[...]) * glg_ref[...]

    d = x_ref.shape[1]
    merged = bg_ref[:, 0:d].astype(F32) * _dot(conv_y.astype(BF16), wbc_ref[...])
    merged = merged + bg_ref[:, d:2 * d].astype(F32) * _dot(hg_y.astype(BF16), wbh_ref[...])
    merged = merged + bg_ref[:, 2 * d:3 * d].astype(F32) * _dot(gla_y.astype(BF16), wbg_ref[...])
    merged = merged + bg_ref[:, 3 * d:4 * d].astype(F32) * _dot(att_ref[...].astype(BF16), wba_ref[...])
    mix = _dot(merged.astype(BF16), wo_ref[...])

    x_new = x_ref[...] + mod_ref[0, 2:3, :] * mix
    xo_ref[...] = x_new
    h2 = _modnorm(x_new, gf_ref[...], mod_ref[0, 3:4, :], mod_ref[0, 4:5, :])
    h2_ref[...] = h2
    gate, sel = _route_select(_sigmoid(_dot_f32(h2, rw_ref[...])), rb_ref[...])

    incl = _dot(tri_ref[...], sel.astype(F32).astype(BF16))
    gran = jnp.floor((incl[tt - 1:tt, :] + (GRANULE - 1.0)) * (1.0 / GRANULE))
    gran8 = jnp.broadcast_to(gran, (8, LANES))
    start = _dot(gran8.astype(BF16), upper_ref[...])[0:1, :]
    pos = GRANULE * start + incl - 1.0
    p0 = jnp.min(jnp.where(sel, pos, 1e9), axis=-1, keepdims=True)
    p1 = jnp.max(jnp.where(sel, pos, -1.0), axis=-1, keepdims=True)
    g0 = jnp.sum(jnp.where(jnp.logical_and(sel, pos == p0), gate, 0.0), axis=-1, keepdims=True)
    g1 = jnp.sum(jnp.where(jnp.logical_and(sel, pos == p1), gate, 0.0), axis=-1, keepdims=True)
    lane = lax.broadcasted_iota(jnp.int32, (tt, LANES), 1)
    slot_ref[...] = jnp.where(lane == 0, p0, jnp.where(lane == 1, p1, jnp.where(lane == 2, g0,
                              jnp.where(lane == 3, g1, 0.0))))
    gran_ref[0] = gran8.astype(jnp.int32)


def merge(x, modv, conv_z, hg_pack, hg_f, hg_b, gla_pack, gla_f, gla_b, att_o, bgate, lw, n_batch,
          n_ctx_tiles):
    n, d = x.shape
    nt = n // TOK_TILE
    tpb = nt // n_batch
    hpt = TOK_TILE // HALO
    n_halo = n // HALO

    def mod_idx(i):
        return (i // tpb) * 2 + ((i % tpb) >= n_ctx_tiles).astype(jnp.int32)

    row = lambda i: (i, 0)
    const = lambda i: (0, 0)
    col = lambda c: (lambda i: (i, c))
    full = lambda a: pl.BlockSpec(a.shape, const)
    weights = [lw["conv_w"], lw["conv_b"], lw["conv_ln_g"], lw["conv_ln_b"], lw["hg_norm_g"],
               lw["gla_norm_g"], lw["ones_bd"], lw["w_br_conv"], lw["w_br_hg"], lw["w_br_gla"],
               lw["w_br_att"], lw["w_out"], lw["g_ffn"], lw["router_w"], lw["router_b"], lw["tri"],
               lw["upper"]]
    return pl.pallas_call(
        functools.partial(_merge_kernel, tiles_per_batch=tpb, n_ctx_tiles=n_ctx_tiles),
        grid=(nt,),
        in_specs=[pl.BlockSpec((TOK_TILE, d), row),
                  pl.BlockSpec((1, N_MOD, d), lambda i: (mod_idx(i), 0, 0)),
                  pl.BlockSpec((HALO, CONV_CH), lambda i: (jnp.maximum(i * hpt - 1, 0), 0)),
                  pl.BlockSpec((TOK_TILE, CONV_CH), row),
                  pl.BlockSpec((HALO, CONV_CH), lambda i: (jnp.minimum((i + 1) * hpt, n_halo - 1), 0)),
                  pl.BlockSpec((TOK_TILE, 256), row),
                  pl.BlockSpec((TOK_TILE, 256), row),
                  pl.BlockSpec((TOK_TILE, 256), col(6)),
                  pl.BlockSpec((TOK_TILE, 256), row),
                  pl.BlockSpec((TOK_TILE, 256), row),
                  pl.BlockSpec((TOK_TILE, 256), col(3)),
                  pl.BlockSpec((TOK_TILE, 256), row),
                  pl.BlockSpec((TOK_TILE, 4 * d), row)] + [full(w) for w in weights],
        out_specs=[pl.BlockSpec((TOK_TILE, d), row),
                   pl.BlockSpec((TOK_TILE, d), row),
                   pl.BlockSpec((TOK_TILE, LANES), row),
                   pl.BlockSpec((1, 8, LANES), lambda i: (i, 0, 0))],
        out_shape=[jax.ShapeDtypeStruct((n, d), F32),
                   jax.ShapeDtypeStruct((n, d), F32),
                   jax.ShapeDtypeStruct((n, LANES), F32),
                   jax.ShapeDtypeStruct((nt, 8, LANES), jnp.int32)],
        scratch_shapes=[pltpu.VMEM((TOK_TILE + 2 * HALO, CONV_CH), F32)],
        compiler_params=_cparams(("parallel",)),
        name="merge",
    )(x, modv, conv_z, conv_z, conv_z, hg_f, hg_b, hg_pack, gla_f, gla_b, gla_pack, att_o, bgate,
      *weights)


def _granule_copy(src, src_row, dst, dst_row, sem):
    return pltpu.make_async_copy(src.at[pl.ds(pl.multiple_of(src_row, GRANULE), GRANULE)],
                                 dst.at[pl.ds(pl.multiple_of(dst_row, GRANULE), GRANULE)], sem)


def _slot_matrix(slot_ref, width, weighted):
    tt = slot_ref.shape[0]
    col = lax.broadcasted_iota(jnp.int32, (tt, width), 1).astype(F32)
    hit0 = col == slot_ref[:, 0:1]
    hit1 = col == slot_ref[:, 1:2]
    if not weighted:
        return jnp.where(jnp.logical_or(hit0, hit1), 1.0, 0.0)
    return jnp.where(hit0, slot_ref[:, 2:3], 0.0) + jnp.where(hit1, slot_ref[:, 3:4], 0.0)


def _dispatch_kernel(tab_ref, tabp_ref, tail_ref, h_ref, slot_ref, xs_hbm, buf, zbuf, sem, zsem, bsem):
    i = pl.program_id(0)
    nb = pl.num_programs(0)
    slot = i % 2
    ng = tab_ref[0, 0, LANES - 1]

    perm = _slot_matrix(slot_ref, buf.shape[1], weighted=False).astype(BF16)
    buf[slot] = lax.dot_general(perm, h_ref[...].astype(BF16), (((0,), (0,)), ((), ())),
                                preferred_element_type=F32)

    def issue(j, carry):
        _granule_copy(buf.at[slot], j * GRANULE, xs_hbm, tab_ref[0, 0, j] * GRANULE, sem.at[slot]).start()
        return carry
    lax.fori_loop(0, ng, issue, 0)

    def drain(count, s):
        def body(j, carry):
            _granule_copy(buf.at[s], 0, xs_hbm, 0, sem.at[s]).wait()
            return carry
        lax.fori_loop(0, count, body, 0)

    @pl.when(i > 0)
    def _():
        drain(tabp_ref[0, 0, LANES - 1], 1 - slot)

    @pl.when(i == nb - 1)
    def _():
        drain(ng, slot)

    @pl.when(i == 0)
    def _():
        zbuf[...] = jnp.zeros_like(zbuf)
        n_used = tail_ref[0, 0, 2 * N_EXPERTS]
        n_blocks = xs_hbm.shape[0] // MOE_BLOCK

        def block_copy(b):
            return pltpu.make_async_copy(
                zbuf, xs_hbm.at[pl.ds(pl.multiple_of(b * MOE_BLOCK, MOE_BLOCK), MOE_BLOCK)], bsem)

        for e in range(N_EXPERTS):
            def fill(m, carry, e=e):
                _granule_copy(zbuf, 0, xs_hbm, (tail_ref[0, 0, e] + m) * GRANULE, zsem).start()
                return carry
            lax.fori_loop(0, tail_ref[0, 0, N_EXPERTS + e], fill, 0)

        def fill_block(b, carry):
            block_copy(b).start()
            return carry
        lax.fori_loop(n_used, n_blocks, fill_block, 0)

        for e in range(N_EXPERTS):
            def done(m, carry):
                _granule_copy(zbuf, 0, xs_hbm, 0, zsem).wait()
                return carry
            lax.fori_loop(0, tail_ref[0, 0, N_EXPERTS + e], done, 0)

        def done_block(b, carry):
            block_copy(b).wait()
            return carry
        lax.fori_loop(n_used, n_blocks, done_block, 0)


def dispatch(h2, slots, table, tails, n_rows):
    n, d = h2.shape
    nt = n // TOK_TILE
    return pl.pallas_call(
        _dispatch_kernel,
        grid=(nt,),
        in_specs=[pl.BlockSpec((1, 1, LANES), lambda i: (i, 0, 0), memory_space=pltpu.SMEM),
                  pl.BlockSpec((1, 1, LANES), lambda i: (jnp.maximum(i - 1, 0), 0, 0), memory_space=pltpu.SMEM),
                  pl.BlockSpec((1, 1, LANES), lambda i: (0, 0, 0), memory_space=pltpu.SMEM),
                  pl.BlockSpec((TOK_TILE, d), lambda i: (i, 0)),
                  pl.BlockSpec((TOK_TILE, LANES), lambda i: (i, 0))],
        out_specs=pl.BlockSpec(memory_space=pl.ANY),
        out_shape=jax.ShapeDtypeStruct((n_rows, d), F32),
        scratch_shapes=[pltpu.VMEM((2, SORT_ROWS, d), F32),
                        pltpu.VMEM((MOE_BLOCK, d), F32),
                        pltpu.SemaphoreType.DMA((2,)),
                        pltpu.SemaphoreType.DMA(()),
                        pltpu.SemaphoreType.DMA(())],
        compiler_params=_cparams(("arbitrary",)),
        name="dispatch",
    )(table, table, tails, h2, slots)


def _expert_kernel(be_ref, nu_ref, x_ref, w1_ref, w3_ref, w2_ref, y_ref, w1b, w3b, w2b):
    i = pl.program_id(0)
    n_used = nu_ref[0]

    @pl.when(i < n_used)
    def _():
        first = jnp.logical_or(i == 0, be_ref[i] != be_ref[jnp.maximum(i - 1, 0)])

        @pl.when(first)
        def _():
            w1b[...] = w1_ref[0].astype(BF16)
            w3b[...] = w3_ref[0].astype(BF16)
            w2b[...] = w2_ref[0].astype(BF16)

        x = x_ref[...].astype(BF16)
        hid = _silu(_dot(x, w1b[...])) * _dot(x, w3b[...])
        y_ref[...] = _dot(hid.astype(BF16), w2b[...])

    @pl.when(i >= n_used)
    def _():
        y_ref[...] = jnp.zeros_like(y_ref)


def experts(xs, block_expert, n_used, w1, w3, w2):
    n_rows, d = xs.shape
    n_blocks = n_rows // MOE_BLOCK
    ff = w1.shape[-1]
    grid_spec = pltpu.PrefetchScalarGridSpec(
        num_scalar_prefetch=2,
        grid=(n_blocks,),
        in_specs=[pl.BlockSpec((MOE_BLOCK, d), lambda i, be, nu: (jnp.minimum(i, nu[0] - 1), 0)),
                  pl.BlockSpec((1, d, ff), lambda i, be, nu: (be[i], 0, 0)),
                  pl.BlockSpec((1, d, ff), lambda i, be, nu: (be[i], 0, 0)),
                  pl.BlockSpec((1, ff, d), lambda i, be, nu: (be[i], 0, 0))],
        out_specs=pl.BlockSpec((MOE_BLOCK, d), lambda i, be, nu: (i, 0)),
        scratch_shapes=[pltpu.VMEM((d, ff), BF16),
                        pltpu.VMEM((d, ff), BF16),
                        pltpu.VMEM((ff, d), BF16)],
    )
    return pl.pallas_call(
        _expert_kernel,
        grid_spec=grid_spec,
        out_shape=jax.ShapeDtypeStruct((n_rows, d), F32),
        compiler_params=_cparams(("arbitrary",)),
        name="experts",
    )(block_expert, n_used, xs, w1, w3, w2)


def _combine_kernel(d_ref, dn_ref, x_ref, mod_ref, g_ref, slot_ref, y_hbm, o_ref, ybuf, sem, *, final):
    i = pl.program_id(0)
    nb = pl.num_programs(0)
    slot = i % 2
    n_gran = ybuf.shape[1] // GRANULE

    def fetch(tab, s):
        def body(j, carry):
            _granule_copy(y_hbm, tab[0, 0, j] * GRANULE, ybuf.at[s], j * GRANULE, sem.at[s]).start()
            return carry
        lax.fori_loop(0, n_gran, body, 0)

    @pl.when(i == 0)
    def _():
        fetch(d_ref, 0)

    @pl.when(i + 1 < nb)
    def _():
        fetch(dn_ref, 1 - slot)

    def drain(j, carry):
        _granule_copy(y_hbm, 0, ybuf.at[slot], 0, sem.at[slot]).wait()
        return carry
    lax.fori_loop(0, n_gran, drain, 0)

    hi, lo = _split(_slot_matrix(slot_ref, ybuf.shape[1], weighted=True))
    yb = ybuf[slot].astype(BF16)
    y = _dot(hi, yb) + _dot(lo, yb)
    x = x_ref[...] + mod_ref[0, 5:6, :] * y
    if final:
        x = x * lax.rsqrt(jnp.mean(x * x, axis=-1, keepdims=True) + EPS) * g_ref[...]
    o_ref[...] = x


def combine(x, modv, ys, table, slots, final_g, n_batch, n_ctx_tiles, final):
    n, d = x.shape
    nt = n // TOK_TILE
    tpb = nt // n_batch
    if final:
        lat = tpb - n_ctx_tiles
        steps = n_batch * lat
        tile = lambda s: (s // lat) * tpb + n_ctx_tiles + s % lat
        out_rows = steps * TOK_TILE
    else:
        steps = nt
        tile = lambda s: s
        out_rows = n

    def mod_idx(s):
        t = tile(s)
        return (t // tpb) * 2 + ((t % tpb) >= n_ctx_tiles).astype(jnp.int32)

    return pl.pallas_call(
        functools.partial(_combine_kernel, final=final),
        grid=(steps,),
        in_specs=[pl.BlockSpec((1, 1, LANES), lambda s: (tile(s), 0, 0), memory_space=pltpu.SMEM),
                  pl.BlockSpec((1, 1, LANES), lambda s: (tile(jnp.minimum(s + 1, steps - 1)), 0, 0),
                               memory_space=pltpu.SMEM),
                  pl.BlockSpec((TOK_TILE, d), lambda s: (tile(s), 0)),
                  pl.BlockSpec((1, N_MOD, d), lambda s: (mod_idx(s), 0, 0)),
                  pl.BlockSpec((1, d), lambda s: (0, 0)),
                  pl.BlockSpec((TOK_TILE, LANES), lambda s: (tile(s), 0)),
                  pl.BlockSpec(memory_space=pl.ANY)],
        out_specs=pl.BlockSpec((TOK_TILE, d), lambda s: (s, 0)),
        out_shape=jax.ShapeDtypeStruct((out_rows, d), F32),
        scratch_shapes=[pltpu.VMEM((2, SORT_ROWS, d), F32),
                        pltpu.SemaphoreType.DMA((2,))],
        compiler_params=_cparams(("arbitrary",)),
        name="combine_final" if final else "combine",
    )(table, table, x, modv, final_g, slots, ys)


def sorted_rows_bound(n_tok):
    nt = n_tok // TOK_TILE
    rows = n_tok * TOP_K + nt * N_EXPERTS * (GRANULE - 1) + N_EXPERTS * (MOE_BLOCK - GRANULE)
    return -(-rows // MOE_BLOCK) * MOE_BLOCK


def moe_tables(gran, n_rows):
    nt = gran.shape[0]
    per_blk = MOE_BLOCK // GRANULE
    local = jnp.cumsum(gran, axis=1) - gran
    n_gran = jnp.sum(gran, axis=1)
    before = jnp.cumsum(gran, axis=0) - gran
    total = jnp.sum(gran, axis=0)
    padded = (total + per_blk - 1) // per_blk * per_blk
    region_end = jnp.cumsum(padded)
    region = region_end - padded
    j = jnp.arange(LANES, dtype=jnp.int32)[None, None, :]
    inside = jnp.logical_and(j >= local[:, :, None], j < (local + gran)[:, :, None])
    dst = region[None, :, None] + before[:, :, None] + j - local[:, :, None]
    table = jnp.sum(jnp.where(inside, dst, 0), axis=1)
    valid = j[0] < n_gran[:, None]
    table = jnp.where(valid, table, table[:, :1])
    table = table.at[:, LANES - 1].set(n_gran).astype(jnp.int32).reshape(nt, 1, LANES)
    tails = jnp.zeros((LANES,), jnp.int32).at[:N_EXPERTS].set(region + total)
    tails = tails.at[N_EXPERTS:2 * N_EXPERTS].set(padded - total)
    n_blocks = n_rows // MOE_BLOCK
    blk = jnp.arange(n_blocks, dtype=jnp.int32) * per_blk
    block_expert = jnp.minimum(jnp.sum(blk[:, None] >= region_end[None, :], axis=1), N_EXPERTS - 1)
    n_used = (region_end[-1] // per_blk).astype(jnp.int32).reshape(1)
    tails = tails.at[2 * N_EXPERTS].set(n_used[0]).reshape(1, 1, LANES)
    return table, tails, block_expert.astype(jnp.int32), n_used


def _rope_tables(ctx_len, n_lat):
    rows = n_lat // GRID_W
    row = np.repeat(np.arange(rows, dtype=np.float32), GRID_W)
    col = np.tile(np.arange(GRID_W, dtype=np.float32), rows)
    half = ATT_HD // 4
    inv = jnp.asarray(ROPE_BASE, F32) ** (-jnp.arange(half, dtype=F32) / half)
    ang_r = jnp.asarray(row)[:, None] * inv
    ang_c = jnp.asarray(col)[:, None] * inv
    cos64 = jnp.concatenate([jnp.cos(ang_r)] * 2 + [jnp.cos(ang_c)] * 2, axis=1)
    sin64 = jnp.concatenate([-jnp.sin(ang_r), jnp.sin(ang_r), -jnp.sin(ang_c), jnp.sin(ang_c)], axis=1)
    cos64 = jnp.concatenate([jnp.ones((ctx_len, ATT_HD), F32), cos64], axis=0)
    sin64 = jnp.concatenate([jnp.zeros((ctx_len, ATT_HD), F32), sin64], axis=0)
    return jnp.tile(cos64, (1, ATT_HEADS)), jnp.tile(sin64, (1, ATT_HEADS))


def _pack_w_in(w):
    gla = w[:, 1792:2592]
    gla = jnp.concatenate([gla[:, 0:512], gla[:, 544:800], gla[:, 512:544],
                           jnp.zeros((w.shape[0], 96), w.dtype)], axis=1)
    return jnp.concatenate([w[:, :1792], gla, w[:, 2592:]], axis=1).astype(BF16)


def kernel(x, c, ctx, c_ctx, hg_lb_logits, router_w, router_b, final_g, w_mod, b_mod, g_mix, g_ffn,
           w_in, conv_w, conv_b, conv_ln_g, conv_ln_b, hg_norm_g, gla_w2, gla_b2, gla_norm_g, att_sink,
           w_br_conv, w_br_hg, w_br_gla, w_br_att, w_out, moe_w1, moe_w3, moe_w2):
    n_batch, n_lat, d = x.shape
    ctx_len = ctx.shape[1]
    depth = w_in.shape[0]
    assert ctx_len % TOK_TILE == 0 and n_lat % TOK_TILE == 0 and n_lat % GRID_W == 0
    seq = ctx_len + n_lat
    n_ctx_tiles = ctx_len // TOK_TILE

    xs = jnp.concatenate([ctx, x], axis=1).reshape(n_batch * seq, d)

    c_rows = jnp.zeros((8, d), F32).at[:n_batch].set(c).at[n_batch].set(c_ctx)
    mods = modulation(c_rows, w_mod, b_mod).reshape(depth, 8, N_MOD, d)
    modv = jnp.stack([mods[:, n_batch] if j % 2 == 0 else mods[:, j // 2] for j in range(2 * n_batch)], axis=1)

    lb_sm = jax.nn.softmax(hg_lb_logits.astype(F32), axis=0)
    lower = jnp.cumsum(lb_sm, axis=0) - lb_sm[0]
    cos_t, sin_t = _rope_tables(ctx_len, n_lat)
    ones_bd = jnp.asarray(np.arange(256)[:, None] // 64 == np.arange(256)[None, :] // 64, BF16)
    rw = jnp.zeros((d, LANES), F32).at[:, :N_EXPERTS].set(router_w.astype(F32))
    rb = jnp.zeros((1, LANES), F32).at[0, :N_EXPERTS].set(router_b.astype(F32))
    tri = jnp.asarray(np.tril(np.ones((TOK_TILE, TOK_TILE))), BF16)
    upper = jnp.asarray(np.triu(np.ones((LANES, LANES)), k=1), BF16)

    for l in range(depth):
        w2p = jnp.zeros((LANES, 2 * GLA_K), F32)
        w2p = w2p.at[0:GLA_RANK, 0:GLA_K].set(gla_w2[l, 0]).at[GLA_RANK:2 * GLA_RANK, GLA_K:].set(gla_w2[l, 1])
        b2p = gla_b2[l].reshape(1, 2 * GLA_K)
        conv_z, hg_pack, gla_pack, att_pack, bgate = inproj(
            xs, modv[l], g_mix[l].reshape(1, d), _pack_w_in(w_in[l]), lower[l], w2p, b2p, cos_t, sin_t, n_batch, n_ctx_tiles)

        hg_f = gated_scan(hg_pack, (0, 1, 3, 4), HG_W, n_batch, n_ctx_tiles, rev=False)
        hg_b = gated_scan(hg_pack, (0, 2, 3, 5), HG_W, n_batch, n_ctx_tiles, rev=True)
        gla_f = gated_scan(gla_pack, (0, 1, 2, 2), GLA_K, n_batch, n_ctx_tiles, rev=False)
        gla_b = gated_scan(gla_pack, (0, 1, 2, 3), GLA_K, n_batch, n_ctx_tiles, rev=True)
        att_o = attention(att_pack, att_sink[l].astype(F32), n_batch, ctx_len)

        lw = dict(conv_w=jnp.zeros((32, CONV_CH), F32).at[:CONV_K].set(conv_w[l]),
                  conv_b=conv_b[l].reshape(1, -1), conv_ln_g=conv_ln_g[l].reshape(1, -1),
                  conv_ln_b=conv_ln_b[l].reshape(1, -1), hg_norm_g=hg_norm_g[l].reshape(1, -1),
                  gla_norm_g=gla_norm_g[l].reshape(1, -1), ones_bd=ones_bd,
                  w_br_conv=w_br_conv[l].astype(BF16), w_br_hg=w_br_hg[l].astype(BF16),
                  w_br_gla=w_br_gla[l].astype(BF16), w_br_att=w_br_att[l].astype(BF16),
                  w_out=w_out[l].astype(BF16), g_ffn=g_ffn[l].reshape(1, d), router_w=rw, router_b=rb,
                  tri=tri, upper=upper)
        x_new, h2, slots, gran = merge(xs, modv[l], conv_z, hg_pack, hg_f, hg_b, gla_pack, gla_f, gla_b,
                                       att_o, bgate, lw, n_batch, n_ctx_tiles)

        n_rows = sorted_rows_bound(n_batch * seq)
        table, tails, block_expert, n_used = moe_tables(gran[:, 0, :N_EXPERTS], n_rows)
        x_sorted = dispatch(h2, slots, table, tails, n_rows)
        ys = experts(x_sorted, block_expert, n_used, moe_w1[l], moe_w3[l], moe_w2[l])
        final = l == depth - 1
        xs = combine(x_new, modv[l], ys, table, slots, final_g.reshape(1, d), n_batch, n_ctx_tiles, final)

    return xs.reshape(n_batch, n_lat, d)
```

```python
import functools

import numpy as np
import jax
import jax.numpy as jnp
from jax import lax
from jax.experimental import pallas as pl
from jax.experimental.pallas import tpu as pltpu

F32 = jnp.float32
BF16 = jnp.bfloat16

EPS = 1e-6
NEG = -1e30
TINY = 1e-30
N_MOD = 6
CONV_CH = 256
CONV_K = 31
HG_HEADS = 4
HG_W = 256
GLA_HEADS = 4
GLA_K = 128
GLA_V = 256
GLA_RANK = 16
GLA_TAU = 16.0
ATT_HEADS = 4
ATT_KV_HEADS = 2
ATT_HD = 64
ATT_BLOCK = 128
GRID_W = 64
ROPE_BASE = 10000.0
N_EXPERTS = 16
N_GROUPS = 4
TOP_K = 2
MOE_BLOCK = 256

LANES = 128
TOK_TILE = 256
SCAN_BLOCK = 16
HALO = 16
GRANULE = 8
SORT_ROWS = -(-(TOP_K * TOK_TILE + N_EXPERTS * (GRANULE - 1)) // LANES) * LANES
VMEM_LIMIT = 56 * 1024 * 1024

W_CONV = (0, 512)
W_HG = (512, 1792)
W_GLA = (1792, 2688)
W_ATT = (2688, 3200)
W_BG = (3200, 7296)
W_IN_PACKED = 7296


def _cparams(sem):
    return pltpu.CompilerParams(dimension_semantics=sem, vmem_limit_bytes=VMEM_LIMIT)


def _dot(a, b):
    return jnp.dot(a, b, preferred_element_type=F32)


def _split(a):
    hi = a.astype(BF16)
    lo = (a - hi.astype(F32)).astype(BF16)
    return hi, lo


def _dot_f32(a, b):
    ah, al = _split(a)
    bh, bl = _split(b)
    return _dot(ah, bh) + _dot(ah, bl) + _dot(al, bh)


def _sigmoid(x):
    return 1.0 / (1.0 + jnp.exp(-x))


def _silu(x):
    return x * _sigmoid(x)


def _mod_kernel(c_ref, w_ref, b_ref, o_ref):
    c = c_ref[...]
    o_ref[0] = _dot_f32(_silu(c), w_ref[0]) + b_ref[0]


def modulation(c_rows, w_mod, b_mod):
    depth, d, six_d = w_mod.shape
    nblk = six_d // d
    return pl.pallas_call(
        _mod_kernel,
        grid=(depth, nblk),
        in_specs=[pl.BlockSpec((8, d), lambda l, j: (0, 0)),
                  pl.BlockSpec((1, d, d), lambda l, j: (l, 0, j)),
                  pl.BlockSpec((1, 1, d), lambda l, j: (l, 0, j))],
        out_specs=pl.BlockSpec((1, 8, d), lambda l, j: (l, 0, j)),
        out_shape=jax.ShapeDtypeStruct((depth, 8, six_d), F32),
        compiler_params=_cparams(("parallel", "parallel")),
        name="modulation",
    )(c_rows, w_mod, b_mod.reshape(depth, 1, six_d))


def _modnorm(x, g, shift, scale):
    y = x * lax.rsqrt(jnp.mean(x * x, axis=-1, keepdims=True) + EPS)
    return (y * g) * (1.0 + scale) + shift


def _inproj_kernel(x_ref, mod_ref, g_ref, w_ref, lb_ref, w2_ref, b2_ref, cos_ref, sin_ref,
                   conv_ref, hg_ref, gla_ref, att_ref, bg_ref):
    x = x_ref[...]
    h = _modnorm(x, g_ref[...], mod_ref[0, 0:1, :], mod_ref[0, 1:2, :]).astype(BF16)

    p = _dot(h, w_ref[:, W_CONV[0]:W_CONV[1]])
    conv_ref[...] = p[:, :CONV_CH] * _sigmoid(p[:, CONV_CH:])

    p = _dot(h, w_ref[:, W_HG[0]:W_HG[1]])
    hg_ref[:, 0:256] = p[:, 0:256]
    hg_ref[:, 768:1024] = p[:, 768:1024]
    hg_ref[:, 1536:1792] = _silu(p[:, 1024:1280])
    for d in range(2):
        z = p[:, 256 * (d + 1):256 * (d + 2)]
        lb = lb_ref[d:d + 1, :]
        hg_ref[:, 256 * (d + 1):256 * (d + 2)] = (1.0 - lb) * _sigmoid(-z)
        hg_ref[:, 256 * (d + 4):256 * (d + 5)] = jnp.maximum(lb + (1.0 - lb) * _sigmoid(z), TINY)

    p = _dot(h, w_ref[:, W_GLA[0]:W_GLA[1]])
    gla_ref[:, 0:128] = p[:, 0:128] * (float(GLA_K // GLA_HEADS) ** -0.5)
    gla_ref[:, 128:256] = p[:, 128:256]
    gla_ref[:, 512:768] = p[:, 256:512]
    gla_ref[:, 768:1024] = _silu(p[:, 512:768])
    u = _dot_f32(p[:, 768:896], w2_ref[...]) + b2_ref[...]
    log_sig = jnp.minimum(u, 0.0) - jnp.log(1.0 + jnp.exp(-jnp.abs(u)))
    gla_ref[:, 256:512] = jnp.exp(log_sig * (1.0 / GLA_TAU))

    p = _dot(h, w_ref[:, W_ATT[0]:W_ATT[1]])
    cos = cos_ref[...]
    sin = sin_ref[...]
    def rope(v, width):
        lane = lax.broadcasted_iota(jnp.int32, v.shape, 1)
        partner = jnp.where((lane & 31) < 16,
                            pltpu.roll(v, width - 16, axis=1), pltpu.roll(v, 16, axis=1))
        return v * cos[:, :width] + partner * sin[:, :width]

    att_ref[:, 0:256] = rope(p[:, 0:256], 256) * (float(ATT_HD) ** -0.5)
    att_ref[:, 256:384] = rope(p[:, 256:384], 128)
    att_ref[:, 384:512] = p[:, 384:512]

    p = _dot(h, w_ref[:, W_BG[0]:W_BG[1]])
    bg_ref[...] = _sigmoid(p).astype(BF16)


def inproj(x, modv, g_mix, w_in_p, lb, w2p, b2p, cos_t, sin_t, n_batch, n_ctx_tiles):
    n, d = x.shape
    nt = n // TOK_TILE
    tiles_per_batch = nt // n_batch

    def mod_idx(i):
        b = i // tiles_per_batch
        return b * 2 + ((i % tiles_per_batch) >= n_ctx_tiles).astype(jnp.int32)

    const = lambda i: (0, 0)
    row = lambda i: (i, 0)
    seq = lambda i: (i % tiles_per_batch, 0)
    outs = pl.pallas_call(
        _inproj_kernel,
        grid=(nt,),
        in_specs=[pl.BlockSpec((TOK_TILE, d), row),
                  pl.BlockSpec((1, N_MOD, d), lambda i: (mod_idx(i), 0, 0)),
                  pl.BlockSpec((1, d), const),
                  pl.BlockSpec((d, W_IN_PACKED), const, pipeline_mode=pl.Buffered(1)),
                  pl.BlockSpec((2, HG_W), const),
                  pl.BlockSpec((LANES, 2 * GLA_K), const),
                  pl.BlockSpec((1, 2 * GLA_K), const),
                  pl.BlockSpec((TOK_TILE, 256), seq),
                  pl.BlockSpec((TOK_TILE, 256), seq)],
        out_specs=[pl.BlockSpec((TOK_TILE, 256), row),
                   pl.BlockSpec((TOK_TILE, 1792), row),
                   pl.BlockSpec((TOK_TILE, 1024), row),
                   pl.BlockSpec((TOK_TILE, 512), row),
                   pl.BlockSpec((TOK_TILE, 4096), row)],
        out_shape=[jax.ShapeDtypeStruct((n, 256), F32),
                   jax.ShapeDtypeStruct((n, 1792), F32),
                   jax.ShapeDtypeStruct((n, 1024), F32),
                   jax.ShapeDtypeStruct((n, 512), F32),
                   jax.ShapeDtypeStruct((n, 4096), BF16)],
        compiler_params=_cparams(("parallel",)),
        name="inproj",
    )(x, modv, g_mix, w_in_p, lb, w2p, b2p, cos_t, sin_t)
    return outs


def _scan_kernel(q_ref, k_ref, v_ref, f_ref, sel_ref, perm_ref, permt_ref, o_ref,
                 s_ref, qs_ref, ks_ref, vs_ref, fs_ref, qt_ref, kh_ref, term_ref, w_ref, ah_ref, al_ref,
                 kv_ref, *, rev):
    c = SCAN_BLOCK
    tt, kl = q_ref.shape
    nb = tt // c
    assert nb == c

    @pl.when(pl.program_id(1) == 0)
    def _():
        s_ref[...] = jnp.zeros_like(s_ref)

    perm = perm_ref[...]
    vb = v_ref[...].astype(BF16)
    f_hi, f_lo = _split(f_ref[...])
    qs_ref[...] = _dot(perm, q_ref[...].astype(BF16))
    ks_ref[...] = _dot(perm, k_ref[...].astype(BF16))
    vs_ref[...] = _dot(perm, vb)
    fs_ref[...] = _dot(perm, f_hi) + _dot(perm, f_lo)

    def slab(ref, s):
        return ref[s * nb:(s + 1) * nb, :]

    a = slab(fs_ref, 0)
    qt_ref[0:nb, :] = (slab(qs_ref, 0) * a).astype(BF16)
    for s in range(1, c):
        a = a * slab(fs_ref, s)
        qt_ref[s * nb:(s + 1) * nb, :] = (slab(qs_ref, s) * a).astype(BF16)
    gam_t = a.T
    g = jnp.ones((nb, kl), F32)
    for s in range(c - 1, -1, -1):
        kh_ref[s * nb:(s + 1) * nb, :] = (slab(ks_ref, s) * g).astype(BF16)
        g = g * slab(fs_ref, s)

    heads = 256 // 64
    dk = kl // heads
    perm_t = permt_ref[...]
    qt = _dot(perm_t, qt_ref[...]).astype(BF16)
    kh = _dot(perm_t, kh_ref[...]).astype(BF16)

    def head_lanes(x, h):
        return x[:, :dk] if h == 0 else pltpu.roll(x, kl - dk * h, axis=1)[:, :dk]

    q_heads = [head_lanes(qt, h) for h in range(heads)]
    k_heads = [head_lanes(kh, h) for h in range(heads)]
    v_head = lax.broadcasted_iota(jnp.int32, (c, 256), 1) // 64
    g_head = lax.broadcasted_iota(jnp.int32, (dk, 256), 1) // 64
    for j in range(nb):
        lo = j * c
        k4 = jnp.concatenate([k_heads[h][lo:lo + c, :] for h in range(heads)], axis=0)
        v4 = jnp.concatenate([jnp.where(v_head == h, vb[lo:lo + c, :], jnp.zeros_like(vb[lo:lo + c, :]))
                              for h in range(heads)], axis=0)
        kv_ref[j] = lax.dot_general(k4, v4, (((0,), (0,)), ((), ())), preferred_element_type=F32)

    n_pair = 0
    for s in range(c):
        p = slab(qs_ref, s)
        for d in range(s + 1):
            if d > 0:
                p = p * slab(fs_ref, s - d + 1)
            term_ref[n_pair * nb:(n_pair + 1) * nb, :] = (p * slab(ks_ref, s - d)).astype(BF16)
            n_pair += 1
    w_ref[...] = _dot(term_ref[...], sel_ref[...])
    n_pair = 0
    for s in range(c):
        acc = jnp.zeros((nb, 256), F32)
        for d in range(s + 1):
            acc = acc + w_ref[n_pair * nb:(n_pair + 1) * nb, :] * slab(vs_ref, s - d)
            n_pair += 1
        hi, lo = _split(acc)
        ah_ref[s * nb:(s + 1) * nb, :] = hi
        al_ref[s * nb:(s + 1) * nb, :] = lo

    o_ref[...] = _dot(perm_t, ah_ref[...]) + _dot(perm_t, al_ref[...])
    st = s_ref[...]
    for jj in range(nb):
        j = nb - 1 - jj if rev else jj
        lo = j * c
        q4 = jnp.concatenate([q_heads[h][lo:lo + c, :] for h in range(heads)], axis=0)
        o4 = _dot(q4, st.astype(BF16))
        inter = o4[0:c, :]
        for h in range(1, heads):
            inter = jnp.where(v_head == h, o4[h * c:(h + 1) * c, :], inter)
        o_ref[lo:lo + c, :] += inter
        gam = jnp.broadcast_to(gam_t[0:dk, j:j + 1], (dk, 256))
        for h in range(1, heads):
            gam = jnp.where(g_head == h, gam_t[h * dk:(h + 1) * dk, j:j + 1], gam)
        st = gam * st + kv_ref[j]
    s_ref[...] = st


def gated_scan(pack, cols, kl, n_batch, n_ctx_tiles, rev):
    n = pack.shape[0]
    nt = n // TOK_TILE
    tpb = nt // n_batch
    qc, kc, vc, fc = cols

    def tile(i):
        if not rev:
            return i
        return jnp.where(i < n_ctx_tiles, n_ctx_tiles - 1 - i, tpb - 1 - (i - n_ctx_tiles))

    heads = 4
    dk = kl // heads
    sel = (np.arange(kl)[:, None] // dk == np.arange(256)[None, :] // 64)
    sel_b = jnp.asarray(sel, BF16)
    c = SCAN_BLOCK
    nb = TOK_TILE // c
    perm = np.zeros((TOK_TILE, TOK_TILE), np.float32)
    for j in range(nb):
        for s in range(c):
            perm[s * nb + j, j * c + (c - 1 - s if rev else s)] = 1.0
    n_pairs = c * (c + 1) // 2
    rows = lambda col: (lambda b, i: (b * tpb + tile(i), col))
    const = lambda b, i: (0, 0)
    return pl.pallas_call(
        functools.partial(_scan_kernel, rev=rev),
        grid=(n_batch, tpb),
        in_specs=[pl.BlockSpec((TOK_TILE, kl), rows(qc)),
                  pl.BlockSpec((TOK_TILE, kl), rows(kc)),
                  pl.BlockSpec((TOK_TILE, 256), rows(vc)),
                  pl.BlockSpec((TOK_TILE, kl), rows(fc)),
                  pl.BlockSpec((kl, 256), const),
                  pl.BlockSpec((TOK_TILE, TOK_TILE), const),
                  pl.BlockSpec((TOK_TILE, TOK_TILE), const)],
        out_specs=pl.BlockSpec((TOK_TILE, 256), rows(0)),
        out_shape=jax.ShapeDtypeStruct((n, 256), F32),
        scratch_shapes=[pltpu.VMEM((dk, 256), F32),
                        pltpu.VMEM((TOK_TILE, kl), F32),
                        pltpu.VMEM((TOK_TILE, kl), F32),
                        pltpu.VMEM((TOK_TILE, 256), F32),
                        pltpu.VMEM((TOK_TILE, kl), F32),
                        pltpu.VMEM((TOK_TILE, kl), BF16),
                        pltpu.VMEM((TOK_TILE, kl), BF16),
                        pltpu.VMEM((n_pairs * nb, kl), BF16),
                        pltpu.VMEM((n_pairs * nb, 256), F32),
                        pltpu.VMEM((TOK_TILE, 256), BF16),
                        pltpu.VMEM((TOK_TILE, 256), BF16),
                        pltpu.VMEM((nb, dk, 256), F32)],
        compiler_params=_cparams(("parallel", "arbitrary")),
        name="scan_rev" if rev else "scan_fwd",
    )(pack, pack, pack, pack, sel_b, jnp.asarray(perm, BF16), jnp.asarray(perm.T, BF16))


def _attn_kernel(sink_ref, q_ref, kc_ref, vc_ref, k0_ref, k1_ref, k2_ref, v0_ref, v1_ref, v2_ref,
                 o_ref, *, ctx_blocks, blocks_per_batch):
    n = pl.program_id(1)
    blk = ATT_BLOCK
    lane = lax.broadcasted_iota(jnp.int32, (blk, LANES), 1)
    low = lane < ATT_HD
    qi = lax.broadcasted_iota(jnp.int32, (blk, blk), 0)
    ki = lax.broadcasted_iota(jnp.int32, (blk, blk), 1)
    is_lat = n >= ctx_blocks
    band_ok = [
        jnp.logical_and(jnp.logical_and(is_lat, n - 1 >= ctx_blocks), qi <= ki),
        jnp.logical_and(is_lat, qi >= 0),
        jnp.logical_and(jnp.logical_and(is_lat, n + 1 < blocks_per_batch), ki <= qi),
    ]

    def dup(x, hk):
        xr = pltpu.roll(x, ATT_HD, axis=1)
        lo_x = lax.broadcasted_iota(jnp.int32, x.shape, 1) < ATT_HD
        return (jnp.where(lo_x, x, xr) if hk == 0 else jnp.where(lo_x, xr, x)).astype(BF16)

    q = q_ref[...]
    keys = [kc_ref[...], k0_ref[...], k1_ref[...], k2_ref[...]]
    vals = [vc_ref[...], v0_ref[...], v1_ref[...], v2_ref[...]]
    nt_dims = (((1,), (1,)), ((), ()))
    halves = []
    for hk in range(ATT_KV_HEADS):
        kd = [dup(x, hk) for x in keys]
        vd = [dup(x, hk) for x in vals]
        qh = q[:, hk * LANES:(hk + 1) * LANES]
        outs = []
        for g in range(2):
            qg = jnp.where(low if g == 0 else jnp.logical_not(low), qh, 0.0).astype(BF16)
            s = [lax.dot_general(qg, kx, nt_dims, preferred_element_type=F32) for kx in kd]
            for t in range(3):
                s[t + 1] = jnp.where(band_ok[t], s[t + 1], NEG)
            sink = sink_ref[hk * 2 + g]
            m = jnp.maximum(sink, s[0].max(axis=-1, keepdims=True))
            for t in range(3):
                m = jnp.maximum(m, s[t + 1].max(axis=-1, keepdims=True))
            den = jnp.exp(sink - m)
            o = jnp.zeros((blk, LANES), F32)
            for t in range(4):
                e = jnp.exp(s[t] - m)
                den = den + e.sum(axis=-1, keepdims=True)
                o = o + _dot(e.astype(BF16), vd[t])
            outs.append(o / den)
        halves.append(jnp.where(low, outs[0], outs[1]))
    o_ref[:, 0:LANES] = halves[0]
    o_ref[:, LANES:2 * LANES] = halves[1]


def attention(att_pack, sink, n_batch, ctx_len):
    n = att_pack.shape[0]
    bpb = n // n_batch // ATT_BLOCK
    cb = ctx_len // ATT_BLOCK

    def band(off, col):
        def idx(b, i, *_):
            m = jnp.clip(i + off, cb, bpb - 1)
            return (b * bpb + m, col)
        return pl.BlockSpec((ATT_BLOCK, LANES), idx)

    def ctx(col):
        return pl.BlockSpec((ctx_len, LANES), lambda b, i, *_: (b * (bpb * ATT_BLOCK // ctx_len), col))

    grid_spec = pltpu.PrefetchScalarGridSpec(
        num_scalar_prefetch=1,
        grid=(n_batch, bpb),
        in_specs=[pl.BlockSpec((ATT_BLOCK, 256), lambda b, i, *_: (b * bpb + i, 0)),
                  ctx(2), ctx(3),
                  band(-1, 2), band(0, 2), band(1, 2),
                  band(-1, 3), band(0, 3), band(1, 3)],
        out_specs=pl.BlockSpec((ATT_BLOCK, 256), lambda b, i, *_: (b * bpb + i, 0)),
    )
    return pl.pallas_call(
        functools.partial(_attn_kernel, ctx_blocks=cb, blocks_per_batch=bpb),
        grid_spec=grid_spec,
        out_shape=jax.ShapeDtypeStruct((n, 256), F32),
        compiler_params=_cparams(("parallel", "parallel")),
        name="attention",
    )(sink, *([att_pack] * 9))


def _head_norm(o, ones_bd, g):
    sq = o * o
    hi, lo = _split(sq)
    ms = (_dot(hi, ones_bd) + _dot(lo, ones_bd)) * (1.0 / 64.0)
    return o * lax.rsqrt(ms + EPS) * g


def _route_select(aff, bias):
    lane = lax.broadcasted_iota(jnp.int32, aff.shape, 1)
    epg = N_EXPERTS // N_GROUPS
    pos = lane & (epg - 1)
    v = aff + bias

    def nxt(x, o):
        return pltpu.roll(x, LANES - o, axis=1)

    def prv(x, o):
        return pltpu.roll(x, o, axis=1)

    beaten = jnp.zeros(aff.shape, jnp.int32)
    for o in range(1, epg):
        beaten = beaten + jnp.where(jnp.logical_and(pos + o < epg, nxt(v, o) > v), 1, 0)
        beaten = beaten + jnp.where(jnp.logical_and(pos >= o, prv(v, o) >= v), 1, 0)
    top2 = beaten < TOP_K
    t = jnp.where(top2, v, 0.0)
    score = t
    for o in range(1, epg):
        score = score + jnp.where(pos + o < epg, nxt(t, o), 0.0) + jnp.where(pos >= o, prv(t, o), 0.0)
    worse = jnp.zeros(aff.shape, jnp.int32)
    for o in range(epg, N_EXPERTS, epg):
        worse = worse + jnp.where(jnp.logical_and(lane + o < N_EXPERTS, nxt(score, o) > score), 1, 0)
        worse = worse + jnp.where(jnp.logical_and(lane >= o, prv(score, o) >= score), 1, 0)
    sel = jnp.logical_and(jnp.logical_and(lane < N_EXPERTS, worse == 0), top2)
    picked = jnp.where(sel, aff, 0.0)
    gate = picked / jnp.sum(picked, axis=-1, keepdims=True)
    return gate, sel


def _merge_kernel(x_ref, mod_ref, zp_ref, z_ref, zn_ref, hgf_ref, hgb_ref, hgg_ref, glf_ref, glb_ref,
                  glg_ref, att_ref, bg_ref, cw_ref, cb_ref, lng_ref, lnb_ref, hgn_ref, gln_ref,
                  ones_ref, wbc_ref, wbh_ref, wbg_ref, wba_ref, wo_ref, gf_ref, rw_ref, rb_ref, tri_ref,
                  upper_ref, xo_ref, h2_ref, slot_ref, gran_ref, zs_ref, *, tiles_per_batch, n_ctx_tiles):
    i = pl.program_id(0)
    tt = x_ref.shape[0]
    ti = i % tiles_per_batch
    has_prev = jnp.logical_and(ti != 0, ti != n_ctx_tiles)
    has_next = jnp.logical_and(ti != n_ctx_tiles - 1, ti != tiles_per_batch - 1)

    zs_ref[0:HALO, :] = jnp.where(has_prev, zp_ref[...], 0.0)
    zs_ref[HALO:HALO + tt, :] = z_ref[...]
    zs_ref[HALO + tt:HALO + tt + HALO, :] = jnp.where(has_next, zn_ref[...], 0.0)
    acc = jnp.zeros((tt, CONV_CH), F32) + cb_ref[...]
    for j in range(CONV_K):
        off = HALO - CONV_K // 2 + j
        acc = acc + zs_ref[off:off + tt, :] * cw_ref[j:j + 1, :]
    mu = jnp.mean(acc, axis=-1, keepdims=True)
    cen = acc - mu
    var = jnp.mean(cen * cen, axis=-1, keepdims=True)
    conv_y = _silu(cen * lax.rsqrt(var + EPS) * lng_ref[...] + lnb_ref[...])

    ones_bd = ones_ref[...]
    hg_y = _head_norm(hgf_ref[...] + hgb_ref[...], ones_bd, hgn_ref[...]) * hgg_ref[...]
    gla_y = _head_norm(glf_ref[...] + glb_ref[...], ones_bd, gln_ref[...]) * glg_ref[...]

    d = x_ref.shape[1]
    merged = bg_ref[:, 0:d].astype(F32) * _dot(conv_y.astype(BF16), wbc_ref[...])
    merged = merged + bg_ref[:, d:2 * d].astype(F32) * _dot(hg_y.astype(BF16), wbh_ref[...])
    merged = merged + bg_ref[:, 2 * d:3 * d].astype(F32) * _dot(gla_y.astype(BF16), wbg_ref[...])
    merged = merged + bg_ref[:, 3 * d:4 * d].astype(F32) * _dot(att_ref[...].astype(BF16), wba_ref[...])
    mix = _dot(merged.astype(BF16), wo_ref[...])

    x_new = x_ref[...] + mod_ref[0, 2:3, :] * mix
    xo_ref[...] = x_new
    h2 = _modnorm(x_new, gf_ref[...], mod_ref[0, 3:4, :], mod_ref[0, 4:5, :])
    h2_ref[...] = h2
    gate, sel = _route_select(_sigmoid(_dot_f32(h2, rw_ref[...])), rb_ref[...])

    incl = _dot(tri_ref[...], sel.astype(F32).astype(BF16))
    gran = jnp.floor((incl[tt - 1:tt, :] + (GRANULE - 1.0)) * (1.0 / GRANULE))
    gran8 = jnp.broadcast_to(gran, (8, LANES))
    start = _dot(gran8.astype(BF16), upper_ref[...])[0:1, :]
    pos = GRANULE * start + incl - 1.0
    p0 = jnp.min(jnp.where(sel, pos, 1e9), axis=-1, keepdims=True)
    p1 = jnp.max(jnp.where(sel, pos, -1.0), axis=-1, keepdims=True)
    g0 = jnp.sum(jnp.where(jnp.logical_and(sel, pos == p0), gate, 0.0), axis=-1, keepdims=True)
    g1 = jnp.sum(jnp.where(jnp.logical_and(sel, pos == p1), gate, 0.0), axis=-1, keepdims=True)
    lane = lax.broadcasted_iota(jnp.int32, (tt, LANES), 1)
    slot_ref[...] = jnp.where(lane == 0, p0, jnp.where(lane == 1, p1, jnp.where(lane == 2, g0,
                              jnp.where(lane == 3, g1, 0.0))))
    gran_ref[0] = gran8.astype(jnp.int32)


def merge(x, modv, conv_z, hg_pack, hg_f, hg_b, gla_pack, gla_f, gla_b, att_o, bgate, lw, n_batch,
          n_ctx_tiles):
    n, d = x.shape
    nt = n // TOK_TILE
    tpb = nt // n_batch
    hpt = TOK_TILE // HALO
    n_halo = n // HALO

    def mod_idx(i):
        return (i // tpb) * 2 + ((i % tpb) >= n_ctx_tiles).astype(jnp.int32)

    row = lambda i: (i, 0)
    const = lambda i: (0, 0)
    col = lambda c: (lambda i: (i, c))
    full = lambda a: pl.BlockSpec(a.shape, const)
    weights = [lw["conv_w"], lw["conv_b"], lw["conv_ln_g"], lw["conv_ln_b"], lw["hg_norm_g"],
               lw["gla_norm_g"], lw["ones_bd"], lw["w_br_conv"], lw["w_br_hg"], lw["w_br_gla"],
               lw["w_br_att"], lw["w_out"], lw["g_ffn"], lw["router_w"], lw["router_b"], lw["tri"],
               lw["upper"]]
    return pl.pallas_call(
        functools.partial(_merge_kernel, tiles_per_batch=tpb, n_ctx_tiles=n_ctx_tiles),
        grid=(nt,),
        in_specs=[pl.BlockSpec((TOK_TILE, d), row),
                  pl.BlockSpec((1, N_MOD, d), lambda i: (mod_idx(i), 0, 0)),
                  pl.BlockSpec((HALO, CONV_CH), lambda i: (jnp.maximum(i * hpt - 1, 0), 0)),
                  pl.BlockSpec((TOK_TILE, CONV_CH), row),
                  pl.BlockSpec((HALO, CONV_CH), lambda i: (jnp.minimum((i + 1) * hpt, n_halo - 1), 0)),
                  pl.BlockSpec((TOK_TILE, 256), row),
                  pl.BlockSpec((TOK_TILE, 256), row),
                  pl.BlockSpec((TOK_TILE, 256), col(6)),
                  pl.BlockSpec((TOK_TILE, 256), row),
                  pl.BlockSpec((TOK_TILE, 256), row),
                  pl.BlockSpec((TOK_TILE, 256), col(3)),
                  pl.BlockSpec((TOK_TILE, 256), row),
                  pl.BlockSpec((TOK_TILE, 4 * d), row)] + [full(w) for w in weights],
        out_specs=[pl.BlockSpec((TOK_TILE, d), row),
                   pl.BlockSpec((TOK_TILE, d), row),
                   pl.BlockSpec((TOK_TILE, LANES), row),
                   pl.BlockSpec((1, 8, LANES), lambda i: (i, 0, 0))],
        out_shape=[jax.ShapeDtypeStruct((n, d), F32),
                   jax.ShapeDtypeStruct((n, d), F32),
                   jax.ShapeDtypeStruct((n, LANES), F32),
                   jax.ShapeDtypeStruct((nt, 8, LANES), jnp.int32)],
        scratch_shapes=[pltpu.VMEM((TOK_TILE + 2 * HALO, CONV_CH), F32)],
        compiler_params=_cparams(("parallel",)),
        name="merge",
    )(x, modv, conv_z, conv_z, conv_z, hg_f, hg_b, hg_pack, gla_f, gla_b, gla_pack, att_o, bgate,
      *weights)


def _granule_copy(src, src_row, dst, dst_row, sem):
    return pltpu.make_async_copy(src.at[pl.ds(pl.multiple_of(src_row, GRANULE), GRANULE)],
                                 dst.at[pl.ds(pl.multiple_of(dst_row, GRANULE), GRANULE)], sem)


def _slot_matrix(slot_ref, width, weighted):
    tt = slot_ref.shape[0]
    col = lax.broadcasted_iota(jnp.int32, (tt, width), 1).astype(F32)
    hit0 = col == slot_ref[:, 0:1]
    hit1 = col == slot_ref[:, 1:2]
    if not weighted:
        return jnp.where(jnp.logical_or(hit0, hit1), 1.0, 0.0)
    return jnp.where(hit0, slot_ref[:, 2:3], 0.0) + jnp.where(hit1, slot_ref[:, 3:4], 0.0)


def _dispatch_kernel(tab_ref, tabp_ref, tail_ref, h_ref, slot_ref, xs_hbm, buf, zbuf, sem, zsem, bsem):
    i = pl.program_id(0)
    nb = pl.num_programs(0)
    slot = i % 2
    ng = tab_ref[0, 0, LANES - 1]

    perm = _slot_matrix(slot_ref, buf.shape[1], weighted=False).astype(BF16)
    buf[slot] = lax.dot_general(perm, h_ref[...].astype(BF16), (((0,), (0,)), ((), ())),
                                preferred_element_type=F32)

    def issue(j, carry):
        _granule_copy(buf.at[slot], j * GRANULE, xs_hbm, tab_ref[0, 0, j] * GRANULE, sem.at[slot]).start()
        return carry
    lax.fori_loop(0, ng, issue, 0)

    def drain(count, s):
        def body(j, carry):
            _granule_copy(buf.at[s], 0, xs_hbm, 0, sem.at[s]).wait()
            return carry
        lax.fori_loop(0, count, body, 0)

    @pl.when(i > 0)
    def _():
        drain(tabp_ref[0, 0, LANES - 1], 1 - slot)

    @pl.when(i == nb - 1)
    def _():
        drain(ng, slot)

    @pl.when(i == 0)
    def _():
        zbuf[...] = jnp.zeros_like(zbuf)
        n_used = tail_ref[0, 0, 2 * N_EXPERTS]
        n_blocks = xs_hbm.shape[0] // MOE_BLOCK

        def block_copy(b):
            return pltpu.make_async_copy(
                zbuf, xs_hbm.at[pl.ds(pl.multiple_of(b * MOE_BLOCK, MOE_BLOCK), MOE_BLOCK)], bsem)

        for e in range(N_EXPERTS):
            def fill(m, carry, e=e):
                _granule_copy(zbuf, 0, xs_hbm, (tail_ref[0, 0, e] + m) * GRANULE, zsem).start()
                return carry
            lax.fori_loop(0, tail_ref[0, 0, N_EXPERTS + e], fill, 0)

        def fill_block(b, carry):
            block_copy(b).start()
            return carry
        lax.fori_loop(n_used, n_blocks, fill_block, 0)

        for e in range(N_EXPERTS):
            def done(m, carry):
                _granule_copy(zbuf, 0, xs_hbm, 0, zsem).wait()
                return carry
            lax.fori_loop(0, tail_ref[0, 0, N_EXPERTS + e], done, 0)

        def done_block(b, carry):
            block_copy(b).wait()
            return carry
        lax.fori_loop(n_used, n_blocks, done_block, 0)


def dispatch(h2, slots, table, tails, n_rows):
    n, d = h2.shape
    nt = n // TOK_TILE
    return pl.pallas_call(
        _dispatch_kernel,
        grid=(nt,),
        in_specs=[pl.BlockSpec((1, 1, LANES), lambda i: (i, 0, 0), memory_space=pltpu.SMEM),
                  pl.BlockSpec((1, 1, LANES), lambda i: (jnp.maximum(i - 1, 0), 0, 0), memory_space=pltpu.SMEM),
                  pl.BlockSpec((1, 1, LANES), lambda i: (0, 0, 0), memory_space=pltpu.SMEM),
                  pl.BlockSpec((TOK_TILE, d), lambda i: (i, 0)),
                  pl.BlockSpec((TOK_TILE, LANES), lambda i: (i, 0))],
        out_specs=pl.BlockSpec(memory_space=pl.ANY),
        out_shape=jax.ShapeDtypeStruct((n_rows, d), F32),
        scratch_shapes=[pltpu.VMEM((2, SORT_ROWS, d), F32),
                        pltpu.VMEM((MOE_BLOCK, d), F32),
                        pltpu.SemaphoreType.DMA((2,)),
                        pltpu.SemaphoreType.DMA(()),
                        pltpu.SemaphoreType.DMA(())],
        compiler_params=_cparams(("arbitrary",)),
        name="dispatch",
    )(table, table, tails, h2, slots)


def _expert_kernel(be_ref, nu_ref, x_ref, w1_ref, w3_ref, w2_ref, y_ref, w1b, w3b, w2b):
    i = pl.program_id(0)
    n_used = nu_ref[0]

    @pl.when(i < n_used)
    def _():
        first = jnp.logical_or(i == 0, be_ref[i] != be_ref[jnp.maximum(i - 1, 0)])

        @pl.when(first)
        def _():
            w1b[...] = w1_ref[0].astype(BF16)
            w3b[...] = w3_ref[0].astype(BF16)
            w2b[...] = w2_ref[0].astype(BF16)

        x = x_ref[...].astype(BF16)
        hid = _silu(_dot(x, w1b[...])) * _dot(x, w3b[...])
        y_ref[...] = _dot(hid.astype(BF16), w2b[...])

    @pl.when(i >= n_used)
    def _():
        y_ref[...] = jnp.zeros_like(y_ref)


def experts(xs, block_expert, n_used, w1, w3, w2):
    n_rows, d = xs.shape
    n_blocks = n_rows // MOE_BLOCK
    ff = w1.shape[-1]
    grid_spec = pltpu.PrefetchScalarGridSpec(
        num_scalar_prefetch=2,
        grid=(n_blocks,),
        in_specs=[pl.BlockSpec((MOE_BLOCK, d), lambda i, be, nu: (jnp.minimum(i, nu[0] - 1), 0)),
                  pl.BlockSpec((1, d, ff), lambda i, be, nu: (be[i], 0, 0)),
                  pl.BlockSpec((1, d, ff), lambda i, be, nu: (be[i], 0, 0)),
                  pl.BlockSpec((1, ff, d), lambda i, be, nu: (be[i], 0, 0))],
        out_specs=pl.BlockSpec((MOE_BLOCK, d), lambda i, be, nu: (i, 0)),
        scratch_shapes=[pltpu.VMEM((d, ff), BF16),
                        pltpu.VMEM((d, ff), BF16),
                        pltpu.VMEM((ff, d), BF16)],
    )
    return pl.pallas_call(
        _expert_kernel,
        grid_spec=grid_spec,
        out_shape=jax.ShapeDtypeStruct((n_rows, d), F32),
        compiler_params=_cparams(("arbitrary",)),
        name="experts",
    )(block_expert, n_used, xs, w1, w3, w2)


def _combine_kernel(d_ref, dn_ref, x_ref, mod_ref, g_ref, slot_ref, y_hbm, o_ref, ybuf, sem, *, final):
    i = pl.program_id(0)
    nb = pl.num_programs(0)
    slot = i % 2
    n_gran = ybuf.shape[1] // GRANULE

    def fetch(tab, s):
        def body(j, carry):
            _granule_copy(y_hbm, tab[0, 0, j] * GRANULE, ybuf.at[s], j * GRANULE, sem.at[s]).start()
            return carry
        lax.fori_loop(0, n_gran, body, 0)

    @pl.when(i == 0)
    def _():
        fetch(d_ref, 0)

    @pl.when(i + 1 < nb)
    def _():
        fetch(dn_ref, 1 - slot)

    def drain(j, carry):
        _granule_copy(y_hbm, 0, ybuf.at[slot], 0, sem.at[slot]).wait()
        return carry
    lax.fori_loop(0, n_gran, drain, 0)

    hi, lo = _split(_slot_matrix(slot_ref, ybuf.shape[1], weighted=True))
    yb = ybuf[slot].astype(BF16)
    y = _dot(hi, yb) + _dot(lo, yb)
    x = x_ref[...] + mod_ref[0, 5:6, :] * y
    if final:
        x = x * lax.rsqrt(jnp.mean(x * x, axis=-1, keepdims=True) + EPS) * g_ref[...]
    o_ref[...] = x


def combine(x, modv, ys, table, slots, final_g, n_batch, n_ctx_tiles, final):
    n, d = x.shape
    nt = n // TOK_TILE
    tpb = nt // n_batch
    if final:
        lat = tpb - n_ctx_tiles
        steps = n_batch * lat
        tile = lambda s: (s // lat) * tpb + n_ctx_tiles + s % lat
        out_rows = steps * TOK_TILE
    else:
        steps = nt
        tile = lambda s: s
        out_rows = n

    def mod_idx(s):
        t = tile(s)
        return (t // tpb) * 2 + ((t % tpb) >= n_ctx_tiles).astype(jnp.int32)

    return pl.pallas_call(
        functools.partial(_combine_kernel, final=final),
        grid=(steps,),
        in_specs=[pl.BlockSpec((1, 1, LANES), lambda s: (tile(s), 0, 0), memory_space=pltpu.SMEM),
                  pl.BlockSpec((1, 1, LANES), lambda s: (tile(jnp.minimum(s + 1, steps - 1)), 0, 0),
                               memory_space=pltpu.SMEM),
                  pl.BlockSpec((TOK_TILE, d), lambda s: (tile(s), 0)),
                  pl.BlockSpec((1, N_MOD, d), lambda s: (mod_idx(s), 0, 0)),
                  pl.BlockSpec((1, d), lambda s: (0, 0)),
                  pl.BlockSpec((TOK_TILE, LANES), lambda s: (tile(s), 0)),
                  pl.BlockSpec(memory_space=pl.ANY)],
        out_specs=pl.BlockSpec((TOK_TILE, d), lambda s: (s, 0)),
        out_shape=jax.ShapeDtypeStruct((out_rows, d), F32),
        scratch_shapes=[pltpu.VMEM((2, SORT_ROWS, d), F32),
                        pltpu.SemaphoreType.DMA((2,))],
        compiler_params=_cparams(("arbitrary",)),
        name="combine_final" if final else "combine",
    )(table, table, x, modv, final_g, slots, ys)


def sorted_rows_bound(n_tok):
    nt = n_tok // TOK_TILE
    rows = n_tok * TOP_K + nt * N_EXPERTS * (GRANULE - 1) + N_EXPERTS * (MOE_BLOCK - GRANULE)
    return -(-rows // MOE_BLOCK) * MOE_BLOCK


def moe_tables(gran, n_rows):
    nt = gran.shape[0]
    per_blk = MOE_BLOCK // GRANULE
    local = jnp.cumsum(gran, axis=1) - gran
    n_gran = jnp.sum(gran, axis=1)
    before = jnp.cumsum(gran, axis=0) - gran
    total = jnp.sum(gran, axis=0)
    padded = (total + per_blk - 1) // per_blk * per_blk
    region_end = jnp.cumsum(padded)
    region = region_end - padded
    j = jnp.arange(LANES, dtype=jnp.int32)[None, None, :]
    inside = jnp.logical_and(j >= local[:, :, None], j < (local + gran)[:, :, None])
    dst = region[None, :, None] + before[:, :, None] + j - local[:, :, None]
    table = jnp.sum(jnp.where(inside, dst, 0), axis=1)
    valid = j[0] < n_gran[:, None]
    table = jnp.where(valid, table, table[:, :1])
    table = table.at[:, LANES - 1].set(n_gran).astype(jnp.int32).reshape(nt, 1, LANES)
    tails = jnp.zeros((LANES,), jnp.int32).at[:N_EXPERTS].set(region + total)
    tails = tails.at[N_EXPERTS:2 * N_EXPERTS].set(padded - total)
    n_blocks = n_rows // MOE_BLOCK
    blk = jnp.arange(n_blocks, dtype=jnp.int32) * per_blk
    block_expert = jnp.minimum(jnp.sum(blk[:, None] >= region_end[None, :], axis=1), N_EXPERTS - 1)
    n_used = (region_end[-1] // per_blk).astype(jnp.int32).reshape(1)
    tails = tails.at[2 * N_EXPERTS].set(n_used[0]).reshape(1, 1, LANES)
    return table, tails, block_expert.astype(jnp.int32), n_used


def _rope_tables(ctx_len, n_lat):
    rows = n_lat // GRID_W
    row = np.repeat(np.arange(rows, dtype=np.float32), GRID_W)
    col = np.tile(np.arange(GRID_W, dtype=np.float32), rows)
    half = ATT_HD // 4
    inv = jnp.asarray(ROPE_BASE, F32) ** (-jnp.arange(half, dtype=F32) / half)
    ang_r = jnp.asarray(row)[:, None] * inv
    ang_c = jnp.asarray(col)[:, None] * inv
    cos64 = jnp.concatenate([jnp.cos(ang_r)] * 2 + [jnp.cos(ang_c)] * 2, axis=1)
    sin64 = jnp.concatenate([-jnp.sin(ang_r), jnp.sin(ang_r), -jnp.sin(ang_c), jnp.sin(ang_c)], axis=1)
    cos64 = jnp.concatenate([jnp.ones((ctx_len, ATT_HD), F32), cos64], axis=0)
    sin64 = jnp.concatenate([jnp.zeros((ctx_len, ATT_HD), F32), sin64], axis=0)
    return jnp.tile(cos64, (1, ATT_HEADS)), jnp.tile(sin64, (1, ATT_HEADS))


def _pack_w_in(w):
    gla = w[:, 1792:2592]
    gla = jnp.concatenate([gla[:, 0:512], gla[:, 544:800], gla[:, 512:544],
                           jnp.zeros((w.shape[0], 96), w.dtype)], axis=1)
    return jnp.concatenate([w[:, :1792], gla, w[:, 2592:]], axis=1).astype(BF16)


def kernel(x, c, ctx, c_ctx, hg_lb_logits, router_w, router_b, final_g, w_mod, b_mod, g_mix, g_ffn,
           w_in, conv_w, conv_b, conv_ln_g, conv_ln_b, hg_norm_g, gla_w2, gla_b2, gla_norm_g, att_sink,
           w_br_conv, w_br_hg, w_br_gla, w_br_att, w_out, moe_w1, moe_w3, moe_w2):
    n_batch, n_lat, d = x.shape
    ctx_len = ctx.shape[1]
    depth = w_in.shape[0]
    assert ctx_len % TOK_TILE == 0 and n_lat % TOK_TILE == 0 and n_lat % GRID_W == 0
    seq = ctx_len + n_lat
    n_ctx_tiles = ctx_len // TOK_TILE

    xs = jnp.concatenate([ctx, x], axis=1).reshape(n_batch * seq, d)

    c_rows = jnp.zeros((8, d), F32).at[:n_batch].set(c).at[n_batch].set(c_ctx)
    mods = modulation(c_rows, w_mod, b_mod).reshape(depth, 8, N_MOD, d)
    modv = jnp.stack([mods[:, n_batch] if j % 2 == 0 else mods[:, j // 2] for j in range(2 * n_batch)], axis=1)

    lb_sm = jax.nn.softmax(hg_lb_logits.astype(F32), axis=0)
    lower = jnp.cumsum(lb_sm, axis=0) - lb_sm[0]
    cos_t, sin_t = _rope_tables(ctx_len, n_lat)
    ones_bd = jnp.asarray(np.arange(256)[:, None] // 64 == np.arange(256)[None, :] // 64, BF16)
    rw = jnp.zeros((d, LANES), F32).at[:, :N_EXPERTS].set(router_w.astype(F32))
    rb = jnp.zeros((1, LANES), F32).at[0, :N_EXPERTS].set(router_b.astype(F32))
    tri = jnp.asarray(np.tril(np.ones((TOK_TILE, TOK_TILE))), BF16)
    upper = jnp.asarray(np.triu(np.ones((LANES, LANES)), k=1), BF16)

    for l in range(depth):
        w2p = jnp.zeros((LANES, 2 * GLA_K), F32)
        w2p = w2p.at[0:GLA_RANK, 0:GLA_K].set(gla_w2[l, 0]).at[GLA_RANK:2 * GLA_RANK, GLA_K:].set(gla_w2[l, 1])
        b2p = gla_b2[l].reshape(1, 2 * GLA_K)
        conv_z, hg_pack, gla_pack, att_pack, bgate = inproj(
            xs, modv[l], g_mix[l].reshape(1, d), _pack_w_in(w_in[l]), lower[l], w2p, b2p, cos_t, sin_t, n_batch, n_ctx_tiles)

        hg_f = gated_scan(hg_pack, (0, 1, 3, 4), HG_W, n_batch, n_ctx_tiles, rev=False)
        hg_b = gated_scan(hg_pack, (0, 2, 3, 5), HG_W, n_batch, n_ctx_tiles, rev=True)
        gla_f = gated_scan(gla_pack, (0, 1, 2, 2), GLA_K, n_batch, n_ctx_tiles, rev=False)
        gla_b = gated_scan(gla_pack, (0, 1, 2, 3), GLA_K, n_batch, n_ctx_tiles, rev=True)
        att_o = attention(att_pack, att_sink[l].astype(F32), n_batch, ctx_len)

        lw = dict(conv_w=jnp.zeros((32, CONV_CH), F32).at[:CONV_K].set(conv_w[l]),
                  conv_b=conv_b[l].reshape(1, -1), conv_ln_g=conv_ln_g[l].reshape(1, -1),
                  conv_ln_b=conv_ln_b[l].reshape(1, -1), hg_norm_g=hg_norm_g[l].reshape(1, -1),
                  gla_norm_g=gla_norm_g[l].reshape(1, -1), ones_bd=ones_bd,
                  w_br_conv=w_br_conv[l].astype(BF16), w_br_hg=w_br_hg[l].astype(BF16),
                  w_br_gla=w_br_gla[l].astype(BF16), w_br_att=w_br_att[l].astype(BF16),
                  w_out=w_out[l].astype(BF16), g_ffn=g_ffn[l].reshape(1, d), router_w=rw, router_b=rb,
                  tri=tri, upper=upper)
        x_new, h2, slots, gran = merge(xs, modv[l], conv_z, hg_pack, hg_f, hg_b, gla_pack, gla_f, gla_b,
                                       att_o, bgate, lw, n_batch, n_ctx_tiles)

        n_rows = sorted_rows_bound(n_batch * seq)
        table, tails, block_expert, n_used = moe_tables(gran[:, 0, :N_EXPERTS], n_rows)
        x_sorted = dispatch(h2, slots, table, tails, n_rows)
        ys = experts(x_sorted, block_expert, n_used, moe_w1[l], moe_w3[l], moe_w2[l])
        final = l == depth - 1
        xs = combine(x_new, modv[l], ys, table, slots, final_g.reshape(1, d), n_batch, n_ctx_tiles, final)

    return xs.reshape(n_batch, n_lat, d)
```

```python
import functools
import itertools

import numpy as np
import jax
import jax.numpy as jnp
from jax import lax
from jax.experimental import pallas as pl
from jax.experimental.pallas import tpu as pltpu

F32 = jnp.float32
BF16 = jnp.bfloat16

EPS = 1e-6
NEG = -1e30
TINY = 1e-30
N_MOD = 6
CONV_CH = 256
CONV_K = 31
HG_HEADS = 4
HG_W = 256
GLA_HEADS = 4
GLA_K = 128
GLA_V = 256
GLA_RANK = 16
GLA_TAU = 16.0
ATT_HEADS = 4
ATT_KV_HEADS = 2
ATT_HD = 64
ATT_BLOCK = 128
GRID_W = 64
ROPE_BASE = 10000.0
N_EXPERTS = 16
N_GROUPS = 4
TOP_K = 2
MOE_BLOCK = 256

LANES = 128
TOK_TILE = 256
SCAN_BLOCK = 16
HALO = 16
GRANULE = 8
SORT_ROWS = -(-(TOP_K * TOK_TILE + N_EXPERTS * (GRANULE - 1)) // LANES) * LANES
VMEM_LIMIT = 56 * 1024 * 1024

W_CONV = (0, 512)
W_HG = (512, 1792)
W_GLA = (1792, 2688)
W_ATT = (2688, 3200)
W_BG = (3200, 7296)
W_IN_PACKED = 7296


def _cparams(sem):
    return pltpu.CompilerParams(dimension_semantics=sem, vmem_limit_bytes=VMEM_LIMIT)


def _dot(a, b):
    return jnp.dot(a, b, preferred_element_type=F32)


def _split(a):
    hi = a.astype(BF16)
    lo = (a - hi.astype(F32)).astype(BF16)
    return hi, lo


def _dot_f32(a, b):
    ah, al = _split(a)
    bh, bl = _split(b)
    return _dot(ah, bh) + _dot(ah, bl) + _dot(al, bh)


def _sigmoid(x):
    return 1.0 / (1.0 + jnp.exp(-x))


def _silu(x):
    return x * _sigmoid(x)


def _mod_kernel(c_ref, w_ref, b_ref, o_ref):
    c = c_ref[...]
    o_ref[0] = _dot_f32(_silu(c), w_ref[0]) + b_ref[0]


def modulation(c_rows, w_mod, b_mod):
    depth, d, six_d = w_mod.shape
    nblk = six_d // d
    return pl.pallas_call(
        _mod_kernel,
        grid=(depth, nblk),
        in_specs=[pl.BlockSpec((8, d), lambda l, j: (0, 0)),
                  pl.BlockSpec((1, d, d), lambda l, j: (l, 0, j)),
                  pl.BlockSpec((1, 1, d), lambda l, j: (l, 0, j))],
        out_specs=pl.BlockSpec((1, 8, d), lambda l, j: (l, 0, j)),
        out_shape=jax.ShapeDtypeStruct((depth, 8, six_d), F32),
        compiler_params=_cparams(("parallel", "parallel")),
        name="modulation",
    )(c_rows, w_mod, b_mod.reshape(depth, 1, six_d))


def _modnorm(x, g, shift, scale):
    y = x * lax.rsqrt(jnp.mean(x * x, axis=-1, keepdims=True) + EPS)
    return (y * g) * (1.0 + scale) + shift


def _inproj_kernel(x_ref, mod_ref, g_ref, w_ref, lb_ref, w2_ref, b2_ref, cos_ref, sin_ref,
                   conv_ref, hg_ref, gla_ref, att_ref, bg_ref):
    x = x_ref[...]
    h = _modnorm(x, g_ref[...], mod_ref[0, 0:1, :], mod_ref[0, 1:2, :]).astype(BF16)

    p = _dot(h, w_ref[:, W_CONV[0]:W_CONV[1]])
    conv_ref[...] = p[:, :CONV_CH] * _sigmoid(p[:, CONV_CH:])

    p = _dot(h, w_ref[:, W_HG[0]:W_HG[1]])
    hg_ref[:, 0:256] = p[:, 0:256]
    hg_ref[:, 768:1024] = p[:, 768:1024]
    hg_ref[:, 1536:1792] = _silu(p[:, 1024:1280])
    for d in range(2):
        z = p[:, 256 * (d + 1):256 * (d + 2)]
        lb = lb_ref[d:d + 1, :]
        hg_ref[:, 256 * (d + 1):256 * (d + 2)] = (1.0 - lb) * _sigmoid(-z)
        hg_ref[:, 256 * (d + 4):256 * (d + 5)] = jnp.maximum(lb + (1.0 - lb) * _sigmoid(z), TINY)

    p = _dot(h, w_ref[:, W_GLA[0]:W_GLA[1]])
    gla_ref[:, 0:128] = p[:, 0:128] * (float(GLA_K // GLA_HEADS) ** -0.5)
    gla_ref[:, 128:256] = p[:, 128:256]
    gla_ref[:, 512:768] = p[:, 256:512]
    gla_ref[:, 768:1024] = _silu(p[:, 512:768])
    u = _dot_f32(p[:, 768:896], w2_ref[...]) + b2_ref[...]
    log_sig = jnp.minimum(u, 0.0) - jnp.log(1.0 + jnp.exp(-jnp.abs(u)))
    gla_ref[:, 256:512] = jnp.exp(log_sig * (1.0 / GLA_TAU))

    p = _dot(h, w_ref[:, W_ATT[0]:W_ATT[1]])
    cos = cos_ref[...]
    sin = sin_ref[...]
    def rope(v, width):
        lane = lax.broadcasted_iota(jnp.int32, v.shape, 1)
        partner = jnp.where((lane & 31) < 16,
                            pltpu.roll(v, width - 16, axis=1), pltpu.roll(v, 16, axis=1))
        return v * cos[:, :width] + partner * sin[:, :width]

    att_ref[:, 0:256] = rope(p[:, 0:256], 256) * (float(ATT_HD) ** -0.5)
    att_ref[:, 256:384] = rope(p[:, 256:384], 128)
    att_ref[:, 384:512] = p[:, 384:512]

    p = _dot(h, w_ref[:, W_BG[0]:W_BG[1]])
    bg_ref[...] = _sigmoid(p).astype(BF16)


def inproj(x, modv, g_mix, w_in_p, lb, w2p, b2p, cos_t, sin_t, n_batch, n_ctx_tiles):
    n, d = x.shape
    nt = n // TOK_TILE
    tiles_per_batch = nt // n_batch

    def mod_idx(i):
        b = i // tiles_per_batch
        return b * 2 + ((i % tiles_per_batch) >= n_ctx_tiles).astype(jnp.int32)

    const = lambda i: (0, 0)
    row = lambda i: (i, 0)
    seq = lambda i: (i % tiles_per_batch, 0)
    outs = pl.pallas_call(
        _inproj_kernel,
        grid=(nt,),
        in_specs=[pl.BlockSpec((TOK_TILE, d), row),
                  pl.BlockSpec((1, N_MOD, d), lambda i: (mod_idx(i), 0, 0)),
                  pl.BlockSpec((1, d), const),
                  pl.BlockSpec((d, W_IN_PACKED), const, pipeline_mode=pl.Buffered(1)),
                  pl.BlockSpec((2, HG_W), const),
                  pl.BlockSpec((LANES, 2 * GLA_K), const),
                  pl.BlockSpec((1, 2 * GLA_K), const),
                  pl.BlockSpec((TOK_TILE, 256), seq),
                  pl.BlockSpec((TOK_TILE, 256), seq)],
        out_specs=[pl.BlockSpec((TOK_TILE, 256), row),
                   pl.BlockSpec((TOK_TILE, 1792), row),
                   pl.BlockSpec((TOK_TILE, 1024), row),
                   pl.BlockSpec((TOK_TILE, 512), row),
                   pl.BlockSpec((TOK_TILE, 4096), row)],
        out_shape=[jax.ShapeDtypeStruct((n, 256), F32),
                   jax.ShapeDtypeStruct((n, 1792), F32),
                   jax.ShapeDtypeStruct((n, 1024), F32),
                   jax.ShapeDtypeStruct((n, 512), F32),
                   jax.ShapeDtypeStruct((n, 4096), BF16)],
        compiler_params=_cparams(("parallel",)),
        name="inproj",
    )(x, modv, g_mix, w_in_p, lb, w2p, b2p, cos_t, sin_t)
    return outs


def _scan_kernel(qf_ref, kf_ref, vf_ref, ff_ref, qr_ref, kr_ref, vr_ref, fr_ref, sel_ref,
                 pf_ref, pft_ref, pr_ref, prt_ref, of_ref, or_ref, *scratch):
    half = len(scratch) // 2
    chains = [_scan_direction(qf_ref, kf_ref, vf_ref, ff_ref, sel_ref, pf_ref, pft_ref, of_ref,
                              *scratch[:half], rev=False),
              _scan_direction(qr_ref, kr_ref, vr_ref, fr_ref, sel_ref, pr_ref, prt_ref, or_ref,
                              *scratch[half:], rev=True)]
    for _ in itertools.zip_longest(*chains):
        pass


def _scan_direction(q_ref, k_ref, v_ref, f_ref, sel_ref, perm_ref, permt_ref, o_ref,
                    s_ref, qs_ref, ks_ref, vs_ref, fs_ref, qt_ref, kh_ref, term_ref, w_ref, ah_ref, al_ref,
                    kv_ref, *, rev):
    c = SCAN_BLOCK
    tt, kl = q_ref.shape
    nb = tt // c
    assert nb == c

    @pl.when(pl.program_id(1) == 0)
    def _():
        s_ref[...] = jnp.zeros_like(s_ref)

    perm = perm_ref[...]
    vb = v_ref[...].astype(BF16)
    f_hi, f_lo = _split(f_ref[...])
    qs_ref[...] = _dot(perm, q_ref[...].astype(BF16))
    ks_ref[...] = _dot(perm, k_ref[...].astype(BF16))
    vs_ref[...] = _dot(perm, vb)
    fs_ref[...] = _dot(perm, f_hi) + _dot(perm, f_lo)
    yield

    def slab(ref, s):
        return ref[s * nb:(s + 1) * nb, :]

    a = slab(fs_ref, 0)
    qt_ref[0:nb, :] = (slab(qs_ref, 0) * a).astype(BF16)
    for s in range(1, c):
        a = a * slab(fs_ref, s)
        qt_ref[s * nb:(s + 1) * nb, :] = (slab(qs_ref, s) * a).astype(BF16)
    gam_t = a.T
    g = jnp.ones((nb, kl), F32)
    for s in range(c - 1, -1, -1):
        kh_ref[s * nb:(s + 1) * nb, :] = (slab(ks_ref, s) * g).astype(BF16)
        g = g * slab(fs_ref, s)
    yield

    heads = 256 // 64
    dk = kl // heads
    perm_t = permt_ref[...]
    qt = _dot(perm_t, qt_ref[...]).astype(BF16)
    kh = _dot(perm_t, kh_ref[...]).astype(BF16)

    def head_lanes(x, h):
        return x[:, :dk] if h == 0 else pltpu.roll(x, kl - dk * h, axis=1)[:, :dk]

    q_heads = [head_lanes(qt, h) for h in range(heads)]
    k_heads = [head_lanes(kh, h) for h in range(heads)]
    v_head = lax.broadcasted_iota(jnp.int32, (c, 256), 1) // 64
    g_head = lax.broadcasted_iota(jnp.int32, (dk, 256), 1) // 64
    for j in range(nb):
        lo = j * c
        k4 = jnp.concatenate([k_heads[h][lo:lo + c, :] for h in range(heads)], axis=0)
        v4 = jnp.concatenate([jnp.where(v_head == h, vb[lo:lo + c, :], jnp.zeros_like(vb[lo:lo + c, :]))
                              for h in range(heads)], axis=0)
        kv_ref[j] = lax.dot_general(k4, v4, (((0,), (0,)), ((), ())), preferred_element_type=F32)
        yield

    n_pair = 0
    for s in range(c):
        p = slab(qs_ref, s)
        for d in range(s + 1):
            if d > 0:
                p = p * slab(fs_ref, s - d + 1)
            term_ref[n_pair * nb:(n_pair + 1) * nb, :] = (p * slab(ks_ref, s - d)).astype(BF16)
            n_pair += 1
        yield
    w_ref[...] = _dot(term_ref[...], sel_ref[...])
    yield
    n_pair = 0
    for s in range(c):
        acc = jnp.zeros((nb, 256), F32)
        for d in range(s + 1):
            acc = acc + w_ref[n_pair * nb:(n_pair + 1) * nb, :] * slab(vs_ref, s - d)
            n_pair += 1
        hi, lo = _split(acc)
        ah_ref[s * nb:(s + 1) * nb, :] = hi
        al_ref[s * nb:(s + 1) * nb, :] = lo
        yield

    o_ref[...] = _dot(perm_t, ah_ref[...]) + _dot(perm_t, al_ref[...])
    yield
    st = s_ref[...]
    for jj in range(nb):
        j = nb - 1 - jj if rev else jj
        lo = j * c
        q4 = jnp.concatenate([q_heads[h][lo:lo + c, :] for h in range(heads)], axis=0)
        o4 = _dot(q4, st.astype(BF16))
        inter = o4[0:c, :]
        for h in range(1, heads):
            inter = jnp.where(v_head == h, o4[h * c:(h + 1) * c, :], inter)
        o_ref[lo:lo + c, :] += inter
        gam = jnp.broadcast_to(gam_t[0:dk, j:j + 1], (dk, 256))
        for h in range(1, heads):
            gam = jnp.where(g_head == h, gam_t[h * dk:(h + 1) * dk, j:j + 1], gam)
        st = gam * st + kv_ref[j]
        yield
    s_ref[...] = st


def gated_scan(pack, cols_fwd, cols_rev, kl, n_batch, n_ctx_tiles, name):
    n = pack.shape[0]
    nt = n // TOK_TILE
    tpb = nt // n_batch

    def rev_tile(i):
        return jnp.where(i < n_ctx_tiles, n_ctx_tiles - 1 - i, tpb - 1 - (i - n_ctx_tiles))

    heads = 4
    dk = kl // heads
    sel = (np.arange(kl)[:, None] // dk == np.arange(256)[None, :] // 64)
    sel_b = jnp.asarray(sel, BF16)
    c = SCAN_BLOCK
    nb = TOK_TILE // c
    perms = []
    for rev in (False, True):
        perm = np.zeros((TOK_TILE, TOK_TILE), np.float32)
        for j in range(nb):
            for s in range(c):
                perm[s * nb + j, j * c + (c - 1 - s if rev else s)] = 1.0
        perms += [jnp.asarray(perm, BF16), jnp.asarray(perm.T, BF16)]
    n_pairs = c * (c + 1) // 2
    fwd = lambda col: (lambda b, i: (b * tpb + i, col))
    bwd = lambda col: (lambda b, i: (b * tpb + rev_tile(i), col))
    const = lambda b, i: (0, 0)

    def direction_specs(rows, cols):
        qc, kc, vc, fc = cols
        return [pl.BlockSpec((TOK_TILE, kl), rows(qc)),
                pl.BlockSpec((TOK_TILE, kl), rows(kc)),
                pl.BlockSpec((TOK_TILE, 256), rows(vc)),
                pl.BlockSpec((TOK_TILE, kl), rows(fc))]

    scratch = [pltpu.VMEM((dk, 256), F32),
               pltpu.VMEM((TOK_TILE, kl), F32),
               pltpu.VMEM((TOK_TILE, kl), F32),
               pltpu.VMEM((TOK_TILE, 256), F32),
               pltpu.VMEM((TOK_TILE, kl), F32),
               pltpu.VMEM((TOK_TILE, kl), BF16),
               pltpu.VMEM((TOK_TILE, kl), BF16),
               pltpu.VMEM((n_pairs * nb, kl), BF16),
               pltpu.VMEM((n_pairs * nb, 256), F32),
               pltpu.VMEM((TOK_TILE, 256), BF16),
               pltpu.VMEM((TOK_TILE, 256), BF16),
               pltpu.VMEM((nb, dk, 256), F32)]
    return pl.pallas_call(
        _scan_kernel,
        grid=(n_batch, tpb),
        in_specs=direction_specs(fwd, cols_fwd) + direction_specs(bwd, cols_rev)
        + [pl.BlockSpec((kl, 256), const)] + [pl.BlockSpec((TOK_TILE, TOK_TILE), const)] * 4,
        out_specs=[pl.BlockSpec((TOK_TILE, 256), fwd(0)), pl.BlockSpec((TOK_TILE, 256), bwd(0))],
        out_shape=[jax.ShapeDtypeStruct((n, 256), F32)] * 2,
        scratch_shapes=scratch + scratch,
        compiler_params=_cparams(("parallel", "arbitrary")),
        name=name,
    )(*([pack] * 8), sel_b, *perms)


def _attn_kernel(sink_ref, q_ref, kc_ref, vc_ref, k0_ref, k1_ref, k2_ref, v0_ref, v1_ref, v2_ref,
                 o_ref, *, ctx_blocks, blocks_per_batch):
    n = pl.program_id(1)
    blk = ATT_BLOCK
    lane = lax.broadcasted_iota(jnp.int32, (blk, LANES), 1)
    low = lane < ATT_HD
    qi = lax.broadcasted_iota(jnp.int32, (blk, blk), 0)
    ki = lax.broadcasted_iota(jnp.int32, (blk, blk), 1)
    is_lat = n >= ctx_blocks
    band_ok = [
        jnp.logical_and(jnp.logical_and(is_lat, n - 1 >= ctx_blocks), qi <= ki),
        jnp.logical_and(is_lat, qi >= 0),
        jnp.logical_and(jnp.logical_and(is_lat, n + 1 < blocks_per_batch), ki <= qi),
    ]

    def dup(x, hk):
        xr = pltpu.roll(x, ATT_HD, axis=1)
        lo_x = lax.broadcasted_iota(jnp.int32, x.shape, 1) < ATT_HD
        return (jnp.where(lo_x, x, xr) if hk == 0 else jnp.where(lo_x, xr, x)).astype(BF16)

    q = q_ref[...]
    keys = [kc_ref[...], k0_ref[...], k1_ref[...], k2_ref[...]]
    vals = [vc_ref[...], v0_ref[...], v1_ref[...], v2_ref[...]]
    nt_dims = (((1,), (1,)), ((), ()))
    halves = []
    for hk in range(ATT_KV_HEADS):
        kd = [dup(x, hk) for x in keys]
        vd = [dup(x, hk) for x in vals]
        qh = q[:, hk * LANES:(hk + 1) * LANES]
        q2 = jnp.concatenate([jnp.where(low, qh, 0.0), jnp.where(low, 0.0, qh)], axis=0).astype(BF16)
        s = [lax.dot_general(q2, kx, nt_dims, preferred_element_type=F32) for kx in kd]
        for t in range(3):
            s[t + 1] = jnp.where(jnp.concatenate([band_ok[t]] * 2, axis=0), s[t + 1], NEG)
        sink = jnp.where(lax.broadcasted_iota(jnp.int32, (2 * blk, 1), 0) < blk,
                         sink_ref[hk * 2], sink_ref[hk * 2 + 1])
        pieces = [s[0][:, :LANES], s[0][:, LANES:], s[1], s[2], s[3]]
        top = pieces[0]
        for piece in pieces[1:]:
            top = jnp.maximum(top, piece)
        m = jnp.maximum(sink, top.max(axis=-1, keepdims=True))
        e = [jnp.exp(x - m) for x in s]
        tot = e[0][:, :LANES] + e[0][:, LANES:] + e[1] + e[2] + e[3]
        den = jnp.exp(sink - m) + tot.sum(axis=-1, keepdims=True)
        o = _dot(e[0].astype(BF16), vd[0])
        for t in range(1, 4):
            o = o + _dot(e[t].astype(BF16), vd[t])
        o = o / den
        halves.append(jnp.where(low, o[:blk, :], o[blk:, :]))
    o_ref[:, 0:LANES] = halves[0]
    o_ref[:, LANES:2 * LANES] = halves[1]


def attention(att_pack, sink, n_batch, ctx_len):
    n = att_pack.shape[0]
    bpb = n // n_batch // ATT_BLOCK
    cb = ctx_len // ATT_BLOCK

    def band(off, col):
        def idx(b, i, *_):
            m = jnp.clip(i + off, cb, bpb - 1)
            return (b * bpb + m, col)
        return pl.BlockSpec((ATT_BLOCK, LANES), idx)

    def ctx(col):
        return pl.BlockSpec((ctx_len, LANES), lambda b, i, *_: (b * (bpb * ATT_BLOCK // ctx_len), col))

    grid_spec = pltpu.PrefetchScalarGridSpec(
        num_scalar_prefetch=1,
        grid=(n_batch, bpb),
        in_specs=[pl.BlockSpec((ATT_BLOCK, 256), lambda b, i, *_: (b * bpb + i, 0)),
                  ctx(2), ctx(3),
                  band(-1, 2), band(0, 2), band(1, 2),
                  band(-1, 3), band(0, 3), band(1, 3)],
        out_specs=pl.BlockSpec((ATT_BLOCK, 256), lambda b, i, *_: (b * bpb + i, 0)),
    )
    return pl.pallas_call(
        functools.partial(_attn_kernel, ctx_blocks=cb, blocks_per_batch=bpb),
        grid_spec=grid_spec,
        out_shape=jax.ShapeDtypeStruct((n, 256), F32),
        compiler_params=_cparams(("parallel", "parallel")),
        name="attention",
    )(sink, *([att_pack] * 9))


def _head_norm(o, ones_bd, g):
    sq = o * o
    hi, lo = _split(sq)
    ms = (_dot(hi, ones_bd) + _dot(lo, ones_bd)) * (1.0 / 64.0)
    return o * lax.rsqrt(ms + EPS) * g


def _route_select(aff, bias):
    lane = lax.broadcasted_iota(jnp.int32, aff.shape, 1)
    epg = N_EXPERTS // N_GROUPS
    pos = lane & (epg - 1)
    v = aff + bias

    def nxt(x, o):
        return pltpu.roll(x, LANES - o, axis=1)

    def prv(x, o):
        return pltpu.roll(x, o, axis=1)

    beaten = jnp.zeros(aff.shape, jnp.int32)
    for o in range(1, epg):
        beaten = beaten + jnp.where(jnp.logical_and(pos + o < epg, nxt(v, o) > v), 1, 0)
        beaten = beaten + jnp.where(jnp.logical_and(pos >= o, prv(v, o) >= v), 1, 0)
    top2 = beaten < TOP_K
    t = jnp.where(top2, v, 0.0)
    score = t
    for o in range(1, epg):
        score = score + jnp.where(pos + o < epg, nxt(t, o), 0.0) + jnp.where(pos >= o, prv(t, o), 0.0)
    worse = jnp.zeros(aff.shape, jnp.int32)
    for o in range(epg, N_EXPERTS, epg):
        worse = worse + jnp.where(jnp.logical_and(lane + o < N_EXPERTS, nxt(score, o) > score), 1, 0)
        worse = worse + jnp.where(jnp.logical_and(lane >= o, prv(score, o) >= score), 1, 0)
    sel = jnp.logical_and(jnp.logical_and(lane < N_EXPERTS, worse == 0), top2)
    picked = jnp.where(sel, aff, 0.0)
    gate = picked / jnp.sum(picked, axis=-1, keepdims=True)
    return gate, sel


def _merge_kernel(x_ref, mod_ref, zp_ref, z_ref, zn_ref, hgf_ref, hgb_ref, hgg_ref, glf_ref, glb_ref,
                  glg_ref, att_ref, bg_ref, cw_ref, cb_ref, lng_ref, lnb_ref, hgn_ref, gln_ref,
                  ones_ref, wbc_ref, wbh_ref, wbg_ref, wba_ref, wo_ref, gf_ref, rw_ref, rb_ref, tri_ref,
                  upper_ref, xo_ref, h2_ref, slot_ref, gran_ref, zs_ref, zsh_ref, *, tiles_per_batch,
                  n_ctx_tiles):
    i = pl.program_id(0)
    tt = x_ref.shape[0]
    ti = i % tiles_per_batch
    has_prev = jnp.logical_and(ti != 0, ti != n_ctx_tiles)
    has_next = jnp.logical_and(ti != n_ctx_tiles - 1, ti != tiles_per_batch - 1)

    zs_ref[0:HALO, :] = jnp.where(has_prev, zp_ref[...], 0.0)
    zs_ref[HALO:HALO + tt, :] = z_ref[...]
    zs_ref[HALO + tt:HALO + tt + HALO, :] = jnp.where(has_next, zn_ref[...], 0.0)
    span = tt + 2 * HALO - 8
    for b in range(8):
        zsh_ref[b] = zs_ref[b:b + span, :]
    acc = jnp.zeros((tt, CONV_CH), F32) + cb_ref[...]
    for j in range(CONV_K):
        off = HALO - CONV_K // 2 + j
        acc = acc + zsh_ref[off % 8, off - off % 8:off - off % 8 + tt, :] * cw_ref[j:j + 1, :]
    mu = jnp.mean(acc, axis=-1, keepdims=True)
    cen = acc - mu
    var = jnp.mean(cen * cen, axis=-1, keepdims=True)
    conv_y = _silu(cen * lax.rsqrt(var + EPS) * lng_ref[...] + lnb_ref[...])

    ones_bd = ones_ref[...]
    hg_y = _head_norm(hgf_ref[...] + hgb_ref[...], ones_bd, hgn_ref[...]) * hgg_ref[...]
    gla_y = _head_norm(glf_ref[...] + glb_ref[...], ones_bd, gln_ref[...]) * glg_ref[...]

    d = x_ref.shape[1]
    merged = bg_ref[:, 0:d].astype(F32) * _dot(conv_y.astype(BF16), wbc_ref[...])
    merged = merged + bg_ref[:, d:2 * d].astype(F32) * _dot(hg_y.astype(BF16), wbh_ref[...])
    merged = merged + bg_ref[:, 2 * d:3 * d].astype(F32) * _dot(gla_y.astype(BF16), wbg_ref[...])
    merged = merged + bg_ref[:, 3 * d:4 * d].astype(F32) * _dot(att_ref[...].astype(BF16), wba_ref[...])
    mix = _dot(merged.astype(BF16), wo_ref[...])

    x_new = x_ref[...] + mod_ref[0, 2:3, :] * mix
    xo_ref[...] = x_new
    h2 = _modnorm(x_new, gf_ref[...], mod_ref[0, 3:4, :], mod_ref[0, 4:5, :])
    h2_ref[...] = h2
    gate, sel = _route_select(_sigmoid(_dot_f32(h2, rw_ref[...])), rb_ref[...])

    incl = _dot(tri_ref[...], sel.astype(F32).astype(BF16))
    gran = jnp.floor((incl[tt - 1:tt, :] + (GRANULE - 1.0)) * (1.0 / GRANULE))
    gran8 = jnp.broadcast_to(gran, (8, LANES))
    start = _dot(gran8.astype(BF16), upper_ref[...])[0:1, :]
    pos = GRANULE * start + incl - 1.0
    p0 = jnp.min(jnp.where(sel, pos, 1e9), axis=-1, keepdims=True)
    p1 = jnp.max(jnp.where(sel, pos, -1.0), axis=-1, keepdims=True)
    g0 = jnp.sum(jnp.where(jnp.logical_and(sel, pos == p0), gate, 0.0), axis=-1, keepdims=True)
    g1 = jnp.sum(jnp.where(jnp.logical_and(sel, pos == p1), gate, 0.0), axis=-1, keepdims=True)
    lane = lax.broadcasted_iota(jnp.int32, (tt, LANES), 1)
    slot_ref[...] = jnp.where(lane == 0, p0, jnp.where(lane == 1, p1, jnp.where(lane == 2, g0,
                              jnp.where(lane == 3, g1, 0.0))))
    gran_ref[0] = gran8.astype(jnp.int32)


def merge(x, modv, conv_z, hg_pack, hg_f, hg_b, gla_pack, gla_f, gla_b, att_o, bgate, lw, n_batch,
          n_ctx_tiles):
    n, d = x.shape
    nt = n // TOK_TILE
    tpb = nt // n_batch
    hpt = TOK_TILE // HALO
    n_halo = n // HALO

    def mod_idx(i):
        return (i // tpb) * 2 + ((i % tpb) >= n_ctx_tiles).astype(jnp.int32)

    row = lambda i: (i, 0)
    const = lambda i: (0, 0)
    col = lambda c: (lambda i: (i, c))
    full = lambda a: pl.BlockSpec(a.shape, const)
    weights = [lw["conv_w"], lw["conv_b"], lw["conv_ln_g"], lw["conv_ln_b"], lw["hg_norm_g"],
               lw["gla_norm_g"], lw["ones_bd"], lw["w_br_conv"], lw["w_br_hg"], lw["w_br_gla"],
               lw["w_br_att"], lw["w_out"], lw["g_ffn"], lw["router_w"], lw["router_b"], lw["tri"],
               lw["upper"]]
    return pl.pallas_call(
        functools.partial(_merge_kernel, tiles_per_batch=tpb, n_ctx_tiles=n_ctx_tiles),
        grid=(nt,),
        in_specs=[pl.BlockSpec((TOK_TILE, d), row),
                  pl.BlockSpec((1, N_MOD, d), lambda i: (mod_idx(i), 0, 0)),
                  pl.BlockSpec((HALO, CONV_CH), lambda i: (jnp.maximum(i * hpt - 1, 0), 0)),
                  pl.BlockSpec((TOK_TILE, CONV_CH), row),
                  pl.BlockSpec((HALO, CONV_CH), lambda i: (jnp.minimum((i + 1) * hpt, n_halo - 1), 0)),
                  pl.BlockSpec((TOK_TILE, 256), row),
                  pl.BlockSpec((TOK_TILE, 256), row),
                  pl.BlockSpec((TOK_TILE, 256), col(6)),
                  pl.BlockSpec((TOK_TILE, 256), row),
                  pl.BlockSpec((TOK_TILE, 256), row),
                  pl.BlockSpec((TOK_TILE, 256), col(3)),
                  pl.BlockSpec((TOK_TILE, 256), row),
                  pl.BlockSpec((TOK_TILE, 4 * d), row)] + [full(w) for w in weights],
        out_specs=[pl.BlockSpec((TOK_TILE, d), row),
                   pl.BlockSpec((TOK_TILE, d), row),
                   pl.BlockSpec((TOK_TILE, LANES), row),
                   pl.BlockSpec((1, 8, LANES), lambda i: (i, 0, 0))],
        out_shape=[jax.ShapeDtypeStruct((n, d), F32),
                   jax.ShapeDtypeStruct((n, d), F32),
                   jax.ShapeDtypeStruct((n, LANES), F32),
                   jax.ShapeDtypeStruct((nt, 8, LANES), jnp.int32)],
        scratch_shapes=[pltpu.VMEM((TOK_TILE + 2 * HALO, CONV_CH), F32),
                        pltpu.VMEM((8, TOK_TILE + 2 * HALO - 8, CONV_CH), F32)],
        compiler_params=_cparams(("parallel",)),
        name="merge",
    )(x, modv, conv_z, conv_z, conv_z, hg_f, hg_b, hg_pack, gla_f, gla_b, gla_pack, att_o, bgate,
      *weights)


def _granule_copy(src, src_row, dst, dst_row, sem):
    return pltpu.make_async_copy(src.at[pl.ds(pl.multiple_of(src_row, GRANULE), GRANULE)],
                                 dst.at[pl.ds(pl.multiple_of(dst_row, GRANULE), GRANULE)], sem)


def _slot_matrix(slot_ref, width, weighted):
    tt = slot_ref.shape[0]
    col = lax.broadcasted_iota(jnp.int32, (tt, width), 1).astype(F32)
    hit0 = col == slot_ref[:, 0:1]
    hit1 = col == slot_ref[:, 1:2]
    if not weighted:
        return jnp.where(jnp.logical_or(hit0, hit1), 1.0, 0.0)
    return jnp.where(hit0, slot_ref[:, 2:3], 0.0) + jnp.where(hit1, slot_ref[:, 3:4], 0.0)


def _dispatch_kernel(tab_ref, tabp_ref, tail_ref, h_ref, slot_ref, xs_hbm, buf, zbuf, sem, zsem, bsem):
    i = pl.program_id(0)
    nb = pl.num_programs(0)
    slot = i % 2
    ng = tab_ref[0, 0, LANES - 1]

    perm = _slot_matrix(slot_ref, buf.shape[1], weighted=False).astype(BF16)
    buf[slot] = lax.dot_general(perm, h_ref[...].astype(BF16), (((0,), (0,)), ((), ())),
                                preferred_element_type=F32)

    def issue(j, carry):
        _granule_copy(buf.at[slot], j * GRANULE, xs_hbm, tab_ref[0, 0, j] * GRANULE, sem.at[slot]).start()
        return carry
    lax.fori_loop(0, ng, issue, 0)

    def drain(count, s):
        def body(j, carry):
            _granule_copy(buf.at[s], 0, xs_hbm, 0, sem.at[s]).wait()
            return carry
        lax.fori_loop(0, count, body, 0)

    @pl.when(i > 0)
    def _():
        drain(tabp_ref[0, 0, LANES - 1], 1 - slot)

    @pl.when(i == nb - 1)
    def _():
        drain(ng, slot)

    @pl.when(i == 0)
    def _():
        zbuf[...] = jnp.zeros_like(zbuf)
        n_used = tail_ref[0, 0, 2 * N_EXPERTS]
        n_blocks = xs_hbm.shape[0] // MOE_BLOCK

        def block_copy(b):
            return pltpu.make_async_copy(
                zbuf, xs_hbm.at[pl.ds(pl.multiple_of(b * MOE_BLOCK, MOE_BLOCK), MOE_BLOCK)], bsem)

        for e in range(N_EXPERTS):
            def fill(m, carry, e=e):
                _granule_copy(zbuf, 0, xs_hbm, (tail_ref[0, 0, e] + m) * GRANULE, zsem).start()
                return carry
            lax.fori_loop(0, tail_ref[0, 0, N_EXPERTS + e], fill, 0)

        def fill_block(b, carry):
            block_copy(b).start()
            return carry
        lax.fori_loop(n_used, n_blocks, fill_block, 0)

        for e in range(N_EXPERTS):
            def done(m, carry):
                _granule_copy(zbuf, 0, xs_hbm, 0, zsem).wait()
                return carry
            lax.fori_loop(0, tail_ref[0, 0, N_EXPERTS + e], done, 0)

        def done_block(b, carry):
            block_copy(b).wait()
            return carry
        lax.fori_loop(n_used, n_blocks, done_block, 0)


def dispatch(h2, slots, table, tails, n_rows):
    n, d = h2.shape
    nt = n // TOK_TILE
    return pl.pallas_call(
        _dispatch_kernel,
        grid=(nt,),
        in_specs=[pl.BlockSpec((1, 1, LANES), lambda i: (i, 0, 0), memory_space=pltpu.SMEM),
                  pl.BlockSpec((1, 1, LANES), lambda i: (jnp.maximum(i - 1, 0), 0, 0), memory_space=pltpu.SMEM),
                  pl.BlockSpec((1, 1, LANES), lambda i: (0, 0, 0), memory_space=pltpu.SMEM),
                  pl.BlockSpec((TOK_TILE, d), lambda i: (i, 0)),
                  pl.BlockSpec((TOK_TILE, LANES), lambda i: (i, 0))],
        out_specs=pl.BlockSpec(memory_space=pl.ANY),
        out_shape=jax.ShapeDtypeStruct((n_rows, d), F32),
        scratch_shapes=[pltpu.VMEM((2, SORT_ROWS, d), F32),
                        pltpu.VMEM((MOE_BLOCK, d), F32),
                        pltpu.SemaphoreType.DMA((2,)),
                        pltpu.SemaphoreType.DMA(()),
                        pltpu.SemaphoreType.DMA(())],
        compiler_params=_cparams(("arbitrary",)),
        name="dispatch",
    )(table, table, tails, h2, slots)


def _expert_kernel(be_ref, nu_ref, x_ref, w1_ref, w3_ref, w2_ref, y_ref, w1b, w3b, w2b):
    i = pl.program_id(0)
    n_used = nu_ref[0]

    @pl.when(i < n_used)
    def _():
        first = jnp.logical_or(i == 0, be_ref[i] != be_ref[jnp.maximum(i - 1, 0)])

        @pl.when(first)
        def _():
            w1b[...] = w1_ref[0].astype(BF16)
            w3b[...] = w3_ref[0].astype(BF16)
            w2b[...] = w2_ref[0].astype(BF16)

        x = x_ref[...].astype(BF16)
        hid = _silu(_dot(x, w1b[...])) * _dot(x, w3b[...])
        y_ref[...] = _dot(hid.astype(BF16), w2b[...])

    @pl.when(i >= n_used)
    def _():
        y_ref[...] = jnp.zeros_like(y_ref)


def experts(xs, block_expert, n_used, w1, w3, w2, layer):
    n_rows, d = xs.shape
    n_blocks = n_rows // MOE_BLOCK
    ff = w1.shape[-1]
    grid_spec = pltpu.PrefetchScalarGridSpec(
        num_scalar_prefetch=2,
        grid=(n_blocks,),
        in_specs=[pl.BlockSpec((MOE_BLOCK, d), lambda i, be, nu: (jnp.minimum(i, nu[0] - 1), 0)),
                  pl.BlockSpec((None, 1, d, ff), lambda i, be, nu: (layer, be[i], 0, 0)),
                  pl.BlockSpec((None, 1, d, ff), lambda i, be, nu: (layer, be[i], 0, 0)),
                  pl.BlockSpec((None, 1, ff, d), lambda i, be, nu: (layer, be[i], 0, 0))],
        out_specs=pl.BlockSpec((MOE_BLOCK, d), lambda i, be, nu: (i, 0)),
        scratch_shapes=[pltpu.VMEM((d, ff), BF16),
                        pltpu.VMEM((d, ff), BF16),
                        pltpu.VMEM((ff, d), BF16)],
    )
    return pl.pallas_call(
        _expert_kernel,
        grid_spec=grid_spec,
        out_shape=jax.ShapeDtypeStruct((n_rows, d), F32),
        compiler_params=_cparams(("arbitrary",)),
        name="experts",
    )(block_expert, n_used, xs, w1, w3, w2)


def _combine_kernel(d_ref, dn_ref, x_ref, mod_ref, g_ref, slot_ref, y_hbm, o_ref, ybuf, sem, *, final):
    i = pl.program_id(0)
    nb = pl.num_programs(0)
    slot = i % 2
    n_gran = ybuf.shape[1] // GRANULE

    def fetch(tab, s):
        def body(j, carry):
            _granule_copy(y_hbm, tab[0, 0, j] * GRANULE, ybuf.at[s], j * GRANULE, sem.at[s]).start()
            return carry
        lax.fori_loop(0, n_gran, body, 0)

    @pl.when(i == 0)
    def _():
        fetch(d_ref, 0)

    @pl.when(i + 1 < nb)
    def _():
        fetch(dn_ref, 1 - slot)

    def drain(j, carry):
        _granule_copy(y_hbm, 0, ybuf.at[slot], 0, sem.at[slot]).wait()
        return carry
    lax.fori_loop(0, n_gran, drain, 0)

    hi, lo = _split(_slot_matrix(slot_ref, ybuf.shape[1], weighted=True))
    yb = ybuf[slot].astype(BF16)
    y = _dot(hi, yb) + _dot(lo, yb)
    x = x_ref[...] + mod_ref[0, 5:6, :] * y
    if final:
        x = x * lax.rsqrt(jnp.mean(x * x, axis=-1, keepdims=True) + EPS) * g_ref[...]
    o_ref[...] = x


def combine(x, modv, ys, table, slots, final_g, n_batch, n_ctx_tiles, final):
    n, d = x.shape
    nt = n // TOK_TILE
    tpb = nt // n_batch
    if final:
        lat = tpb - n_ctx_tiles
        steps = n_batch * lat
        tile = lambda s: (s // lat) * tpb + n_ctx_tiles + s % lat
        out_rows = steps * TOK_TILE
    else:
        steps = nt
        tile = lambda s: s
        out_rows = n

    def mod_idx(s):
        t = tile(s)
        return (t // tpb) * 2 + ((t % tpb) >= n_ctx_tiles).astype(jnp.int32)

    return pl.pallas_call(
        functools.partial(_combine_kernel, final=final),
        grid=(steps,),
        in_specs=[pl.BlockSpec((1, 1, LANES), lambda s: (tile(s), 0, 0), memory_space=pltpu.SMEM),
                  pl.BlockSpec((1, 1, LANES), lambda s: (tile(jnp.minimum(s + 1, steps - 1)), 0, 0),
                               memory_space=pltpu.SMEM),
                  pl.BlockSpec((TOK_TILE, d), lambda s: (tile(s), 0)),
                  pl.BlockSpec((1, N_MOD, d), lambda s: (mod_idx(s), 0, 0)),
                  pl.BlockSpec((1, d), lambda s: (0, 0)),
                  pl.BlockSpec((TOK_TILE, LANES), lambda s: (tile(s), 0)),
                  pl.BlockSpec(memory_space=pl.ANY)],
        out_specs=pl.BlockSpec((TOK_TILE, d), lambda s: (s, 0)),
        out_shape=jax.ShapeDtypeStruct((out_rows, d), F32),
        scratch_shapes=[pltpu.VMEM((2, SORT_ROWS, d), F32),
                        pltpu.SemaphoreType.DMA((2,))],
        compiler_params=_cparams(("arbitrary",)),
        name="combine_final" if final else "combine",
    )(table, table, x, modv, final_g, slots, ys)


def sorted_rows_bound(n_tok):
    nt = n_tok // TOK_TILE
    rows = n_tok * TOP_K + nt * N_EXPERTS * (GRANULE - 1) + N_EXPERTS * (MOE_BLOCK - GRANULE)
    return -(-rows // MOE_BLOCK) * MOE_BLOCK


def moe_tables(gran, n_rows):
    nt = gran.shape[0]
    per_blk = MOE_BLOCK // GRANULE
    local = jnp.cumsum(gran, axis=1) - gran
    n_gran = jnp.sum(gran, axis=1)
    before = jnp.cumsum(gran, axis=0) - gran
    total = jnp.sum(gran, axis=0)
    padded = (total + per_blk - 1) // per_blk * per_blk
    region_end = jnp.cumsum(padded)
    region = region_end - padded
    j = jnp.arange(LANES, dtype=jnp.int32)[None, None, :]
    inside = jnp.logical_and(j >= local[:, :, None], j < (local + gran)[:, :, None])
    dst = region[None, :, None] + before[:, :, None] + j - local[:, :, None]
    table = jnp.sum(jnp.where(inside, dst, 0), axis=1)
    valid = j[0] < n_gran[:, None]
    table = jnp.where(valid, table, table[:, :1])
    table = table.at[:, LANES - 1].set(n_gran).astype(jnp.int32).reshape(nt, 1, LANES)
    tails = jnp.zeros((LANES,), jnp.int32).at[:N_EXPERTS].set(region + total)
    tails = tails.at[N_EXPERTS:2 * N_EXPERTS].set(padded - total)
    n_blocks = n_rows // MOE_BLOCK
    blk = jnp.arange(n_blocks, dtype=jnp.int32) * per_blk
    block_expert = jnp.minimum(jnp.sum(blk[:, None] >= region_end[None, :], axis=1), N_EXPERTS - 1)
    n_used = (region_end[-1] // per_blk).astype(jnp.int32).reshape(1)
    tails = tails.at[2 * N_EXPERTS].set(n_used[0]).reshape(1, 1, LANES)
    return table, tails, block_expert.astype(jnp.int32), n_used


def _rope_tables(ctx_len, n_lat):
    rows = n_lat // GRID_W
    row = np.repeat(np.arange(rows, dtype=np.float32), GRID_W)
    col = np.tile(np.arange(GRID_W, dtype=np.float32), rows)
    half = ATT_HD // 4
    inv = jnp.asarray(ROPE_BASE, F32) ** (-jnp.arange(half, dtype=F32) / half)
    ang_r = jnp.asarray(row)[:, None] * inv
    ang_c = jnp.asarray(col)[:, None] * inv
    cos64 = jnp.concatenate([jnp.cos(ang_r)] * 2 + [jnp.cos(ang_c)] * 2, axis=1)
    sin64 = jnp.concatenate([-jnp.sin(ang_r), jnp.sin(ang_r), -jnp.sin(ang_c), jnp.sin(ang_c)], axis=1)
    cos64 = jnp.concatenate([jnp.ones((ctx_len, ATT_HD), F32), cos64], axis=0)
    sin64 = jnp.concatenate([jnp.zeros((ctx_len, ATT_HD), F32), sin64], axis=0)
    return jnp.tile(cos64, (1, ATT_HEADS)), jnp.tile(sin64, (1, ATT_HEADS))


def _pack_w_in(w):
    gla = w[:, 1792:2592]
    gla = jnp.concatenate([gla[:, 0:512], gla[:, 544:800], gla[:, 512:544],
                           jnp.zeros((w.shape[0], 96), w.dtype)], axis=1)
    return jnp.concatenate([w[:, :1792], gla, w[:, 2592:]], axis=1).astype(BF16)


def kernel(x, c, ctx, c_ctx, hg_lb_logits, router_w, router_b, final_g, w_mod, b_mod, g_mix, g_ffn,
           w_in, conv_w, conv_b, conv_ln_g, conv_ln_b, hg_norm_g, gla_w2, gla_b2, gla_norm_g, att_sink,
           w_br_conv, w_br_hg, w_br_gla, w_br_att, w_out, moe_w1, moe_w3, moe_w2):
    n_batch, n_lat, d = x.shape
    ctx_len = ctx.shape[1]
    depth = w_in.shape[0]
    assert ctx_len % TOK_TILE == 0 and n_lat % TOK_TILE == 0 and n_lat % GRID_W == 0
    seq = ctx_len + n_lat
    n_ctx_tiles = ctx_len // TOK_TILE

    xs = jnp.concatenate([ctx, x], axis=1).reshape(n_batch * seq, d)

    c_rows = jnp.zeros((8, d), F32).at[:n_batch].set(c).at[n_batch].set(c_ctx)
    mods = modulation(c_rows, w_mod, b_mod).reshape(depth, 8, N_MOD, d)
    modv = jnp.stack([mods[:, n_batch] if j % 2 == 0 else mods[:, j // 2] for j in range(2 * n_batch)], axis=1)

    lb_sm = jax.nn.softmax(hg_lb_logits.astype(F32), axis=0)
    lower = jnp.cumsum(lb_sm, axis=0) - lb_sm[0]
    cos_t, sin_t = _rope_tables(ctx_len, n_lat)
    ones_bd = jnp.asarray(np.arange(256)[:, None] // 64 == np.arange(256)[None, :] // 64, BF16)
    rw = jnp.zeros((d, LANES), F32).at[:, :N_EXPERTS].set(router_w.astype(F32))
    rb = jnp.zeros((1, LANES), F32).at[0, :N_EXPERTS].set(router_b.astype(F32))
    tri = jnp.asarray(np.tril(np.ones((TOK_TILE, TOK_TILE))), BF16)
    upper = jnp.asarray(np.triu(np.ones((LANES, LANES)), k=1), BF16)

    for l in range(depth):
        w2p = jnp.zeros((LANES, 2 * GLA_K), F32)
        w2p = w2p.at[0:GLA_RANK, 0:GLA_K].set(gla_w2[l, 0]).at[GLA_RANK:2 * GLA_RANK, GLA_K:].set(gla_w2[l, 1])
        b2p = gla_b2[l].reshape(1, 2 * GLA_K)
        conv_z, hg_pack, gla_pack, att_pack, bgate = inproj(
            xs, modv[l], g_mix[l].reshape(1, d), _pack_w_in(w_in[l]), lower[l], w2p, b2p, cos_t, sin_t, n_batch, n_ctx_tiles)

        hg_f, hg_b = gated_scan(hg_pack, (0, 1, 3, 4), (0, 2, 3, 5), HG_W, n_batch, n_ctx_tiles, "scan_hgrn")
        gla_f, gla_b = gated_scan(gla_pack, (0, 1, 2, 2), (0, 1, 2, 3), GLA_K, n_batch, n_ctx_tiles, "scan_gla")
        att_o = attention(att_pack, att_sink[l].astype(F32), n_batch, ctx_len)

        lw = dict(conv_w=jnp.zeros((32, CONV_CH), F32).at[:CONV_K].set(conv_w[l]),
                  conv_b=conv_b[l].reshape(1, -1), conv_ln_g=conv_ln_g[l].reshape(1, -1),
                  conv_ln_b=conv_ln_b[l].reshape(1, -1), hg_norm_g=hg_norm_g[l].reshape(1, -1),
                  gla_norm_g=gla_norm_g[l].reshape(1, -1), ones_bd=ones_bd,
                  w_br_conv=w_br_conv[l].astype(BF16), w_br_hg=w_br_hg[l].astype(BF16),
                  w_br_gla=w_br_gla[l].astype(BF16), w_br_att=w_br_att[l].astype(BF16),
                  w_out=w_out[l].astype(BF16), g_ffn=g_ffn[l].reshape(1, d), router_w=rw, router_b=rb,
                  tri=tri, upper=upper)
        x_new, h2, slots, gran = merge(xs, modv[l], conv_z, hg_pack, hg_f, hg_b, gla_pack, gla_f, gla_b,
                                       att_o, bgate, lw, n_batch, n_ctx_tiles)

        n_rows = sorted_rows_bound(n_batch * seq)
        table, tails, block_expert, n_used = moe_tables(gran[:, 0, :N_EXPERTS], n_rows)
        x_sorted = dispatch(h2, slots, table, tails, n_rows)
        ys = experts(x_sorted, block_expert, n_used, moe_w1, moe_w3, moe_w2, l)
        final = l == depth - 1
        xs = combine(x_new, modv[l], ys, table, slots, final_g.reshape(1, d), n_batch, n_ctx_tiles, final)

    return xs.reshape(n_batch, n_lat, d)
```

```python
import functools
import itertools

import numpy as np
import jax
import jax.numpy as jnp
from jax import lax
from jax.experimental import pallas as pl
from jax.experimental.pallas import tpu as pltpu

F32 = jnp.float32
BF16 = jnp.bfloat16

EPS = 1e-6
NEG = -1e30
TINY = 1e-30
N_MOD = 6
CONV_CH = 256
CONV_K = 31
HG_HEADS = 4
HG_W = 256
GLA_HEADS = 4
GLA_K = 128
GLA_V = 256
GLA_RANK = 16
GLA_TAU = 16.0
ATT_HEADS = 4
ATT_KV_HEADS = 2
ATT_HD = 64
ATT_BLOCK = 128
GRID_W = 64
ROPE_BASE = 10000.0
N_EXPERTS = 16
N_GROUPS = 4
TOP_K = 2
MOE_BLOCK = 256

LANES = 128
TOK_TILE = 256
SCAN_BLOCK = 16
HALO = 16
GRANULE = 8
SORT_ROWS = -(-(TOP_K * TOK_TILE + N_EXPERTS * (GRANULE - 1)) // LANES) * LANES
VMEM_LIMIT = 56 * 1024 * 1024

W_CONV = (0, 512)
W_HG = (512, 1792)
W_GLA_QKV = (1792, 2304)
W_TAIL = 2304
W_IN_COLS = 7200
W_IN_PADDED = 7296
W_GLA_RANK = (W_TAIL, W_TAIL + LANES)
W_SKEW = 2 * GLA_RANK
T_GLA_GATE = (0, 256)
T_ATT = (256, 768)
T_BG = (768, 4864)


def _cparams(sem):
    return pltpu.CompilerParams(dimension_semantics=sem, vmem_limit_bytes=VMEM_LIMIT)


def _dot(a, b):
    return jnp.dot(a, b, preferred_element_type=F32)


def _split(a):
    hi = a.astype(BF16)
    lo = (a - hi.astype(F32)).astype(BF16)
    return hi, lo


def _dot_f32(a, b):
    ah, al = _split(a)
    bh, bl = _split(b)
    return _dot(ah, bh) + _dot(ah, bl) + _dot(al, bh)


def _sigmoid(x):
    return 1.0 / (1.0 + jnp.exp(-x))


def _silu(x):
    return x * _sigmoid(x)


def _mod_kernel(c_ref, w_ref, b_ref, o_ref):
    c = c_ref[...]
    o_ref[0] = _dot_f32(_silu(c), w_ref[0]) + b_ref[0]


def modulation(c_rows, w_mod, b_mod):
    depth, d, six_d = w_mod.shape
    nblk = six_d // d
    return pl.pallas_call(
        _mod_kernel,
        grid=(depth, nblk),
        in_specs=[pl.BlockSpec((8, d), lambda l, j: (0, 0)),
                  pl.BlockSpec((1, d, d), lambda l, j: (l, 0, j)),
                  pl.BlockSpec((1, 1, d), lambda l, j: (l, 0, j))],
        out_specs=pl.BlockSpec((1, 8, d), lambda l, j: (l, 0, j)),
        out_shape=jax.ShapeDtypeStruct((depth, 8, six_d), F32),
        compiler_params=_cparams(("parallel", "parallel")),
        name="modulation",
    )(c_rows, w_mod, b_mod.reshape(depth, 1, six_d))


def _modnorm(x, g, shift, scale):
    y = x * lax.rsqrt(jnp.mean(x * x, axis=-1, keepdims=True) + EPS)
    return (y * g) * (1.0 + scale) + shift


def _inproj_kernel(x_ref, mod_ref, g_ref, w_ref, lb_ref, w2_ref, b2_ref, cos_ref, sin_ref,
                   conv_ref, hg_ref, gla_ref, att_ref, bg_ref, tail_ref):
    @pl.when(pl.program_id(0) == 0)
    def _():
        width = tail_ref.shape[1]
        tail_ref[...] = pltpu.roll(w_ref[:, W_TAIL:W_TAIL + width], width - W_SKEW, axis=1)

    x = x_ref[...]
    h = _modnorm(x, g_ref[...], mod_ref[0, 0:1, :], mod_ref[0, 1:2, :]).astype(BF16)

    p = _dot(h, w_ref[:, W_CONV[0]:W_CONV[1]])
    conv_ref[...] = p[:, :CONV_CH] * _sigmoid(p[:, CONV_CH:])

    p = _dot(h, w_ref[:, W_HG[0]:W_HG[1]])
    hg_ref[:, 0:256] = p[:, 0:256]
    hg_ref[:, 768:1024] = p[:, 768:1024]
    hg_ref[:, 1536:1792] = _silu(p[:, 1024:1280])
    for d in range(2):
        z = p[:, 256 * (d + 1):256 * (d + 2)]
        lb = lb_ref[d:d + 1, :]
        hg_ref[:, 256 * (d + 1):256 * (d + 2)] = (1.0 - lb) * _sigmoid(-z)
        hg_ref[:, 256 * (d + 4):256 * (d + 5)] = jnp.maximum(lb + (1.0 - lb) * _sigmoid(z), TINY)

    p = _dot(h, w_ref[:, W_GLA_QKV[0]:W_GLA_QKV[1]])
    gla_ref[:, 0:128] = p[:, 0:128] * (float(GLA_K // GLA_HEADS) ** -0.5)
    gla_ref[:, 128:256] = p[:, 128:256]
    gla_ref[:, 512:768] = p[:, 256:512]
    gla_ref[:, 768:1024] = _silu(_dot(h, tail_ref[:, T_GLA_GATE[0]:T_GLA_GATE[1]]))
    rank = _dot(h, w_ref[:, W_GLA_RANK[0]:W_GLA_RANK[1]])
    u = _dot_f32(rank, w2_ref[...]) + b2_ref[...]
    log_sig = jnp.minimum(u, 0.0) - jnp.log(1.0 + jnp.exp(-jnp.abs(u)))
    gla_ref[:, 256:512] = jnp.exp(log_sig * (1.0 / GLA_TAU))

    p = _dot(h, tail_ref[:, T_ATT[0]:T_ATT[1]])
    cos = cos_ref[...]
    sin = sin_ref[...]
    def rope(v, width):
        lane = lax.broadcasted_iota(jnp.int32, v.shape, 1)
        partner = jnp.where((lane & 31) < 16,
                            pltpu.roll(v, width - 16, axis=1), pltpu.roll(v, 16, axis=1))
        return v * cos[:, :width] + partner * sin[:, :width]

    att_ref[:, 0:256] = rope(p[:, 0:256], 256) * (float(ATT_HD) ** -0.5)
    att_ref[:, 256:384] = rope(p[:, 256:384], 128)
    att_ref[:, 384:512] = p[:, 384:512]

    p = _dot(h, tail_ref[:, T_BG[0]:T_BG[1]])
    bg_ref[...] = _sigmoid(p).astype(BF16)


def inproj(x, modv, g_mix, w_in_p, lb, w2p, b2p, cos_t, sin_t, n_batch, n_ctx_tiles):
    n, d = x.shape
    nt = n // TOK_TILE
    tiles_per_batch = nt // n_batch

    def mod_idx(i):
        b = i // tiles_per_batch
        return b * 2 + ((i % tiles_per_batch) >= n_ctx_tiles).astype(jnp.int32)

    const = lambda i: (0, 0)
    row = lambda i: (i, 0)
    seq = lambda i: (i % tiles_per_batch, 0)
    outs = pl.pallas_call(
        _inproj_kernel,
        grid=(nt,),
        in_specs=[pl.BlockSpec((TOK_TILE, d), row),
                  pl.BlockSpec((1, N_MOD, d), lambda i: (mod_idx(i), 0, 0)),
                  pl.BlockSpec((1, d), const),
                  pl.BlockSpec((d, W_IN_PADDED), const, pipeline_mode=pl.Buffered(1)),
                  pl.BlockSpec((2, HG_W), const),
                  pl.BlockSpec((LANES, 2 * GLA_K), const),
                  pl.BlockSpec((1, 2 * GLA_K), const),
                  pl.BlockSpec((TOK_TILE, 256), seq),
                  pl.BlockSpec((TOK_TILE, 256), seq)],
        out_specs=[pl.BlockSpec((TOK_TILE, 256), row),
                   pl.BlockSpec((TOK_TILE, 1792), row),
                   pl.BlockSpec((TOK_TILE, 1024), row),
                   pl.BlockSpec((TOK_TILE, 512), row),
                   pl.BlockSpec((TOK_TILE, 4096), row)],
        out_shape=[jax.ShapeDtypeStruct((n, 256), F32),
                   jax.ShapeDtypeStruct((n, 1792), F32),
                   jax.ShapeDtypeStruct((n, 1024), F32),
                   jax.ShapeDtypeStruct((n, 512), F32),
                   jax.ShapeDtypeStruct((n, 4096), BF16)],
        scratch_shapes=[pltpu.VMEM((d, W_IN_PADDED - W_TAIL), BF16)],
        compiler_params=_cparams(("arbitrary",)),
        name="inproj",
    )(x, modv, g_mix, w_in_p, lb, w2p, b2p, cos_t, sin_t)
    return outs


SCAN_IN = 4
SCAN_SCRATCH = 12


ATTN_IN = 11
ATTN_STRIDE = 3


def _trace_alternately(chains):
    live = list(chains)
    rnd = 0
    while live:
        for entry in list(live):
            gen, stride = entry
            if rnd % stride == 0:
                try:
                    next(gen)
                except StopIteration:
                    live.remove(entry)
        rnd += 1


def _scan_kernel(*refs, attn):
    if attn is not None:
        sink_ref, refs = refs[0], refs[1:]
    n_chain = 2
    n_in = n_chain * SCAN_IN
    ins = refs[:n_in]
    sel = refs[n_in]
    perms = refs[n_in + 1:n_in + 5]
    pos = n_in + 5
    if attn is not None:
        att_in = refs[pos:pos + ATTN_IN]
        pos += ATTN_IN
    outs = refs[pos:pos + n_chain]
    pos += n_chain
    if attn is not None:
        att_out = refs[pos]
        pos += 1
    scratch = refs[pos:]
    chains = []
    for i in range(n_chain):
        rev = i % 2 == 1
        chains.append((_scan_direction(*ins[i * SCAN_IN:(i + 1) * SCAN_IN], sel,
                                       perms[2 * rev], perms[2 * rev + 1], outs[i],
                                       *scratch[i * SCAN_SCRATCH:(i + 1) * SCAN_SCRATCH], rev=rev), 1))
    if attn is not None:
        ctx_blocks, blocks_per_batch = attn
        q_ref, kc_ref, vc_ref = att_in[:3]
        kb, vb = att_in[3:7], att_in[7:11]
        per_tile = q_ref.shape[0] // ATT_BLOCK
        for a in range(per_tile):
            rows = slice(a * ATT_BLOCK, (a + 1) * ATT_BLOCK)

            def write(hk, x, rows=rows):
                att_out[rows, hk * LANES:(hk + 1) * LANES] = x

            gens = _attn_chains(pl.program_id(1) * per_tile + a, sink_ref, q_ref[rows, :],
                                [kc_ref[...]] + [kb[a + t][...] for t in range(3)],
                                [vc_ref[...]] + [vb[a + t][...] for t in range(3)],
                                write, ctx_blocks, blocks_per_batch)
            chains += [(g, ATTN_STRIDE) for g in gens]
    _trace_alternately(chains)


def _scan_direction(q_ref, k_ref, v_ref, f_ref, sel_ref, perm_ref, permt_ref, o_ref,
                    s_ref, qs_ref, ks_ref, vs_ref, fs_ref, qt_ref, kh_ref, term_ref, w_ref, ah_ref, al_ref,
                    kv_ref, *, rev):
    c = SCAN_BLOCK
    tt, kl = q_ref.shape
    nb = tt // c
    assert nb == c

    @pl.when(pl.program_id(1) == 0)
    def _():
        s_ref[...] = jnp.zeros_like(s_ref)

    perm = perm_ref[...]
    vb = v_ref[...].astype(BF16)
    f_hi, f_lo = _split(f_ref[...])
    qs_ref[...] = _dot(perm, q_ref[...].astype(BF16))
    ks_ref[...] = _dot(perm, k_ref[...].astype(BF16))
    vs_ref[...] = _dot(perm, vb)
    fs_ref[...] = _dot(perm, f_hi) + _dot(perm, f_lo)
    yield

    def slab(ref, s):
        return ref[s * nb:(s + 1) * nb, :]

    a = slab(fs_ref, 0)
    qt_ref[0:nb, :] = (slab(qs_ref, 0) * a).astype(BF16)
    for s in range(1, c):
        a = a * slab(fs_ref, s)
        qt_ref[s * nb:(s + 1) * nb, :] = (slab(qs_ref, s) * a).astype(BF16)
    gam_t = a.T
    g = jnp.ones((nb, kl), F32)
    for s in range(c - 1, -1, -1):
        kh_ref[s * nb:(s + 1) * nb, :] = (slab(ks_ref, s) * g).astype(BF16)
        g = g * slab(fs_ref, s)
    yield

    heads = 256 // 64
    dk = kl // heads
    perm_t = permt_ref[...]
    qt = _dot(perm_t, qt_ref[...]).astype(BF16)
    kh = _dot(perm_t, kh_ref[...]).astype(BF16)

    def head_lanes(x, h):
        return x[:, :dk] if h == 0 else pltpu.roll(x, kl - dk * h, axis=1)[:, :dk]

    q_heads = [head_lanes(qt, h) for h in range(heads)]
    k_heads = [head_lanes(kh, h) for h in range(heads)]
    v_head = lax.broadcasted_iota(jnp.int32, (c, 256), 1) // 64
    g_head = lax.broadcasted_iota(jnp.int32, (dk, 256), 1) // 64
    for j in range(nb):
        lo = j * c
        k4 = jnp.concatenate([k_heads[h][lo:lo + c, :] for h in range(heads)], axis=0)
        v4 = jnp.concatenate([jnp.where(v_head == h, vb[lo:lo + c, :], jnp.zeros_like(vb[lo:lo + c, :]))
                              for h in range(heads)], axis=0)
        kv_ref[j] = lax.dot_general(k4, v4, (((0,), (0,)), ((), ())), preferred_element_type=F32)
        yield

    n_pair = 0
    for s in range(c):
        p = slab(qs_ref, s)
        for d in range(s + 1):
            if d > 0:
                p = p * slab(fs_ref, s - d + 1)
            term_ref[n_pair * nb:(n_pair + 1) * nb, :] = (p * slab(ks_ref, s - d)).astype(BF16)
            n_pair += 1
        yield
    w_ref[...] = _dot(term_ref[...], sel_ref[...])
    yield
    n_pair = 0
    for s in range(c):
        acc = jnp.zeros((nb, 256), F32)
        for d in range(s + 1):
            acc = acc + w_ref[n_pair * nb:(n_pair + 1) * nb, :] * slab(vs_ref, s - d)
            n_pair += 1
        hi, lo = _split(acc)
        ah_ref[s * nb:(s + 1) * nb, :] = hi
        al_ref[s * nb:(s + 1) * nb, :] = lo
        yield

    o_ref[...] = _dot(perm_t, ah_ref[...]) + _dot(perm_t, al_ref[...])
    yield
    st = s_ref[...]
    for jj in range(nb):
        j = nb - 1 - jj if rev else jj
        lo = j * c
        q4 = jnp.concatenate([q_heads[h][lo:lo + c, :] for h in range(heads)], axis=0)
        o4 = _dot(q4, st.astype(BF16))
        inter = o4[0:c, :]
        for h in range(1, heads):
            inter = jnp.where(v_head == h, o4[h * c:(h + 1) * c, :], inter)
        o_ref[lo:lo + c, :] += inter
        gam = jnp.broadcast_to(gam_t[0:dk, j:j + 1], (dk, 256))
        for h in range(1, heads):
            gam = jnp.where(g_head == h, gam_t[h * dk:(h + 1) * dk, j:j + 1], gam)
        st = gam * st + kv_ref[j]
        yield
    s_ref[...] = st


def gated_scan(pack, cols_fwd, cols_rev, kl, n_batch, n_ctx_tiles, name, attn=None):
    n = pack.shape[0]
    nt = n // TOK_TILE
    tpb = nt // n_batch

    def rev_tile(i):
        return jnp.where(i < n_ctx_tiles, n_ctx_tiles - 1 - i, tpb - 1 - (i - n_ctx_tiles))

    heads = 4
    c = SCAN_BLOCK
    nb = TOK_TILE // c
    perms = []
    for rev in (False, True):
        perm = np.zeros((TOK_TILE, TOK_TILE), np.float32)
        for j in range(nb):
            for s in range(c):
                perm[s * nb + j, j * c + (c - 1 - s if rev else s)] = 1.0
        perms += [jnp.asarray(perm, BF16), jnp.asarray(perm.T, BF16)]
    n_pairs = c * (c + 1) // 2
    fwd = lambda col: (lambda b, i, *_: (b * tpb + i, col))
    bwd = lambda col: (lambda b, i, *_: (b * tpb + rev_tile(i), col))
    const = lambda b, i, *_: (0, 0)

    def direction_specs(rows, cols, kl):
        qc, kc, vc, fc = cols
        specs = [pl.BlockSpec((TOK_TILE, kl), rows(qc)),
                 pl.BlockSpec((TOK_TILE, kl), rows(kc)),
                 pl.BlockSpec((TOK_TILE, 256), rows(vc)),
                 pl.BlockSpec((TOK_TILE, kl), rows(fc))]
        assert len(specs) == SCAN_IN
        return specs

    def chain_scratch(kl):
        dk = kl // heads
        shapes = [pltpu.VMEM((dk, 256), F32),
                  pltpu.VMEM((TOK_TILE, kl), F32),
                  pltpu.VMEM((TOK_TILE, kl), F32),
                  pltpu.VMEM((TOK_TILE, 256), F32),
                  pltpu.VMEM((TOK_TILE, kl), F32),
                  pltpu.VMEM((TOK_TILE, kl), BF16),
                  pltpu.VMEM((TOK_TILE, kl), BF16),
                  pltpu.VMEM((n_pairs * nb, kl), BF16),
                  pltpu.VMEM((n_pairs * nb, 256), F32),
                  pltpu.VMEM((TOK_TILE, 256), BF16),
                  pltpu.VMEM((TOK_TILE, 256), BF16),
                  pltpu.VMEM((nb, dk, 256), F32)]
        assert len(shapes) == SCAN_SCRATCH
        return shapes

    dk = kl // heads
    sel = jnp.asarray(np.arange(kl)[:, None] // dk == np.arange(256)[None, :] // 64, BF16)
    in_specs = direction_specs(fwd, cols_fwd, kl) + direction_specs(bwd, cols_rev, kl)
    in_specs += [pl.BlockSpec(sel.shape, const)] + [pl.BlockSpec((TOK_TILE, TOK_TILE), const)] * 4
    operands = [pack] * (2 * SCAN_IN) + [sel] + perms
    out_specs = [pl.BlockSpec((TOK_TILE, 256), fwd(0)), pl.BlockSpec((TOK_TILE, 256), bwd(0))]
    prefetch, kernel_attn = [], None
    if attn is not None:
        att_pack, sink, ctx_len = attn
        per_tile = TOK_TILE // ATT_BLOCK
        bpb = tpb * per_tile
        cb = ctx_len // ATT_BLOCK

        def band(off, col):
            def idx(b, i, *_):
                return (b * bpb + jnp.clip(i * per_tile + off, cb, bpb - 1), col)
            return pl.BlockSpec((ATT_BLOCK, LANES), idx)

        def ctx(col):
            return pl.BlockSpec((ctx_len, LANES), lambda b, i, *_: (b * (bpb * ATT_BLOCK // ctx_len), col))

        att_specs = ([pl.BlockSpec((TOK_TILE, 256), fwd(0)), ctx(2), ctx(3)]
                     + [band(off, 2) for off in range(-1, per_tile + 1)]
                     + [band(off, 3) for off in range(-1, per_tile + 1)])
        assert len(att_specs) == ATTN_IN
        in_specs += att_specs
        operands += [att_pack] * ATTN_IN
        out_specs.append(pl.BlockSpec((TOK_TILE, 256), fwd(0)))
        prefetch, kernel_attn = [sink], (cb, bpb)
    grid_spec = pltpu.PrefetchScalarGridSpec(
        num_scalar_prefetch=len(prefetch),
        grid=(n_batch, tpb),
        in_specs=in_specs,
        out_specs=out_specs,
        scratch_shapes=chain_scratch(kl) + chain_scratch(kl),
    )
    return pl.pallas_call(
        functools.partial(_scan_kernel, attn=kernel_attn),
        grid_spec=grid_spec,
        out_shape=[jax.ShapeDtypeStruct((n, 256), F32)] * len(out_specs),
        compiler_params=_cparams(("parallel", "arbitrary")),
        name=name,
    )(*prefetch, *operands)


def _attn_chains(n, sink_ref, q, keys, vals, write, ctx_blocks, blocks_per_batch):
    blk = ATT_BLOCK
    lane = lax.broadcasted_iota(jnp.int32, (blk, LANES), 1)
    low = lane < ATT_HD
    qi = lax.broadcasted_iota(jnp.int32, (blk, blk), 0)
    ki = lax.broadcasted_iota(jnp.int32, (blk, blk), 1)
    is_lat = n >= ctx_blocks
    band_ok = [
        jnp.logical_and(jnp.logical_and(is_lat, n - 1 >= ctx_blocks), qi <= ki),
        jnp.logical_and(is_lat, qi >= 0),
        jnp.logical_and(jnp.logical_and(is_lat, n + 1 < blocks_per_batch), ki <= qi),
    ]

    def dup(x, hk):
        xr = pltpu.roll(x, ATT_HD, axis=1)
        lo_x = lax.broadcasted_iota(jnp.int32, x.shape, 1) < ATT_HD
        return (jnp.where(lo_x, x, xr) if hk == 0 else jnp.where(lo_x, xr, x)).astype(BF16)

    nt_dims = (((1,), (1,)), ((), ()))

    def kv_group(hk):
        kd = [dup(x, hk) for x in keys]
        vd = [dup(x, hk) for x in vals]
        qh = q[:, hk * LANES:(hk + 1) * LANES]
        q2 = jnp.concatenate([jnp.where(low, qh, 0.0), jnp.where(low, 0.0, qh)], axis=0).astype(BF16)
        yield
        s = [lax.dot_general(q2, kx, nt_dims, preferred_element_type=F32) for kx in kd]
        for t in range(3):
            s[t + 1] = jnp.where(jnp.concatenate([band_ok[t]] * 2, axis=0), s[t + 1], NEG)
        yield
        sink = jnp.where(lax.broadcasted_iota(jnp.int32, (2 * blk, 1), 0) < blk,
                         sink_ref[hk * 2], sink_ref[hk * 2 + 1])
        pieces = [s[0][:, :LANES], s[0][:, LANES:], s[1], s[2], s[3]]
        top = pieces[0]
        for piece in pieces[1:]:
            top = jnp.maximum(top, piece)
        m = jnp.maximum(sink, top.max(axis=-1, keepdims=True))
        yield
        e = [jnp.exp(x - m) for x in s]
        tot = e[0][:, :LANES] + e[0][:, LANES:] + e[1] + e[2] + e[3]
        den = jnp.exp(sink - m) + tot.sum(axis=-1, keepdims=True)
        yield
        o = _dot(e[0].astype(BF16), vd[0])
        for t in range(1, 4):
            o = o + _dot(e[t].astype(BF16), vd[t])
        yield
        o = o / den
        write(hk, jnp.where(low, o[:blk, :], o[blk:, :]))

    return [kv_group(hk) for hk in range(ATT_KV_HEADS)]


def _head_norm(o, ones_bd, g):
    sq = o * o
    hi, lo = _split(sq)
    ms = (_dot(hi, ones_bd) + _dot(lo, ones_bd)) * (1.0 / 64.0)
    return o * lax.rsqrt(ms + EPS) * g


def _route_select(aff, bias):
    lane = lax.broadcasted_iota(jnp.int32, aff.shape, 1)
    epg = N_EXPERTS // N_GROUPS
    pos = lane & (epg - 1)
    v = aff + bias

    def nxt(x, o):
        return pltpu.roll(x, LANES - o, axis=1)

    def prv(x, o):
        return pltpu.roll(x, o, axis=1)

    beaten = jnp.zeros(aff.shape, jnp.int32)
    for o in range(1, epg):
        beaten = beaten + jnp.where(jnp.logical_and(pos + o < epg, nxt(v, o) > v), 1, 0)
        beaten = beaten + jnp.where(jnp.logical_and(pos >= o, prv(v, o) >= v), 1, 0)
    top2 = beaten < TOP_K
    t = jnp.where(top2, v, 0.0)
    score = t
    for o in range(1, epg):
        score = score + jnp.where(pos + o < epg, nxt(t, o), 0.0) + jnp.where(pos >= o, prv(t, o), 0.0)
    worse = jnp.zeros(aff.shape, jnp.int32)
    for o in range(epg, N_EXPERTS, epg):
        worse = worse + jnp.where(jnp.logical_and(lane + o < N_EXPERTS, nxt(score, o) > score), 1, 0)
        worse = worse + jnp.where(jnp.logical_and(lane >= o, prv(score, o) >= score), 1, 0)
    sel = jnp.logical_and(jnp.logical_and(lane < N_EXPERTS, worse == 0), top2)
    picked = jnp.where(sel, aff, 0.0)
    gate = picked / jnp.sum(picked, axis=-1, keepdims=True)
    return gate, sel


def _merge_kernel(x_ref, mod_ref, zp_ref, z_ref, zn_ref, hgf_ref, hgb_ref, hgg_ref, glf_ref, glb_ref,
                  glg_ref, att_ref, bg_ref, cw_ref, cb_ref, lng_ref, lnb_ref, hgn_ref, gln_ref,
                  ones_ref, wbc_ref, wbh_ref, wbg_ref, wba_ref, wo_ref, gf_ref, rw_ref, rb_ref, tri_ref,
                  upper_ref, xo_ref, h2_ref, slot_ref, gran_ref, zs_ref, zsh_ref, *, tiles_per_batch,
                  n_ctx_tiles):
    i = pl.program_id(0)
    tt = x_ref.shape[0]
    ti = i % tiles_per_batch
    has_prev = jnp.logical_and(ti != 0, ti != n_ctx_tiles)
    has_next = jnp.logical_and(ti != n_ctx_tiles - 1, ti != tiles_per_batch - 1)

    zs_ref[0:HALO, :] = jnp.where(has_prev, zp_ref[...], 0.0)
    zs_ref[HALO:HALO + tt, :] = z_ref[...]
    zs_ref[HALO + tt:HALO + tt + HALO, :] = jnp.where(has_next, zn_ref[...], 0.0)
    span = tt + 2 * HALO - 8
    for b in range(8):
        zsh_ref[b] = zs_ref[b:b + span, :]
    acc = jnp.zeros((tt, CONV_CH), F32) + cb_ref[...]
    for j in range(CONV_K):
        off = HALO - CONV_K // 2 + j
        acc = acc + zsh_ref[off % 8, off - off % 8:off - off % 8 + tt, :] * cw_ref[j:j + 1, :]
    mu = jnp.mean(acc, axis=-1, keepdims=True)
    cen = acc - mu
    var = jnp.mean(cen * cen, axis=-1, keepdims=True)
    conv_y = _silu(cen * lax.rsqrt(var + EPS) * lng_ref[...] + lnb_ref[...])

    ones_bd = ones_ref[...]
    hg_y = _head_norm(hgf_ref[...] + hgb_ref[...], ones_bd, hgn_ref[...]) * hgg_ref[...]
    gla_y = _head_norm(glf_ref[...] + glb_ref[...], ones_bd, gln_ref[...]) * glg_ref[...]

    d = x_ref.shape[1]
    merged = bg_ref[:, 0:d].astype(F32) * _dot(conv_y.astype(BF16), wbc_ref[...])
    merged = merged + bg_ref[:, d:2 * d].astype(F32) * _dot(hg_y.astype(BF16), wbh_ref[...])
    merged = merged + bg_ref[:, 2 * d:3 * d].astype(F32) * _dot(gla_y.astype(BF16), wbg_ref[...])
    merged = merged + bg_ref[:, 3 * d:4 * d].astype(F32) * _dot(att_ref[...].astype(BF16), wba_ref[...])
    mix = _dot(merged.astype(BF16), wo_ref[...])

    x_new = x_ref[...] + mod_ref[0, 2:3, :] * mix
    xo_ref[...] = x_new
    h2 = _modnorm(x_new, gf_ref[...], mod_ref[0, 3:4, :], mod_ref[0, 4:5, :])
    h2_ref[...] = h2
    gate, sel = _route_select(_sigmoid(_dot_f32(h2, rw_ref[...])), rb_ref[...])

    incl = _dot(tri_ref[...], sel.astype(F32).astype(BF16))
    gran = jnp.floor((incl[tt - 1:tt, :] + (GRANULE - 1.0)) * (1.0 / GRANULE))
    gran8 = jnp.broadcast_to(gran, (8, LANES))
    start = _dot(gran8.astype(BF16), upper_ref[...])[0:1, :]
    pos = GRANULE * start + incl - 1.0
    p0 = jnp.min(jnp.where(sel, pos, 1e9), axis=-1, keepdims=True)
    p1 = jnp.max(jnp.where(sel, pos, -1.0), axis=-1, keepdims=True)
    g0 = jnp.sum(jnp.where(jnp.logical_and(sel, pos == p0), gate, 0.0), axis=-1, keepdims=True)
    g1 = jnp.sum(jnp.where(jnp.logical_and(sel, pos == p1), gate, 0.0), axis=-1, keepdims=True)
    lane = lax.broadcasted_iota(jnp.int32, (tt, LANES), 1)
    slot_ref[...] = jnp.where(lane == 0, p0, jnp.where(lane == 1, p1, jnp.where(lane == 2, g0,
                              jnp.where(lane == 3, g1, 0.0))))
    gran_ref[0] = gran8.astype(jnp.int32)


def merge(x, modv, conv_z, hg_pack, hg_f, hg_b, gla_pack, gla_f, gla_b, att_o, bgate, lw, n_batch,
          n_ctx_tiles):
    n, d = x.shape
    nt = n // TOK_TILE
    tpb = nt // n_batch
    hpt = TOK_TILE // HALO
    n_halo = n // HALO

    def mod_idx(i):
        return (i // tpb) * 2 + ((i % tpb) >= n_ctx_tiles).astype(jnp.int32)

    row = lambda i: (i, 0)
    const = lambda i: (0, 0)
    col = lambda c: (lambda i: (i, c))
    full = lambda a: pl.BlockSpec(a.shape, const)
    weights = [lw["conv_w"], lw["conv_b"], lw["conv_ln_g"], lw["conv_ln_b"], lw["hg_norm_g"],
               lw["gla_norm_g"], lw["ones_bd"], lw["w_br_conv"], lw["w_br_hg"], lw["w_br_gla"],
               lw["w_br_att"], lw["w_out"], lw["g_ffn"], lw["router_w"], lw["router_b"], lw["tri"],
               lw["upper"]]
    return pl.pallas_call(
        functools.partial(_merge_kernel, tiles_per_batch=tpb, n_ctx_tiles=n_ctx_tiles),
        grid=(nt,),
        in_specs=[pl.BlockSpec((TOK_TILE, d), row),
                  pl.BlockSpec((1, N_MOD, d), lambda i: (mod_idx(i), 0, 0)),
                  pl.BlockSpec((HALO, CONV_CH), lambda i: (jnp.maximum(i * hpt - 1, 0), 0)),
                  pl.BlockSpec((TOK_TILE, CONV_CH), row),
                  pl.BlockSpec((HALO, CONV_CH), lambda i: (jnp.minimum((i + 1) * hpt, n_halo - 1), 0)),
                  pl.BlockSpec((TOK_TILE, 256), row),
                  pl.BlockSpec((TOK_TILE, 256), row),
                  pl.BlockSpec((TOK_TILE, 256), col(6)),
                  pl.BlockSpec((TOK_TILE, 256), row),
                  pl.BlockSpec((TOK_TILE, 256), row),
                  pl.BlockSpec((TOK_TILE, 256), col(3)),
                  pl.BlockSpec((TOK_TILE, 256), row),
                  pl.BlockSpec((TOK_TILE, 4 * d), row)] + [full(w) for w in weights],
        out_specs=[pl.BlockSpec((TOK_TILE, d), row),
                   pl.BlockSpec((TOK_TILE, d), row),
                   pl.BlockSpec((TOK_TILE, LANES), row),
                   pl.BlockSpec((1, 8, LANES), lambda i: (i, 0, 0))],
        out_shape=[jax.ShapeDtypeStruct((n, d), F32),
                   jax.ShapeDtypeStruct((n, d), F32),
                   jax.ShapeDtypeStruct((n, LANES), F32),
                   jax.ShapeDtypeStruct((nt, 8, LANES), jnp.int32)],
        scratch_shapes=[pltpu.VMEM((TOK_TILE + 2 * HALO, CONV_CH), F32),
                        pltpu.VMEM((8, TOK_TILE + 2 * HALO - 8, CONV_CH), F32)],
        compiler_params=_cparams(("parallel",)),
        name="merge",
    )(x, modv, conv_z, conv_z, conv_z, hg_f, hg_b, hg_pack, gla_f, gla_b, gla_pack, att_o, bgate,
      *weights)


def _granule_copy(src, src_row, dst, dst_row, sem):
    return pltpu.make_async_copy(src.at[pl.ds(pl.multiple_of(src_row, GRANULE), GRANULE)],
                                 dst.at[pl.ds(pl.multiple_of(dst_row, GRANULE), GRANULE)], sem)


def _slot_matrix(slot_ref, width, weighted):
    tt = slot_ref.shape[0]
    col = lax.broadcasted_iota(jnp.int32, (tt, width), 1).astype(F32)
    hit0 = col == slot_ref[:, 0:1]
    hit1 = col == slot_ref[:, 1:2]
    if not weighted:
        return jnp.where(jnp.logical_or(hit0, hit1), 1.0, 0.0)
    return jnp.where(hit0, slot_ref[:, 2:3], 0.0) + jnp.where(hit1, slot_ref[:, 3:4], 0.0)


def _dispatch_kernel(tab_ref, tabp_ref, tail_ref, h_ref, slot_ref, xs_hbm, buf, zbuf, sem, zsem, bsem):
    i = pl.program_id(0)
    nb = pl.num_programs(0)
    slot = i % 2
    ng = tab_ref[0, 0, LANES - 1]

    perm = _slot_matrix(slot_ref, buf.shape[1], weighted=False).astype(BF16)
    buf[slot] = lax.dot_general(perm, h_ref[...].astype(BF16), (((0,), (0,)), ((), ())),
                                preferred_element_type=F32)

    def issue(j, carry):
        _granule_copy(buf.at[slot], j * GRANULE, xs_hbm, tab_ref[0, 0, j] * GRANULE, sem.at[slot]).start()
        return carry
    lax.fori_loop(0, ng, issue, 0)

    def drain(count, s):
        def body(j, carry):
            _granule_copy(buf.at[s], 0, xs_hbm, 0, sem.at[s]).wait()
            return carry
        lax.fori_loop(0, count, body, 0)

    @pl.when(i > 0)
    def _():
        drain(tabp_ref[0, 0, LANES - 1], 1 - slot)

    @pl.when(i == nb - 1)
    def _():
        drain(ng, slot)

    @pl.when(i == 0)
    def _():
        zbuf[...] = jnp.zeros_like(zbuf)
        n_used = tail_ref[0, 0, 2 * N_EXPERTS]
        n_blocks = xs_hbm.shape[0] // MOE_BLOCK

        def block_copy(b):
            return pltpu.make_async_copy(
                zbuf, xs_hbm.at[pl.ds(pl.multiple_of(b * MOE_BLOCK, MOE_BLOCK), MOE_BLOCK)], bsem)

        for e in range(N_EXPERTS):
            def fill(m, carry, e=e):
                _granule_copy(zbuf, 0, xs_hbm, (tail_ref[0, 0, e] + m) * GRANULE, zsem).start()
                return carry
            lax.fori_loop(0, tail_ref[0, 0, N_EXPERTS + e], fill, 0)

        def fill_block(b, carry):
            block_copy(b).start()
            return carry
        lax.fori_loop(n_used, n_blocks, fill_block, 0)

        for e in range(N_EXPERTS):
            def done(m, carry):
                _granule_copy(zbuf, 0, xs_hbm, 0, zsem).wait()
                return carry
            lax.fori_loop(0, tail_ref[0, 0, N_EXPERTS + e], done, 0)

        def done_block(b, carry):
            block_copy(b).wait()
            return carry
        lax.fori_loop(n_used, n_blocks, done_block, 0)


def dispatch(h2, slots, table, tails, n_rows):
    n, d = h2.shape
    nt = n // TOK_TILE
    return pl.pallas_call(
        _dispatch_kernel,
        grid=(nt,),
        in_specs=[pl.BlockSpec((1, 1, LANES), lambda i: (i, 0, 0), memory_space=pltpu.SMEM),
                  pl.BlockSpec((1, 1, LANES), lambda i: (jnp.maximum(i - 1, 0), 0, 0), memory_space=pltpu.SMEM),
                  pl.BlockSpec((1, 1, LANES), lambda i: (0, 0, 0), memory_space=pltpu.SMEM),
                  pl.BlockSpec((TOK_TILE, d), lambda i: (i, 0)),
                  pl.BlockSpec((TOK_TILE, LANES), lambda i: (i, 0))],
        out_specs=pl.BlockSpec(memory_space=pl.ANY),
        out_shape=jax.ShapeDtypeStruct((n_rows, d), F32),
        scratch_shapes=[pltpu.VMEM((2, SORT_ROWS, d), F32),
                        pltpu.VMEM((MOE_BLOCK, d), F32),
                        pltpu.SemaphoreType.DMA((2,)),
                        pltpu.SemaphoreType.DMA(()),
                        pltpu.SemaphoreType.DMA(())],
        compiler_params=_cparams(("arbitrary",)),
        name="dispatch",
    )(table, table, tails, h2, slots)


def _expert_kernel(be_ref, nu_ref, x_ref, w1_ref, w3_ref, w2_ref, y_ref, w1b, w3b, w2b):
    i = pl.program_id(0)
    n_used = nu_ref[0]

    @pl.when(i < n_used)
    def _():
        first = jnp.logical_or(i == 0, be_ref[i] != be_ref[jnp.maximum(i - 1, 0)])

        @pl.when(first)
        def _():
            w1b[...] = w1_ref[0].astype(BF16)
            w3b[...] = w3_ref[0].astype(BF16)
            w2b[...] = w2_ref[0].astype(BF16)

        x = x_ref[...].astype(BF16)
        hid = _silu(_dot(x, w1b[...])) * _dot(x, w3b[...])
        y_ref[...] = _dot(hid.astype(BF16), w2b[...])

    @pl.when(i >= n_used)
    def _():
        y_ref[...] = jnp.zeros_like(y_ref)


def experts(xs, block_expert, n_used, w1, w3, w2, layer):
    n_rows, d = xs.shape
    n_blocks = n_rows // MOE_BLOCK
    ff = w1.shape[-1]
    grid_spec = pltpu.PrefetchScalarGridSpec(
        num_scalar_prefetch=2,
        grid=(n_blocks,),
        in_specs=[pl.BlockSpec((MOE_BLOCK, d), lambda i, be, nu: (jnp.minimum(i, nu[0] - 1), 0)),
                  pl.BlockSpec((None, 1, d, ff), lambda i, be, nu: (layer, be[i], 0, 0)),
                  pl.BlockSpec((None, 1, d, ff), lambda i, be, nu: (layer, be[i], 0, 0)),
                  pl.BlockSpec((None, 1, ff, d), lambda i, be, nu: (layer, be[i], 0, 0))],
        out_specs=pl.BlockSpec((MOE_BLOCK, d), lambda i, be, nu: (i, 0)),
        scratch_shapes=[pltpu.VMEM((d, ff), BF16),
                        pltpu.VMEM((d, ff), BF16),
                        pltpu.VMEM((ff, d), BF16)],
    )
    return pl.pallas_call(
        _expert_kernel,
        grid_spec=grid_spec,
        out_shape=jax.ShapeDtypeStruct((n_rows, d), F32),
        compiler_params=_cparams(("arbitrary",)),
        name="experts",
    )(block_expert, n_used, xs, w1, w3, w2)


def _combine_kernel(d_ref, dn_ref, x_ref, mod_ref, g_ref, slot_ref, y_hbm, o_ref, ybuf, sem, *, final):
    i = pl.program_id(0)
    nb = pl.num_programs(0)
    slot = i % 2
    n_gran = ybuf.shape[1] // GRANULE

    def fetch(tab, s):
        def body(j, carry):
            _granule_copy(y_hbm, tab[0, 0, j] * GRANULE, ybuf.at[s], j * GRANULE, sem.at[s]).start()
            return carry
        lax.fori_loop(0, n_gran, body, 0)

    @pl.when(i == 0)
    def _():
        fetch(d_ref, 0)

    @pl.when(i + 1 < nb)
    def _():
        fetch(dn_ref, 1 - slot)

    def drain(j, carry):
        _granule_copy(y_hbm, 0, ybuf.at[slot], 0, sem.at[slot]).wait()
        return carry
    lax.fori_loop(0, n_gran, drain, 0)

    hi, lo = _split(_slot_matrix(slot_ref, ybuf.shape[1], weighted=True))
    yb = ybuf[slot].astype(BF16)
    y = _dot(hi, yb) + _dot(lo, yb)
    x = x_ref[...] + mod_ref[0, 5:6, :] * y
    if final:
        x = x * lax.rsqrt(jnp.mean(x * x, axis=-1, keepdims=True) + EPS) * g_ref[...]
    o_ref[...] = x


def combine(x, modv, ys, table, slots, final_g, n_batch, n_ctx_tiles, final):
    n, d = x.shape
    nt = n // TOK_TILE
    tpb = nt // n_batch
    if final:
        lat = tpb - n_ctx_tiles
        steps = n_batch * lat
        tile = lambda s: (s // lat) * tpb + n_ctx_tiles + s % lat
        out_rows = steps * TOK_TILE
    else:
        steps = nt
        tile = lambda s: s
        out_rows = n

    def mod_idx(s):
        t = tile(s)
        return (t // tpb) * 2 + ((t % tpb) >= n_ctx_tiles).astype(jnp.int32)

    return pl.pallas_call(
        functools.partial(_combine_kernel, final=final),
        grid=(steps,),
        in_specs=[pl.BlockSpec((1, 1, LANES), lambda s: (tile(s), 0, 0), memory_space=pltpu.SMEM),
                  pl.BlockSpec((1, 1, LANES), lambda s: (tile(jnp.minimum(s + 1, steps - 1)), 0, 0),
                               memory_space=pltpu.SMEM),
                  pl.BlockSpec((TOK_TILE, d), lambda s: (tile(s), 0)),
                  pl.BlockSpec((1, N_MOD, d), lambda s: (mod_idx(s), 0, 0)),
                  pl.BlockSpec((1, d), lambda s: (0, 0)),
                  pl.BlockSpec((TOK_TILE, LANES), lambda s: (tile(s), 0)),
                  pl.BlockSpec(memory_space=pl.ANY)],
        out_specs=pl.BlockSpec((TOK_TILE, d), lambda s: (s, 0)),
        out_shape=jax.ShapeDtypeStruct((out_rows, d), F32),
        scratch_shapes=[pltpu.VMEM((2, SORT_ROWS, d), F32),
                        pltpu.SemaphoreType.DMA((2,))],
        compiler_params=_cparams(("arbitrary",)),
        name="combine_final" if final else "combine",
    )(table, table, x, modv, final_g, slots, ys)


def sorted_rows_bound(n_tok):
    nt = n_tok // TOK_TILE
    rows = n_tok * TOP_K + nt * N_EXPERTS * (GRANULE - 1) + N_EXPERTS * (MOE_BLOCK - GRANULE)
    return -(-rows // MOE_BLOCK) * MOE_BLOCK


def moe_tables(gran, n_rows):
    nt = gran.shape[0]
    per_blk = MOE_BLOCK // GRANULE
    local = jnp.cumsum(gran, axis=1) - gran
    n_gran = jnp.sum(gran, axis=1)
    before = jnp.cumsum(gran, axis=0) - gran
    total = jnp.sum(gran, axis=0)
    padded = (total + per_blk - 1) // per_blk * per_blk
    region_end = jnp.cumsum(padded)
    region = region_end - padded
    j = jnp.arange(LANES, dtype=jnp.int32)[None, None, :]
    inside = jnp.logical_and(j >= local[:, :, None], j < (local + gran)[:, :, None])
    dst = region[None, :, None] + before[:, :, None] + j - local[:, :, None]
    table = jnp.sum(jnp.where(inside, dst, 0), axis=1)
    valid = j[0] < n_gran[:, None]
    table = jnp.where(valid, table, table[:, :1])
    table = table.at[:, LANES - 1].set(n_gran).astype(jnp.int32).reshape(nt, 1, LANES)
    tails = jnp.zeros((LANES,), jnp.int32).at[:N_EXPERTS].set(region + total)
    tails = tails.at[N_EXPERTS:2 * N_EXPERTS].set(padded - total)
    n_blocks = n_rows // MOE_BLOCK
    blk = jnp.arange(n_blocks, dtype=jnp.int32) * per_blk
    block_expert = jnp.minimum(jnp.sum(blk[:, None] >= region_end[None, :], axis=1), N_EXPERTS - 1)
    n_used = (region_end[-1] // per_blk).astype(jnp.int32).reshape(1)
    tails = tails.at[2 * N_EXPERTS].set(n_used[0]).reshape(1, 1, LANES)
    return table, tails, block_expert.astype(jnp.int32), n_used


def _rope_tables(ctx_len, n_lat):
    rows = n_lat // GRID_W
    row = np.repeat(np.arange(rows, dtype=np.float32), GRID_W)
    col = np.tile(np.arange(GRID_W, dtype=np.float32), rows)
    half = ATT_HD // 4
    inv = jnp.asarray(ROPE_BASE, F32) ** (-jnp.arange(half, dtype=F32) / half)
    ang_r = jnp.asarray(row)[:, None] * inv
    ang_c = jnp.asarray(col)[:, None] * inv
    cos64 = jnp.concatenate([jnp.cos(ang_r)] * 2 + [jnp.cos(ang_c)] * 2, axis=1)
    sin64 = jnp.concatenate([-jnp.sin(ang_r), jnp.sin(ang_r), -jnp.sin(ang_c), jnp.sin(ang_c)], axis=1)
    cos64 = jnp.concatenate([jnp.ones((ctx_len, ATT_HD), F32), cos64], axis=0)
    sin64 = jnp.concatenate([jnp.zeros((ctx_len, ATT_HD), F32), sin64], axis=0)
    return jnp.tile(cos64, (1, ATT_HEADS)), jnp.tile(sin64, (1, ATT_HEADS))


def _pack_w_in(w):
    assert w.shape[-1] == W_IN_COLS
    return jnp.pad(w.astype(BF16), ((0, 0), (0, W_IN_PADDED - W_IN_COLS)))


def kernel(x, c, ctx, c_ctx, hg_lb_logits, router_w, router_b, final_g, w_mod, b_mod, g_mix, g_ffn,
           w_in, conv_w, conv_b, conv_ln_g, conv_ln_b, hg_norm_g, gla_w2, gla_b2, gla_norm_g, att_sink,
           w_br_conv, w_br_hg, w_br_gla, w_br_att, w_out, moe_w1, moe_w3, moe_w2):
    n_batch, n_lat, d = x.shape
    ctx_len = ctx.shape[1]
    depth = w_in.shape[0]
    assert ctx_len % TOK_TILE == 0 and n_lat % TOK_TILE == 0 and n_lat % GRID_W == 0
    seq = ctx_len + n_lat
    n_ctx_tiles = ctx_len // TOK_TILE

    xs = jnp.concatenate([ctx, x], axis=1).reshape(n_batch * seq, d)

    c_rows = jnp.zeros((8, d), F32).at[:n_batch].set(c).at[n_batch].set(c_ctx)
    mods = modulation(c_rows, w_mod, b_mod).reshape(depth, 8, N_MOD, d)
    modv = jnp.stack([mods[:, n_batch] if j % 2 == 0 else mods[:, j // 2] for j in range(2 * n_batch)], axis=1)

    lb_sm = jax.nn.softmax(hg_lb_logits.astype(F32), axis=0)
    lower = jnp.cumsum(lb_sm, axis=0) - lb_sm[0]
    cos_t, sin_t = _rope_tables(ctx_len, n_lat)
    ones_bd = jnp.asarray(np.arange(256)[:, None] // 64 == np.arange(256)[None, :] // 64, BF16)
    rw = jnp.zeros((d, LANES), F32).at[:, :N_EXPERTS].set(router_w.astype(F32))
    rb = jnp.zeros((1, LANES), F32).at[0, :N_EXPERTS].set(router_b.astype(F32))
    tri = jnp.asarray(np.tril(np.ones((TOK_TILE, TOK_TILE))), BF16)
    upper = jnp.asarray(np.triu(np.ones((LANES, LANES)), k=1), BF16)

    for l in range(depth):
        w2p = jnp.zeros((LANES, 2 * GLA_K), F32)
        w2p = w2p.at[0:GLA_RANK, 0:GLA_K].set(gla_w2[l, 0]).at[GLA_RANK:2 * GLA_RANK, GLA_K:].set(gla_w2[l, 1])
        b2p = gla_b2[l].reshape(1, 2 * GLA_K)
        conv_z, hg_pack, gla_pack, att_pack, bgate = inproj(
            xs, modv[l], g_mix[l].reshape(1, d), _pack_w_in(w_in[l]), lower[l], w2p, b2p, cos_t, sin_t, n_batch, n_ctx_tiles)

        hg_f, hg_b = gated_scan(hg_pack, (0, 1, 3, 4), (0, 2, 3, 5), HG_W, n_batch, n_ctx_tiles, "scan_hgrn")
        gla_f, gla_b, att_o = gated_scan(gla_pack, (0, 1, 2, 2), (0, 1, 2, 3), GLA_K, n_batch, n_ctx_tiles,
                                         "scan_gla_attn", attn=(att_pack, att_sink[l].astype(F32), ctx_len))

        lw = dict(conv_w=jnp.zeros((32, CONV_CH), F32).at[:CONV_K].set(conv_w[l]),
                  conv_b=conv_b[l].reshape(1, -1), conv_ln_g=conv_ln_g[l].reshape(1, -1),
                  conv_ln_b=conv_ln_b[l].reshape(1, -1), hg_norm_g=hg_norm_g[l].reshape(1, -1),
                  gla_norm_g=gla_norm_g[l].reshape(1, -1), ones_bd=ones_bd,
                  w_br_conv=w_br_conv[l].astype(BF16), w_br_hg=w_br_hg[l].astype(BF16),
                  w_br_gla=w_br_gla[l].astype(BF16), w_br_att=w_br_att[l].astype(BF16),
                  w_out=w_out[l].astype(BF16), g_ffn=g_ffn[l].reshape(1, d), router_w=rw, router_b=rb,
                  tri=tri, upper=upper)
        x_new, h2, slots, gran = merge(xs, modv[l], conv_z, hg_pack, hg_f, hg_b, gla_pack, gla_f, gla_b,
                                       att_o, bgate, lw, n_batch, n_ctx_tiles)

        n_rows = sorted_rows_bound(n_batch * seq)
        table, tails, block_expert, n_used = moe_tables(gran[:, 0, :N_EXPERTS], n_rows)
        x_sorted = dispatch(h2, slots, table, tails, n_rows)
        ys = experts(x_sorted, block_expert, n_used, moe_w1, moe_w3, moe_w2, l)
        final = l == depth - 1
        xs = combine(x_new, modv[l], ys, table, slots, final_g.reshape(1, d), n_batch, n_ctx_tiles, final)

    return xs.reshape(n_batch, n_lat, d)
```

```python
import functools
import itertools

import numpy as np
import jax
import jax.numpy as jnp
from jax import lax
from jax.experimental import pallas as pl
from jax.experimental.pallas import tpu as pltpu

F32 = jnp.float32
BF16 = jnp.bfloat16

EPS = 1e-6
NEG = -1e30
TINY = 1e-30
N_MOD = 6
CONV_CH = 256
CONV_K = 31
HG_HEADS = 4
HG_W = 256
GLA_HEADS = 4
GLA_K = 128
GLA_V = 256
GLA_RANK = 16
GLA_TAU = 16.0
ATT_HEADS = 4
ATT_KV_HEADS = 2
ATT_HD = 64
ATT_BLOCK = 128
GRID_W = 64
ROPE_BASE = 10000.0
N_EXPERTS = 16
N_GROUPS = 4
TOP_K = 2
MOE_BLOCK = 256

LANES = 128
TOK_TILE = 256
SCAN_BLOCK = 16
HALO = 16
GRANULE = 8
SORT_ROWS = -(-(TOP_K * TOK_TILE + N_EXPERTS * (GRANULE - 1)) // LANES) * LANES
VMEM_LIMIT = 56 * 1024 * 1024

W_CONV = (0, 512)
W_HG = (512, 1792)
W_GLA_QKV = (1792, 2304)
W_TAIL = 2304
W_IN_COLS = 7200
W_IN_PADDED = 7296
W_GLA_RANK = (W_TAIL, W_TAIL + LANES)
W_SKEW = 2 * GLA_RANK
T_GLA_GATE = (0, 256)
T_ATT = (256, 768)
T_BG = (768, 4864)


def _cparams(sem):
    return pltpu.CompilerParams(dimension_semantics=sem, vmem_limit_bytes=VMEM_LIMIT)


def _dot(a, b):
    return jnp.dot(a, b, preferred_element_type=F32)


def _split(a):
    hi = a.astype(BF16)
    lo = (a - hi.astype(F32)).astype(BF16)
    return hi, lo


def _dot_f32(a, b):
    ah, al = _split(a)
    bh, bl = _split(b)
    return _dot(ah, bh) + _dot(ah, bl) + _dot(al, bh)


def _sigmoid(x):
    return 1.0 / (1.0 + jnp.exp(-x))


def _silu(x):
    return x * _sigmoid(x)


def _mod_kernel(c_ref, w_ref, b_ref, o_ref):
    c = c_ref[...]
    o_ref[0] = _dot_f32(_silu(c), w_ref[0]) + b_ref[0]


def modulation(c_rows, w_mod, b_mod):
    depth, d, six_d = w_mod.shape
    nblk = six_d // d
    return pl.pallas_call(
        _mod_kernel,
        grid=(depth, nblk),
        in_specs=[pl.BlockSpec((8, d), lambda l, j: (0, 0)),
                  pl.BlockSpec((1, d, d), lambda l, j: (l, 0, j)),
                  pl.BlockSpec((1, 1, d), lambda l, j: (l, 0, j))],
        out_specs=pl.BlockSpec((1, 8, d), lambda l, j: (l, 0, j)),
        out_shape=jax.ShapeDtypeStruct((depth, 8, six_d), F32),
        compiler_params=_cparams(("parallel", "parallel")),
        name="modulation",
    )(c_rows, w_mod, b_mod.reshape(depth, 1, six_d))


def _modnorm(x, g, shift, scale):
    y = x * lax.rsqrt(jnp.mean(x * x, axis=-1, keepdims=True) + EPS)
    return (y * g) * (1.0 + scale) + shift


def _inproj_kernel(*refs, after_moe):
    if after_moe:
        (d_ref, dn_ref, x_ref, modp_ref, slot_ref, y_hbm, mod_ref, g_ref, w_ref, lb_ref, w2_ref, b2_ref,
         cos_ref, sin_ref, xo_ref, conv_ref, hg_ref, gla_ref, att_ref, bg_ref, tail_ref, ybuf, sem) = refs
    else:
        (x_ref, mod_ref, g_ref, w_ref, lb_ref, w2_ref, b2_ref, cos_ref, sin_ref,
         conv_ref, hg_ref, gla_ref, att_ref, bg_ref, tail_ref) = refs

    @pl.when(pl.program_id(0) == 0)
    def _():
        width = tail_ref.shape[1]
        tail_ref[...] = pltpu.roll(w_ref[:, W_TAIL:W_TAIL + width], width - W_SKEW, axis=1)

    x = x_ref[...]
    if after_moe:
        x = x + modp_ref[0, 5:6, :] * _expert_mix(d_ref, dn_ref, slot_ref, y_hbm, ybuf, sem)
        xo_ref[...] = x
    h = _modnorm(x, g_ref[...], mod_ref[0, 0:1, :], mod_ref[0, 1:2, :]).astype(BF16)

    p = _dot(h, w_ref[:, W_CONV[0]:W_CONV[1]])
    conv_ref[...] = p[:, :CONV_CH] * _sigmoid(p[:, CONV_CH:])

    p = _dot(h, w_ref[:, W_HG[0]:W_HG[1]])
    hg_ref[:, 0:256] = p[:, 0:256]
    hg_ref[:, 768:1024] = p[:, 768:1024]
    hg_ref[:, 1536:1792] = _silu(p[:, 1024:1280])
    for d in range(2):
        z = p[:, 256 * (d + 1):256 * (d + 2)]
        lb = lb_ref[d:d + 1, :]
        hg_ref[:, 256 * (d + 1):256 * (d + 2)] = (1.0 - lb) * _sigmoid(-z)
        hg_ref[:, 256 * (d + 4):256 * (d + 5)] = jnp.maximum(lb + (1.0 - lb) * _sigmoid(z), TINY)

    p = _dot(h, w_ref[:, W_GLA_QKV[0]:W_GLA_QKV[1]])
    gla_ref[:, 0:128] = p[:, 0:128] * (float(GLA_K // GLA_HEADS) ** -0.5)
    gla_ref[:, 128:256] = p[:, 128:256]
    gla_ref[:, 512:768] = p[:, 256:512]
    gla_ref[:, 768:1024] = _silu(_dot(h, tail_ref[:, T_GLA_GATE[0]:T_GLA_GATE[1]]))
    rank = _dot(h, w_ref[:, W_GLA_RANK[0]:W_GLA_RANK[1]])
    u = _dot_f32(rank, w2_ref[...]) + b2_ref[...]
    log_sig = jnp.minimum(u, 0.0) - jnp.log(1.0 + jnp.exp(-jnp.abs(u)))
    gla_ref[:, 256:512] = jnp.exp(log_sig * (1.0 / GLA_TAU))

    p = _dot(h, tail_ref[:, T_ATT[0]:T_ATT[1]])
    cos = cos_ref[...]
    sin = sin_ref[...]
    def rope(v, width):
        lane = lax.broadcasted_iota(jnp.int32, v.shape, 1)
        partner = jnp.where((lane & 31) < 16,
                            pltpu.roll(v, width - 16, axis=1), pltpu.roll(v, 16, axis=1))
        return v * cos[:, :width] + partner * sin[:, :width]

    att_ref[:, 0:256] = rope(p[:, 0:256], 256) * (float(ATT_HD) ** -0.5)
    att_ref[:, 256:384] = rope(p[:, 256:384], 128)
    att_ref[:, 384:512] = p[:, 384:512]

    p = _dot(h, tail_ref[:, T_BG[0]:T_BG[1]])
    bg_ref[...] = _sigmoid(p).astype(BF16)


def inproj(x, modv, g_mix, w_in_p, lb, w2p, b2p, cos_t, sin_t, n_batch, n_ctx_tiles, moe=None):
    n, d = x.shape
    nt = n // TOK_TILE
    tiles_per_batch = nt // n_batch

    def mod_idx(i):
        b = i // tiles_per_batch
        return b * 2 + ((i % tiles_per_batch) >= n_ctx_tiles).astype(jnp.int32)

    const = lambda i: (0, 0)
    row = lambda i: (i, 0)
    seq = lambda i: (i % tiles_per_batch, 0)
    pre_specs, pre_ops, pre_out_specs, pre_out_shapes, pre_scratch = [], [], [], [], []
    if moe is not None:
        modv_prev, ys, table, slots = moe
        pre_specs = [pl.BlockSpec((1, 1, LANES), lambda i: (i, 0, 0), memory_space=pltpu.SMEM),
                     pl.BlockSpec((1, 1, LANES), lambda i: (jnp.minimum(i + 1, nt - 1), 0, 0),
                                  memory_space=pltpu.SMEM)]
        pre_ops = [table, table]
        pre_out_specs = [pl.BlockSpec((TOK_TILE, d), row)]
        pre_out_shapes = [jax.ShapeDtypeStruct((n, d), F32)]
        pre_scratch = [pltpu.VMEM((2, SORT_ROWS, d), F32), pltpu.SemaphoreType.DMA((2,))]
    mid_specs, mid_ops = [], []
    if moe is not None:
        mid_specs = [pl.BlockSpec((1, N_MOD, d), lambda i: (mod_idx(i), 0, 0)),
                     pl.BlockSpec((TOK_TILE, LANES), row),
                     pl.BlockSpec(memory_space=pl.ANY)]
        mid_ops = [modv_prev, slots, ys]
    outs = pl.pallas_call(
        functools.partial(_inproj_kernel, after_moe=moe is not None),
        grid=(nt,),
        in_specs=pre_specs + [pl.BlockSpec((TOK_TILE, d), row)] + mid_specs + [
                  pl.BlockSpec((1, N_MOD, d), lambda i: (mod_idx(i), 0, 0)),
                  pl.BlockSpec((1, d), const),
                  pl.BlockSpec((d, W_IN_PADDED), const, pipeline_mode=pl.Buffered(1)),
                  pl.BlockSpec((2, HG_W), const),
                  pl.BlockSpec((LANES, 2 * GLA_K), const),
                  pl.BlockSpec((1, 2 * GLA_K), const),
                  pl.BlockSpec((TOK_TILE, 256), seq),
                  pl.BlockSpec((TOK_TILE, 256), seq)],
        out_specs=pre_out_specs + [pl.BlockSpec((TOK_TILE, 256), row),
                                   pl.BlockSpec((TOK_TILE, 1792), row),
                                   pl.BlockSpec((TOK_TILE, 1024), row),
                                   pl.BlockSpec((TOK_TILE, 512), row),
                                   pl.BlockSpec((TOK_TILE, 4096), row)],
        out_shape=pre_out_shapes + [jax.ShapeDtypeStruct((n, 256), F32),
                                    jax.ShapeDtypeStruct((n, 1792), F32),
                                    jax.ShapeDtypeStruct((n, 1024), F32),
                                    jax.ShapeDtypeStruct((n, 512), F32),
                                    jax.ShapeDtypeStruct((n, 4096), BF16)],
        scratch_shapes=[pltpu.VMEM((d, W_IN_PADDED - W_TAIL), BF16)] + pre_scratch,
        compiler_params=_cparams(("arbitrary",)),
        name="combine_inproj" if moe is not None else "inproj",
    )(*pre_ops, x, *mid_ops, modv, g_mix, w_in_p, lb, w2p, b2p, cos_t, sin_t)
    return outs


SCAN_IN = 4
SCAN_SCRATCH = 12


ATTN_IN = 11
ATTN_STRIDE = 3


def _trace_alternately(chains):
    live = list(chains)
    rnd = 0
    while live:
        for entry in list(live):
            gen, stride = entry
            if rnd % stride == 0:
                try:
                    next(gen)
                except StopIteration:
                    live.remove(entry)
        rnd += 1


def _scan_kernel(*refs, attn):
    if attn is not None:
        sink_ref, refs = refs[0], refs[1:]
    n_chain = 2
    n_in = n_chain * SCAN_IN
    ins = refs[:n_in]
    sel = refs[n_in]
    perms = refs[n_in + 1:n_in + 5]
    pos = n_in + 5
    if attn is not None:
        att_in = refs[pos:pos + ATTN_IN]
        pos += ATTN_IN
    outs = refs[pos:pos + n_chain]
    pos += n_chain
    if attn is not None:
        att_out = refs[pos]
        pos += 1
    scratch = refs[pos:]
    chains = []
    for i in range(n_chain):
        rev = i % 2 == 1
        chains.append((_scan_direction(*ins[i * SCAN_IN:(i + 1) * SCAN_IN], sel,
                                       perms[2 * rev], perms[2 * rev + 1], outs[i],
                                       *scratch[i * SCAN_SCRATCH:(i + 1) * SCAN_SCRATCH], rev=rev), 1))
    if attn is not None:
        ctx_blocks, blocks_per_batch = attn
        q_ref, kc_ref, vc_ref = att_in[:3]
        kb, vb = att_in[3:7], att_in[7:11]
        per_tile = q_ref.shape[0] // ATT_BLOCK
        for a in range(per_tile):
            rows = slice(a * ATT_BLOCK, (a + 1) * ATT_BLOCK)

            def write(hk, x, rows=rows):
                att_out[rows, hk * LANES:(hk + 1) * LANES] = x

            gens = _attn_chains(pl.program_id(1) * per_tile + a, sink_ref, q_ref[rows, :],
                                [kc_ref[...]] + [kb[a + t][...] for t in range(3)],
                                [vc_ref[...]] + [vb[a + t][...] for t in range(3)],
                                write, ctx_blocks, blocks_per_batch)
            chains += [(g, ATTN_STRIDE) for g in gens]
    _trace_alternately(chains)


def _scan_direction(q_ref, k_ref, v_ref, f_ref, sel_ref, perm_ref, permt_ref, o_ref,
                    s_ref, qs_ref, ks_ref, vs_ref, fs_ref, qt_ref, kh_ref, term_ref, w_ref, ah_ref, al_ref,
                    kv_ref, *, rev):
    c = SCAN_BLOCK
    tt, kl = q_ref.shape
    nb = tt // c
    assert nb == c

    @pl.when(pl.program_id(1) == 0)
    def _():
        s_ref[...] = jnp.zeros_like(s_ref)

    perm = perm_ref[...]
    vb = v_ref[...].astype(BF16)
    f_hi, f_lo = _split(f_ref[...])
    qs_ref[...] = _dot(perm, q_ref[...].astype(BF16))
    ks_ref[...] = _dot(perm, k_ref[...].astype(BF16))
    vs_ref[...] = _dot(perm, vb)
    fs_ref[...] = _dot(perm, f_hi) + _dot(perm, f_lo)
    yield

    def slab(ref, s):
        return ref[s * nb:(s + 1) * nb, :]

    a = slab(fs_ref, 0)
    qt_ref[0:nb, :] = (slab(qs_ref, 0) * a).astype(BF16)
    for s in range(1, c):
        a = a * slab(fs_ref, s)
        qt_ref[s * nb:(s + 1) * nb, :] = (slab(qs_ref, s) * a).astype(BF16)
    gam_t = a.T
    g = jnp.ones((nb, kl), F32)
    for s in range(c - 1, -1, -1):
        kh_ref[s * nb:(s + 1) * nb, :] = (slab(ks_ref, s) * g).astype(BF16)
        g = g * slab(fs_ref, s)
    yield

    heads = 256 // 64
    dk = kl // heads
    perm_t = permt_ref[...]
    qt = _dot(perm_t, qt_ref[...]).astype(BF16)
    kh = _dot(perm_t, kh_ref[...]).astype(BF16)

    def head_lanes(x, h):
        return x[:, :dk] if h == 0 else pltpu.roll(x, kl - dk * h, axis=1)[:, :dk]

    q_heads = [head_lanes(qt, h) for h in range(heads)]
    k_heads = [head_lanes(kh, h) for h in range(heads)]
    v_head = lax.broadcasted_iota(jnp.int32, (c, 256), 1) // 64
    g_head = lax.broadcasted_iota(jnp.int32, (dk, 256), 1) // 64
    for j in range(nb):
        lo = j * c
        k4 = jnp.concatenate([k_heads[h][lo:lo + c, :] for h in range(heads)], axis=0)
        v4 = jnp.concatenate([jnp.where(v_head == h, vb[lo:lo + c, :], jnp.zeros_like(vb[lo:lo + c, :]))
                              for h in range(heads)], axis=0)
        kv_ref[j] = lax.dot_general(k4, v4, (((0,), (0,)), ((), ())), preferred_element_type=F32)
        yield

    n_pair = 0
    for s in range(c):
        p = slab(qs_ref, s)
        for d in range(s + 1):
            if d > 0:
                p = p * slab(fs_ref, s - d + 1)
            term_ref[n_pair * nb:(n_pair + 1) * nb, :] = (p * slab(ks_ref, s - d)).astype(BF16)
            n_pair += 1
        yield
    w_ref[...] = _dot(term_ref[...], sel_ref[...])
    yield
    n_pair = 0
    for s in range(c):
        acc = jnp.zeros((nb, 256), F32)
        for d in range(s + 1):
            acc = acc + w_ref[n_pair * nb:(n_pair + 1) * nb, :] * slab(vs_ref, s - d)
            n_pair += 1
        hi, lo = _split(acc)
        ah_ref[s * nb:(s + 1) * nb, :] = hi
        al_ref[s * nb:(s + 1) * nb, :] = lo
        yield

    o_ref[...] = _dot(perm_t, ah_ref[...]) + _dot(perm_t, al_ref[...])
    yield
    st = s_ref[...]
    for jj in range(nb):
        j = nb - 1 - jj if rev else jj
        lo = j * c
        q4 = jnp.concatenate([q_heads[h][lo:lo + c, :] for h in range(heads)], axis=0)
        o4 = _dot(q4, st.astype(BF16))
        inter = o4[0:c, :]
        for h in range(1, heads):
            inter = jnp.where(v_head == h, o4[h * c:(h + 1) * c, :], inter)
        o_ref[lo:lo + c, :] += inter
        gam = jnp.broadcast_to(gam_t[0:dk, j:j + 1], (dk, 256))
        for h in range(1, heads):
            gam = jnp.where(g_head == h, gam_t[h * dk:(h + 1) * dk, j:j + 1], gam)
        st = gam * st + kv_ref[j]
        yield
    s_ref[...] = st


def gated_scan(pack, cols_fwd, cols_rev, kl, n_batch, n_ctx_tiles, name, attn=None):
    n = pack.shape[0]
    nt = n // TOK_TILE
    tpb = nt // n_batch

    def rev_tile(i):
        return jnp.where(i < n_ctx_tiles, n_ctx_tiles - 1 - i, tpb - 1 - (i - n_ctx_tiles))

    heads = 4
    c = SCAN_BLOCK
    nb = TOK_TILE // c
    perms = []
    for rev in (False, True):
        perm = np.zeros((TOK_TILE, TOK_TILE), np.float32)
        for j in range(nb):
            for s in range(c):
                perm[s * nb + j, j * c + (c - 1 - s if rev else s)] = 1.0
        perms += [jnp.asarray(perm, BF16), jnp.asarray(perm.T, BF16)]
    n_pairs = c * (c + 1) // 2
    fwd = lambda col: (lambda b, i, *_: (b * tpb + i, col))
    bwd = lambda col: (lambda b, i, *_: (b * tpb + rev_tile(i), col))
    const = lambda b, i, *_: (0, 0)

    def direction_specs(rows, cols, kl):
        qc, kc, vc, fc = cols
        specs = [pl.BlockSpec((TOK_TILE, kl), rows(qc)),
                 pl.BlockSpec((TOK_TILE, kl), rows(kc)),
                 pl.BlockSpec((TOK_TILE, 256), rows(vc)),
                 pl.BlockSpec((TOK_TILE, kl), rows(fc))]
        assert len(specs) == SCAN_IN
        return specs

    def chain_scratch(kl):
        dk = kl // heads
        shapes = [pltpu.VMEM((dk, 256), F32),
                  pltpu.VMEM((TOK_TILE, kl), F32),
                  pltpu.VMEM((TOK_TILE, kl), F32),
                  pltpu.VMEM((TOK_TILE, 256), F32),
                  pltpu.VMEM((TOK_TILE, kl), F32),
                  pltpu.VMEM((TOK_TILE, kl), BF16),
                  pltpu.VMEM((TOK_TILE, kl), BF16),
                  pltpu.VMEM((n_pairs * nb, kl), BF16),
                  pltpu.VMEM((n_pairs * nb, 256), F32),
                  pltpu.VMEM((TOK_TILE, 256), BF16),
                  pltpu.VMEM((TOK_TILE, 256), BF16),
                  pltpu.VMEM((nb, dk, 256), F32)]
        assert len(shapes) == SCAN_SCRATCH
        return shapes

    dk = kl // heads
    sel = jnp.asarray(np.arange(kl)[:, None] // dk == np.arange(256)[None, :] // 64, BF16)
    in_specs = direction_specs(fwd, cols_fwd, kl) + direction_specs(bwd, cols_rev, kl)
    in_specs += [pl.BlockSpec(sel.shape, const)] + [pl.BlockSpec((TOK_TILE, TOK_TILE), const)] * 4
    operands = [pack] * (2 * SCAN_IN) + [sel] + perms
    out_specs = [pl.BlockSpec((TOK_TILE, 256), fwd(0)), pl.BlockSpec((TOK_TILE, 256), bwd(0))]
    prefetch, kernel_attn = [], None
    if attn is not None:
        att_pack, sink, ctx_len = attn
        per_tile = TOK_TILE // ATT_BLOCK
        bpb = tpb * per_tile
        cb = ctx_len // ATT_BLOCK

        def band(off, col):
            def idx(b, i, *_):
                return (b * bpb + jnp.clip(i * per_tile + off, cb, bpb - 1), col)
            return pl.BlockSpec((ATT_BLOCK, LANES), idx)

        def ctx(col):
            return pl.BlockSpec((ctx_len, LANES), lambda b, i, *_: (b * (bpb * ATT_BLOCK // ctx_len), col))

        att_specs = ([pl.BlockSpec((TOK_TILE, 256), fwd(0)), ctx(2), ctx(3)]
                     + [band(off, 2) for off in range(-1, per_tile + 1)]
                     + [band(off, 3) for off in range(-1, per_tile + 1)])
        assert len(att_specs) == ATTN_IN
        in_specs += att_specs
        operands += [att_pack] * ATTN_IN
        out_specs.append(pl.BlockSpec((TOK_TILE, 256), fwd(0)))
        prefetch, kernel_attn = [sink], (cb, bpb)
    grid_spec = pltpu.PrefetchScalarGridSpec(
        num_scalar_prefetch=len(prefetch),
        grid=(n_batch, tpb),
        in_specs=in_specs,
        out_specs=out_specs,
        scratch_shapes=chain_scratch(kl) + chain_scratch(kl),
    )
    return pl.pallas_call(
        functools.partial(_scan_kernel, attn=kernel_attn),
        grid_spec=grid_spec,
        out_shape=[jax.ShapeDtypeStruct((n, 256), F32)] * len(out_specs),
        compiler_params=_cparams(("parallel", "arbitrary")),
        name=name,
    )(*prefetch, *operands)


def _attn_chains(n, sink_ref, q, keys, vals, write, ctx_blocks, blocks_per_batch):
    blk = ATT_BLOCK
    lane = lax.broadcasted_iota(jnp.int32, (blk, LANES), 1)
    low = lane < ATT_HD
    qi = lax.broadcasted_iota(jnp.int32, (blk, blk), 0)
    ki = lax.broadcasted_iota(jnp.int32, (blk, blk), 1)
    is_lat = n >= ctx_blocks
    band_ok = [
        jnp.logical_and(jnp.logical_and(is_lat, n - 1 >= ctx_blocks), qi <= ki),
        jnp.logical_and(is_lat, qi >= 0),
        jnp.logical_and(jnp.logical_and(is_lat, n + 1 < blocks_per_batch), ki <= qi),
    ]

    def dup(x, hk):
        xr = pltpu.roll(x, ATT_HD, axis=1)
        lo_x = lax.broadcasted_iota(jnp.int32, x.shape, 1) < ATT_HD
        return (jnp.where(lo_x, x, xr) if hk == 0 else jnp.where(lo_x, xr, x)).astype(BF16)

    nt_dims = (((1,), (1,)), ((), ()))

    def kv_group(hk):
        kd = [dup(x, hk) for x in keys]
        vd = [dup(x, hk) for x in vals]
        qh = q[:, hk * LANES:(hk + 1) * LANES]
        q2 = jnp.concatenate([jnp.where(low, qh, 0.0), jnp.where(low, 0.0, qh)], axis=0).astype(BF16)
        yield
        s = [lax.dot_general(q2, kx, nt_dims, preferred_element_type=F32) for kx in kd]
        for t in range(3):
            s[t + 1] = jnp.where(jnp.concatenate([band_ok[t]] * 2, axis=0), s[t + 1], NEG)
        yield
        sink = jnp.where(lax.broadcasted_iota(jnp.int32, (2 * blk, 1), 0) < blk,
                         sink_ref[hk * 2], sink_ref[hk * 2 + 1])
        pieces = [s[0][:, :LANES], s[0][:, LANES:], s[1], s[2], s[3]]
        top = pieces[0]
        for piece in pieces[1:]:
            top = jnp.maximum(top, piece)
        m = jnp.maximum(sink, top.max(axis=-1, keepdims=True))
        yield
        e = [jnp.exp(x - m) for x in s]
        tot = e[0][:, :LANES] + e[0][:, LANES:] + e[1] + e[2] + e[3]
        den = jnp.exp(sink - m) + tot.sum(axis=-1, keepdims=True)
        yield
        o = _dot(e[0].astype(BF16), vd[0])
        for t in range(1, 4):
            o = o + _dot(e[t].astype(BF16), vd[t])
        yield
        o = o / den
        write(hk, jnp.where(low, o[:blk, :], o[blk:, :]))

    return [kv_group(hk) for hk in range(ATT_KV_HEADS)]


def _head_norm(o, ones_bd, g):
    sq = o * o
    hi, lo = _split(sq)
    ms = (_dot(hi, ones_bd) + _dot(lo, ones_bd)) * (1.0 / 64.0)
    return o * lax.rsqrt(ms + EPS) * g


def _route_select(aff, bias):
    lane = lax.broadcasted_iota(jnp.int32, aff.shape, 1)
    epg = N_EXPERTS // N_GROUPS
    pos = lane & (epg - 1)
    v = aff + bias

    def nxt(x, o):
        return pltpu.roll(x, LANES - o, axis=1)

    def prv(x, o):
        return pltpu.roll(x, o, axis=1)

    beaten = jnp.zeros(aff.shape, jnp.int32)
    for o in range(1, epg):
        beaten = beaten + jnp.where(jnp.logical_and(pos + o < epg, nxt(v, o) > v), 1, 0)
        beaten = beaten + jnp.where(jnp.logical_and(pos >= o, prv(v, o) >= v), 1, 0)
    top2 = beaten < TOP_K
    t = jnp.where(top2, v, 0.0)
    score = t
    for o in range(1, epg):
        score = score + jnp.where(pos + o < epg, nxt(t, o), 0.0) + jnp.where(pos >= o, prv(t, o), 0.0)
    worse = jnp.zeros(aff.shape, jnp.int32)
    for o in range(epg, N_EXPERTS, epg):
        worse = worse + jnp.where(jnp.logical_and(lane + o < N_EXPERTS, nxt(score, o) > score), 1, 0)
        worse = worse + jnp.where(jnp.logical_and(lane >= o, prv(score, o) >= score), 1, 0)
    sel = jnp.logical_and(jnp.logical_and(lane < N_EXPERTS, worse == 0), top2)
    picked = jnp.where(sel, aff, 0.0)
    gate = picked / jnp.sum(picked, axis=-1, keepdims=True)
    return gate, sel


def _merge_kernel(x_ref, mod_ref, zp_ref, z_ref, zn_ref, hgf_ref, hgb_ref, hgg_ref, glf_ref, glb_ref,
                  glg_ref, att_ref, bg_ref, cw_ref, cb_ref, lng_ref, lnb_ref, hgn_ref, gln_ref,
                  ones_ref, wbc_ref, wbh_ref, wbg_ref, wba_ref, wo_ref, gf_ref, rw_ref, rb_ref, tri_ref,
                  upper_ref, xo_ref, h2_ref, slot_ref, gran_ref, zs_ref, zsh_ref, *, tiles_per_batch,
                  n_ctx_tiles):
    i = pl.program_id(0)
    tt = x_ref.shape[0]
    ti = i % tiles_per_batch
    has_prev = jnp.logical_and(ti != 0, ti != n_ctx_tiles)
    has_next = jnp.logical_and(ti != n_ctx_tiles - 1, ti != tiles_per_batch - 1)

    zs_ref[0:HALO, :] = jnp.where(has_prev, zp_ref[...], 0.0)
    zs_ref[HALO:HALO + tt, :] = z_ref[...]
    zs_ref[HALO + tt:HALO + tt + HALO, :] = jnp.where(has_next, zn_ref[...], 0.0)
    span = tt + 2 * HALO - 8
    for b in range(8):
        zsh_ref[b] = zs_ref[b:b + span, :]
    acc = jnp.zeros((tt, CONV_CH), F32) + cb_ref[...]
    for j in range(CONV_K):
        off = HALO - CONV_K // 2 + j
        acc = acc + zsh_ref[off % 8, off - off % 8:off - off % 8 + tt, :] * cw_ref[j:j + 1, :]
    mu = jnp.mean(acc, axis=-1, keepdims=True)
    cen = acc - mu
    var = jnp.mean(cen * cen, axis=-1, keepdims=True)
    conv_y = _silu(cen * lax.rsqrt(var + EPS) * lng_ref[...] + lnb_ref[...])

    ones_bd = ones_ref[...]
    hg_y = _head_norm(hgf_ref[...] + hgb_ref[...], ones_bd, hgn_ref[...]) * hgg_ref[...]
    gla_y = _head_norm(glf_ref[...] + glb_ref[...], ones_bd, gln_ref[...]) * glg_ref[...]

    d = x_ref.shape[1]
    merged = bg_ref[:, 0:d].astype(F32) * _dot(conv_y.astype(BF16), wbc_ref[...])
    merged = merged + bg_ref[:, d:2 * d].astype(F32) * _dot(hg_y.astype(BF16), wbh_ref[...])
    merged = merged + bg_ref[:, 2 * d:3 * d].astype(F32) * _dot(gla_y.astype(BF16), wbg_ref[...])
    merged = merged + bg_ref[:, 3 * d:4 * d].astype(F32) * _dot(att_ref[...].astype(BF16), wba_ref[...])
    mix = _dot(merged.astype(BF16), wo_ref[...])

    x_new = x_ref[...] + mod_ref[0, 2:3, :] * mix
    xo_ref[...] = x_new
    h2 = _modnorm(x_new, gf_ref[...], mod_ref[0, 3:4, :], mod_ref[0, 4:5, :])
    h2_ref[...] = h2
    gate, sel = _route_select(_sigmoid(_dot_f32(h2, rw_ref[...])), rb_ref[...])

    incl = _dot(tri_ref[...], sel.astype(F32).astype(BF16))
    gran = jnp.floor((incl[tt - 1:tt, :] + (GRANULE - 1.0)) * (1.0 / GRANULE))
    gran8 = jnp.broadcast_to(gran, (8, LANES))
    start = _dot(gran8.astype(BF16), upper_ref[...])[0:1, :]
    pos = GRANULE * start + incl - 1.0
    p0 = jnp.min(jnp.where(sel, pos, 1e9), axis=-1, keepdims=True)
    p1 = jnp.max(jnp.where(sel, pos, -1.0), axis=-1, keepdims=True)
    g0 = jnp.sum(jnp.where(jnp.logical_and(sel, pos == p0), gate, 0.0), axis=-1, keepdims=True)
    g1 = jnp.sum(jnp.where(jnp.logical_and(sel, pos == p1), gate, 0.0), axis=-1, keepdims=True)
    lane = lax.broadcasted_iota(jnp.int32, (tt, LANES), 1)
    slot_ref[...] = jnp.where(lane == 0, p0, jnp.where(lane == 1, p1, jnp.where(lane == 2, g0,
                              jnp.where(lane == 3, g1, 0.0))))
    gran_ref[0] = gran8.astype(jnp.int32)


def merge(x, modv, conv_z, hg_pack, hg_f, hg_b, gla_pack, gla_f, gla_b, att_o, bgate, lw, n_batch,
          n_ctx_tiles):
    n, d = x.shape
    nt = n // TOK_TILE
    tpb = nt // n_batch
    hpt = TOK_TILE // HALO
    n_halo = n // HALO

    def mod_idx(i):
        return (i // tpb) * 2 + ((i % tpb) >= n_ctx_tiles).astype(jnp.int32)

    row = lambda i: (i, 0)
    const = lambda i: (0, 0)
    col = lambda c: (lambda i: (i, c))
    full = lambda a: pl.BlockSpec(a.shape, const)
    weights = [lw["conv_w"], lw["conv_b"], lw["conv_ln_g"], lw["conv_ln_b"], lw["hg_norm_g"],
               lw["gla_norm_g"], lw["ones_bd"], lw["w_br_conv"], lw["w_br_hg"], lw["w_br_gla"],
               lw["w_br_att"], lw["w_out"], lw["g_ffn"], lw["router_w"], lw["router_b"], lw["tri"],
               lw["upper"]]
    return pl.pallas_call(
        functools.partial(_merge_kernel, tiles_per_batch=tpb, n_ctx_tiles=n_ctx_tiles),
        grid=(nt,),
        in_specs=[pl.BlockSpec((TOK_TILE, d), row),
                  pl.BlockSpec((1, N_MOD, d), lambda i: (mod_idx(i), 0, 0)),
                  pl.BlockSpec((HALO, CONV_CH), lambda i: (jnp.maximum(i * hpt - 1, 0), 0)),
                  pl.BlockSpec((TOK_TILE, CONV_CH), row),
                  pl.BlockSpec((HALO, CONV_CH), lambda i: (jnp.minimum((i + 1) * hpt, n_halo - 1), 0)),
                  pl.BlockSpec((TOK_TILE, 256), row),
                  pl.BlockSpec((TOK_TILE, 256), row),
                  pl.BlockSpec((TOK_TILE, 256), col(6)),
                  pl.BlockSpec((TOK_TILE, 256), row),
                  pl.BlockSpec((TOK_TILE, 256), row),
                  pl.BlockSpec((TOK_TILE, 256), col(3)),
                  pl.BlockSpec((TOK_TILE, 256), row),
                  pl.BlockSpec((TOK_TILE, 4 * d), row)] + [full(w) for w in weights],
        out_specs=[pl.BlockSpec((TOK_TILE, d), row),
                   pl.BlockSpec((TOK_TILE, d), row),
                   pl.BlockSpec((TOK_TILE, LANES), row),
                   pl.BlockSpec((1, 8, LANES), lambda i: (i, 0, 0))],
        out_shape=[jax.ShapeDtypeStruct((n, d), F32),
                   jax.ShapeDtypeStruct((n, d), F32),
                   jax.ShapeDtypeStruct((n, LANES), F32),
                   jax.ShapeDtypeStruct((nt, 8, LANES), jnp.int32)],
        scratch_shapes=[pltpu.VMEM((TOK_TILE + 2 * HALO, CONV_CH), F32),
                        pltpu.VMEM((8, TOK_TILE + 2 * HALO - 8, CONV_CH), F32)],
        compiler_params=_cparams(("parallel",)),
        name="merge",
    )(x, modv, conv_z, conv_z, conv_z, hg_f, hg_b, hg_pack, gla_f, gla_b, gla_pack, att_o, bgate,
      *weights)


def _granule_copy(src, src_row, dst, dst_row, sem):
    return pltpu.make_async_copy(src.at[pl.ds(pl.multiple_of(src_row, GRANULE), GRANULE)],
                                 dst.at[pl.ds(pl.multiple_of(dst_row, GRANULE), GRANULE)], sem)


def _slot_matrix(slot_ref, width, weighted):
    tt = slot_ref.shape[0]
    col = lax.broadcasted_iota(jnp.int32, (tt, width), 1).astype(F32)
    hit0 = col == slot_ref[:, 0:1]
    hit1 = col == slot_ref[:, 1:2]
    if not weighted:
        return jnp.where(jnp.logical_or(hit0, hit1), 1.0, 0.0)
    return jnp.where(hit0, slot_ref[:, 2:3], 0.0) + jnp.where(hit1, slot_ref[:, 3:4], 0.0)


def _dispatch_kernel(tab_ref, tabp_ref, tail_ref, h_ref, slot_ref, xs_hbm, buf, zbuf, sem, zsem, bsem):
    i = pl.program_id(0)
    nb = pl.num_programs(0)
    slot = i % 2
    ng = tab_ref[0, 0, LANES - 1]

    perm = _slot_matrix(slot_ref, buf.shape[1], weighted=False).astype(BF16)
    buf[slot] = lax.dot_general(perm, h_ref[...].astype(BF16), (((0,), (0,)), ((), ())),
                                preferred_element_type=F32)

    def issue(j, carry):
        _granule_copy(buf.at[slot], j * GRANULE, xs_hbm, tab_ref[0, 0, j] * GRANULE, sem.at[slot]).start()
        return carry
    lax.fori_loop(0, ng, issue, 0)

    def drain(count, s):
        def body(j, carry):
            _granule_copy(buf.at[s], 0, xs_hbm, 0, sem.at[s]).wait()
            return carry
        lax.fori_loop(0, count, body, 0)

    @pl.when(i > 0)
    def _():
        drain(tabp_ref[0, 0, LANES - 1], 1 - slot)

    @pl.when(i == nb - 1)
    def _():
        drain(ng, slot)

    @pl.when(i == 0)
    def _():
        zbuf[...] = jnp.zeros_like(zbuf)
        n_used = tail_ref[0, 0, 2 * N_EXPERTS]
        n_blocks = xs_hbm.shape[0] // MOE_BLOCK

        def block_copy(b):
            return pltpu.make_async_copy(
                zbuf, xs_hbm.at[pl.ds(pl.multiple_of(b * MOE_BLOCK, MOE_BLOCK), MOE_BLOCK)], bsem)

        for e in range(N_EXPERTS):
            def fill(m, carry, e=e):
                _granule_copy(zbuf, 0, xs_hbm, (tail_ref[0, 0, e] + m) * GRANULE, zsem).start()
                return carry
            lax.fori_loop(0, tail_ref[0, 0, N_EXPERTS + e], fill, 0)

        def fill_block(b, carry):
            block_copy(b).start()
            return carry
        lax.fori_loop(n_used, n_blocks, fill_block, 0)

        for e in range(N_EXPERTS):
            def done(m, carry):
                _granule_copy(zbuf, 0, xs_hbm, 0, zsem).wait()
                return carry
            lax.fori_loop(0, tail_ref[0, 0, N_EXPERTS + e], done, 0)

        def done_block(b, carry):
            block_copy(b).wait()
            return carry
        lax.fori_loop(n_used, n_blocks, done_block, 0)


def dispatch(h2, slots, table, tails, n_rows):
    n, d = h2.shape
    nt = n // TOK_TILE
    return pl.pallas_call(
        _dispatch_kernel,
        grid=(nt,),
        in_specs=[pl.BlockSpec((1, 1, LANES), lambda i: (i, 0, 0), memory_space=pltpu.SMEM),
                  pl.BlockSpec((1, 1, LANES), lambda i: (jnp.maximum(i - 1, 0), 0, 0), memory_space=pltpu.SMEM),
                  pl.BlockSpec((1, 1, LANES), lambda i: (0, 0, 0), memory_space=pltpu.SMEM),
                  pl.BlockSpec((TOK_TILE, d), lambda i: (i, 0)),
                  pl.BlockSpec((TOK_TILE, LANES), lambda i: (i, 0))],
        out_specs=pl.BlockSpec(memory_space=pl.ANY),
        out_shape=jax.ShapeDtypeStruct((n_rows, d), F32),
        scratch_shapes=[pltpu.VMEM((2, SORT_ROWS, d), F32),
                        pltpu.VMEM((MOE_BLOCK, d), F32),
                        pltpu.SemaphoreType.DMA((2,)),
                        pltpu.SemaphoreType.DMA(()),
                        pltpu.SemaphoreType.DMA(())],
        compiler_params=_cparams(("arbitrary",)),
        name="dispatch",
    )(table, table, tails, h2, slots)


def _expert_kernel(be_ref, nu_ref, x_ref, w1_ref, w3_ref, w2_ref, y_ref, w1b, w3b, w2b):
    i = pl.program_id(0)
    n_used = nu_ref[0]

    @pl.when(i < n_used)
    def _():
        first = jnp.logical_or(i == 0, be_ref[i] != be_ref[jnp.maximum(i - 1, 0)])

        @pl.when(first)
        def _():
            w1b[...] = w1_ref[0].astype(BF16)
            w3b[...] = w3_ref[0].astype(BF16)
            w2b[...] = w2_ref[0].astype(BF16)

        x = x_ref[...].astype(BF16)
        hid = _silu(_dot(x, w1b[...])) * _dot(x, w3b[...])
        y_ref[...] = _dot(hid.astype(BF16), w2b[...])

    @pl.when(i >= n_used)
    def _():
        y_ref[...] = jnp.zeros_like(y_ref)


def experts(xs, block_expert, n_used, w1, w3, w2, layer):
    n_rows, d = xs.shape
    n_blocks = n_rows // MOE_BLOCK
    ff = w1.shape[-1]
    grid_spec = pltpu.PrefetchScalarGridSpec(
        num_scalar_prefetch=2,
        grid=(n_blocks,),
        in_specs=[pl.BlockSpec((MOE_BLOCK, d), lambda i, be, nu: (jnp.minimum(i, nu[0] - 1), 0)),
                  pl.BlockSpec((None, 1, d, ff), lambda i, be, nu: (layer, be[i], 0, 0)),
                  pl.BlockSpec((None, 1, d, ff), lambda i, be, nu: (layer, be[i], 0, 0)),
                  pl.BlockSpec((None, 1, ff, d), lambda i, be, nu: (layer, be[i], 0, 0))],
        out_specs=pl.BlockSpec((MOE_BLOCK, d), lambda i, be, nu: (i, 0)),
        scratch_shapes=[pltpu.VMEM((d, ff), BF16),
                        pltpu.VMEM((d, ff), BF16),
                        pltpu.VMEM((ff, d), BF16)],
    )
    return pl.pallas_call(
        _expert_kernel,
        grid_spec=grid_spec,
        out_shape=jax.ShapeDtypeStruct((n_rows, d), F32),
        compiler_params=_cparams(("arbitrary",)),
        name="experts",
    )(block_expert, n_used, xs, w1, w3, w2)


def _combine_kernel(d_ref, dn_ref, x_ref, mod_ref, g_ref, slot_ref, y_hbm, o_ref, ybuf, sem, *, final):
    y = _expert_mix(d_ref, dn_ref, slot_ref, y_hbm, ybuf, sem)
    x = x_ref[...] + mod_ref[0, 5:6, :] * y
    if final:
        x = x * lax.rsqrt(jnp.mean(x * x, axis=-1, keepdims=True) + EPS) * g_ref[...]
    o_ref[...] = x


def _expert_mix(d_ref, dn_ref, slot_ref, y_hbm, ybuf, sem):
    i = pl.program_id(0)
    nb = pl.num_programs(0)
    slot = i % 2
    n_gran = ybuf.shape[1] // GRANULE

    def fetch(tab, s):
        def body(j, carry):
            _granule_copy(y_hbm, tab[0, 0, j] * GRANULE, ybuf.at[s], j * GRANULE, sem.at[s]).start()
            return carry
        lax.fori_loop(0, n_gran, body, 0)

    @pl.when(i == 0)
    def _():
        fetch(d_ref, 0)

    @pl.when(i + 1 < nb)
    def _():
        fetch(dn_ref, 1 - slot)

    def drain(j, carry):
        _granule_copy(y_hbm, 0, ybuf.at[slot], 0, sem.at[slot]).wait()
        return carry
    lax.fori_loop(0, n_gran, drain, 0)

    hi, lo = _split(_slot_matrix(slot_ref, ybuf.shape[1], weighted=True))
    yb = ybuf[slot].astype(BF16)
    return _dot(hi, yb) + _dot(lo, yb)


def combine(x, modv, ys, table, slots, final_g, n_batch, n_ctx_tiles, final):
    n, d = x.shape
    nt = n // TOK_TILE
    tpb = nt // n_batch
    if final:
        lat = tpb - n_ctx_tiles
        steps = n_batch * lat
        tile = lambda s: (s // lat) * tpb + n_ctx_tiles + s % lat
        out_rows = steps * TOK_TILE
    else:
        steps = nt
        tile = lambda s: s
        out_rows = n

    def mod_idx(s):
        t = tile(s)
        return (t // tpb) * 2 + ((t % tpb) >= n_ctx_tiles).astype(jnp.int32)

    return pl.pallas_call(
        functools.partial(_combine_kernel, final=final),
        grid=(steps,),
        in_specs=[pl.BlockSpec((1, 1, LANES), lambda s: (tile(s), 0, 0), memory_space=pltpu.SMEM),
                  pl.BlockSpec((1, 1, LANES), lambda s: (tile(jnp.minimum(s + 1, steps - 1)), 0, 0),
                               memory_space=pltpu.SMEM),
                  pl.BlockSpec((TOK_TILE, d), lambda s: (tile(s), 0)),
                  pl.BlockSpec((1, N_MOD, d), lambda s: (mod_idx(s), 0, 0)),
                  pl.BlockSpec((1, d), lambda s: (0, 0)),
                  pl.BlockSpec((TOK_TILE, LANES), lambda s: (tile(s), 0)),
                  pl.BlockSpec(memory_space=pl.ANY)],
        out_specs=pl.BlockSpec((TOK_TILE, d), lambda s: (s, 0)),
        out_shape=jax.ShapeDtypeStruct((out_rows, d), F32),
        scratch_shapes=[pltpu.VMEM((2, SORT_ROWS, d), F32),
                        pltpu.SemaphoreType.DMA((2,))],
        compiler_params=_cparams(("arbitrary",)),
        name="combine_final" if final else "combine",
    )(table, table, x, modv, final_g, slots, ys)


def sorted_rows_bound(n_tok):
    nt = n_tok // TOK_TILE
    rows = n_tok * TOP_K + nt * N_EXPERTS * (GRANULE - 1) + N_EXPERTS * (MOE_BLOCK - GRANULE)
    return -(-rows // MOE_BLOCK) * MOE_BLOCK


def moe_tables(gran, n_rows):
    nt = gran.shape[0]
    per_blk = MOE_BLOCK // GRANULE
    local = jnp.cumsum(gran, axis=1) - gran
    n_gran = jnp.sum(gran, axis=1)
    before = jnp.cumsum(gran, axis=0) - gran
    total = jnp.sum(gran, axis=0)
    padded = (total + per_blk - 1) // per_blk * per_blk
    region_end = jnp.cumsum(padded)
    region = region_end - padded
    j = jnp.arange(LANES, dtype=jnp.int32)[None, None, :]
    inside = jnp.logical_and(j >= local[:, :, None], j < (local + gran)[:, :, None])
    dst = region[None, :, None] + before[:, :, None] + j - local[:, :, None]
    table = jnp.sum(jnp.where(inside, dst, 0), axis=1)
    valid = j[0] < n_gran[:, None]
    table = jnp.where(valid, table, table[:, :1])
    table = table.at[:, LANES - 1].set(n_gran).astype(jnp.int32).reshape(nt, 1, LANES)
    tails = jnp.zeros((LANES,), jnp.int32).at[:N_EXPERTS].set(region + total)
    tails = tails.at[N_EXPERTS:2 * N_EXPERTS].set(padded - total)
    n_blocks = n_rows // MOE_BLOCK
    blk = jnp.arange(n_blocks, dtype=jnp.int32) * per_blk
    block_expert = jnp.minimum(jnp.sum(blk[:, None] >= region_end[None, :], axis=1), N_EXPERTS - 1)
    n_used = (region_end[-1] // per_blk).astype(jnp.int32).reshape(1)
    tails = tails.at[2 * N_EXPERTS].set(n_used[0]).reshape(1, 1, LANES)
    return table, tails, block_expert.astype(jnp.int32), n_used


def _rope_tables(ctx_len, n_lat):
    rows = n_lat // GRID_W
    row = np.repeat(np.arange(rows, dtype=np.float32), GRID_W)
    col = np.tile(np.arange(GRID_W, dtype=np.float32), rows)
    half = ATT_HD // 4
    inv = jnp.asarray(ROPE_BASE, F32) ** (-jnp.arange(half, dtype=F32) / half)
    ang_r = jnp.asarray(row)[:, None] * inv
    ang_c = jnp.asarray(col)[:, None] * inv
    cos64 = jnp.concatenate([jnp.cos(ang_r)] * 2 + [jnp.cos(ang_c)] * 2, axis=1)
    sin64 = jnp.concatenate([-jnp.sin(ang_r), jnp.sin(ang_r), -jnp.sin(ang_c), jnp.sin(ang_c)], axis=1)
    cos64 = jnp.concatenate([jnp.ones((ctx_len, ATT_HD), F32), cos64], axis=0)
    sin64 = jnp.concatenate([jnp.zeros((ctx_len, ATT_HD), F32), sin64], axis=0)
    return jnp.tile(cos64, (1, ATT_HEADS)), jnp.tile(sin64, (1, ATT_HEADS))


def _pack_w_in(w):
    assert w.shape[-1] == W_IN_COLS
    return jnp.pad(w.astype(BF16), ((0, 0), (0, W_IN_PADDED - W_IN_COLS)))


def kernel(x, c, ctx, c_ctx, hg_lb_logits, router_w, router_b, final_g, w_mod, b_mod, g_mix, g_ffn,
           w_in, conv_w, conv_b, conv_ln_g, conv_ln_b, hg_norm_g, gla_w2, gla_b2, gla_norm_g, att_sink,
           w_br_conv, w_br_hg, w_br_gla, w_br_att, w_out, moe_w1, moe_w3, moe_w2):
    n_batch, n_lat, d = x.shape
    ctx_len = ctx.shape[1]
    depth = w_in.shape[0]
    assert ctx_len % TOK_TILE == 0 and n_lat % TOK_TILE == 0 and n_lat % GRID_W == 0
    seq = ctx_len + n_lat
    n_ctx_tiles = ctx_len // TOK_TILE

    xs = jnp.concatenate([ctx, x], axis=1).reshape(n_batch * seq, d)

    c_rows = jnp.zeros((8, d), F32).at[:n_batch].set(c).at[n_batch].set(c_ctx)
    mods = modulation(c_rows, w_mod, b_mod).reshape(depth, 8, N_MOD, d)
    modv = jnp.stack([mods[:, n_batch] if j % 2 == 0 else mods[:, j // 2] for j in range(2 * n_batch)], axis=1)

    lb_sm = jax.nn.softmax(hg_lb_logits.astype(F32), axis=0)
    lower = jnp.cumsum(lb_sm, axis=0) - lb_sm[0]
    cos_t, sin_t = _rope_tables(ctx_len, n_lat)
    ones_bd = jnp.asarray(np.arange(256)[:, None] // 64 == np.arange(256)[None, :] // 64, BF16)
    rw = jnp.zeros((d, LANES), F32).at[:, :N_EXPERTS].set(router_w.astype(F32))
    rb = jnp.zeros((1, LANES), F32).at[0, :N_EXPERTS].set(router_b.astype(F32))
    tri = jnp.asarray(np.tril(np.ones((TOK_TILE, TOK_TILE))), BF16)
    upper = jnp.asarray(np.triu(np.ones((LANES, LANES)), k=1), BF16)

    pending = None
    for l in range(depth):
        w2p = jnp.zeros((LANES, 2 * GLA_K), F32)
        w2p = w2p.at[0:GLA_RANK, 0:GLA_K].set(gla_w2[l, 0]).at[GLA_RANK:2 * GLA_RANK, GLA_K:].set(gla_w2[l, 1])
        b2p = gla_b2[l].reshape(1, 2 * GLA_K)
        proj = inproj(xs, modv[l], g_mix[l].reshape(1, d), _pack_w_in(w_in[l]), lower[l], w2p, b2p, cos_t, sin_t,
                      n_batch, n_ctx_tiles, moe=pending)
        if pending is not None:
            xs, proj = proj[0], proj[1:]
        conv_z, hg_pack, gla_pack, att_pack, bgate = proj

        hg_f, hg_b = gated_scan(hg_pack, (0, 1, 3, 4), (0, 2, 3, 5), HG_W, n_batch, n_ctx_tiles, "scan_hgrn")
        gla_f, gla_b, att_o = gated_scan(gla_pack, (0, 1, 2, 2), (0, 1, 2, 3), GLA_K, n_batch, n_ctx_tiles,
                                         "scan_gla_attn", attn=(att_pack, att_sink[l].astype(F32), ctx_len))

        lw = dict(conv_w=jnp.zeros((32, CONV_CH), F32).at[:CONV_K].set(conv_w[l]),
                  conv_b=conv_b[l].reshape(1, -1), conv_ln_g=conv_ln_g[l].reshape(1, -1),
                  conv_ln_b=conv_ln_b[l].reshape(1, -1), hg_norm_g=hg_norm_g[l].reshape(1, -1),
                  gla_norm_g=gla_norm_g[l].reshape(1, -1), ones_bd=ones_bd,
                  w_br_conv=w_br_conv[l].astype(BF16), w_br_hg=w_br_hg[l].astype(BF16),
                  w_br_gla=w_br_gla[l].astype(BF16), w_br_att=w_br_att[l].astype(BF16),
                  w_out=w_out[l].astype(BF16), g_ffn=g_ffn[l].reshape(1, d), router_w=rw, router_b=rb,
                  tri=tri, upper=upper)
        x_new, h2, slots, gran = merge(xs, modv[l], conv_z, hg_pack, hg_f, hg_b, gla_pack, gla_f, gla_b,
                                       att_o, bgate, lw, n_batch, n_ctx_tiles)

        n_rows = sorted_rows_bound(n_batch * seq)
        table, tails, block_expert, n_used = moe_tables(gran[:, 0, :N_EXPERTS], n_rows)
        x_sorted = dispatch(h2, slots, table, tails, n_rows)
        ys = experts(x_sorted, block_expert, n_used, moe_w1, moe_w3, moe_w2, l)
        if l < depth - 1:
            xs, pending = x_new, (modv[l], ys, table, slots)
        else:
            xs = combine(x_new, modv[l], ys, table, slots, final_g.reshape(1, d), n_batch, n_ctx_tiles, True)

    return xs.reshape(n_batch, n_lat, d)
```

```python
import functools
import itertools

import numpy as np
import jax
import jax.numpy as jnp
from jax import lax
from jax.experimental import pallas as pl
from jax.experimental.pallas import tpu as pltpu

F32 = jnp.float32
BF16 = jnp.bfloat16

EPS = 1e-6
NEG = -1e30
TINY = 1e-30
N_MOD = 6
CONV_CH = 256
CONV_K = 31
HG_HEADS = 4
HG_W = 256
GLA_HEADS = 4
GLA_K = 128
GLA_V = 256
GLA_RANK = 16
GLA_TAU = 16.0
ATT_HEADS = 4
ATT_KV_HEADS = 2
ATT_HD = 64
ATT_BLOCK = 128
GRID_W = 64
ROPE_BASE = 10000.0
N_EXPERTS = 16
N_GROUPS = 4
TOP_K = 2
MOE_BLOCK = 256

LANES = 128
TOK_TILE = 256
SCAN_BLOCK = 16
HALO = 16
GRANULE = 8
SORT_ROWS = -(-(TOP_K * TOK_TILE + N_EXPERTS * (GRANULE - 1)) // LANES) * LANES
VMEM_LIMIT = 56 * 1024 * 1024

W_CONV = (0, 512)
W_HG = (512, 1792)
W_GLA_QKV = (1792, 2304)
W_TAIL = 2304
W_IN_COLS = 7200
W_IN_PADDED = 7296
W_GLA_RANK = (W_TAIL, W_TAIL + LANES)
W_SKEW = 2 * GLA_RANK
T_GLA_GATE = (0, 256)
T_ATT = (256, 768)
T_BG = (768, 4864)


def _cparams(sem):
    return pltpu.CompilerParams(dimension_semantics=sem, vmem_limit_bytes=VMEM_LIMIT)


def _dot(a, b):
    return jnp.dot(a, b, preferred_element_type=F32)


def _split(a):
    hi = a.astype(BF16)
    lo = (a - hi.astype(F32)).astype(BF16)
    return hi, lo


def _dot_f32(a, b):
    ah, al = _split(a)
    bh, bl = _split(b)
    return _dot(ah, bh) + _dot(ah, bl) + _dot(al, bh)


def _sigmoid(x):
    return 1.0 / (1.0 + jnp.exp(-x))


def _silu(x):
    return x * _sigmoid(x)


def _mod_kernel(c_ref, w_ref, b_ref, o_ref):
    c = c_ref[...]
    o_ref[0] = _dot_f32(_silu(c), w_ref[0]) + b_ref[0]


def modulation(c_rows, w_mod, b_mod):
    depth, d, six_d = w_mod.shape
    nblk = six_d // d
    return pl.pallas_call(
        _mod_kernel,
        grid=(depth, nblk),
        in_specs=[pl.BlockSpec((8, d), lambda l, j: (0, 0)),
                  pl.BlockSpec((1, d, d), lambda l, j: (l, 0, j)),
                  pl.BlockSpec((1, 1, d), lambda l, j: (l, 0, j))],
        out_specs=pl.BlockSpec((1, 8, d), lambda l, j: (l, 0, j)),
        out_shape=jax.ShapeDtypeStruct((depth, 8, six_d), F32),
        compiler_params=_cparams(("parallel", "parallel")),
        name="modulation",
    )(c_rows, w_mod, b_mod.reshape(depth, 1, six_d))


def _modnorm(x, g, shift, scale):
    y = x * lax.rsqrt(jnp.mean(x * x, axis=-1, keepdims=True) + EPS)
    return (y * g) * (1.0 + scale) + shift


def _inproj_kernel(*refs, after_moe):
    if after_moe:
        (d_ref, dn_ref, x_ref, modp_ref, slot_ref, y_hbm, mod_ref, g_ref, w_ref, lb_ref, w2_ref, b2_ref,
         cos_ref, sin_ref, xo_ref, conv_ref, hg_ref, gla_ref, att_ref, bg_ref, tail_ref, ybuf, sem) = refs
    else:
        (x_ref, mod_ref, g_ref, w_ref, lb_ref, w2_ref, b2_ref, cos_ref, sin_ref,
         conv_ref, hg_ref, gla_ref, att_ref, bg_ref, tail_ref) = refs

    @pl.when(pl.program_id(0) == 0)
    def _():
        width = tail_ref.shape[1]
        tail_ref[...] = pltpu.roll(w_ref[:, W_TAIL:W_TAIL + width], width - W_SKEW, axis=1)

    x = x_ref[...]
    if after_moe:
        x = x + modp_ref[0, 5:6, :] * _expert_mix(d_ref, dn_ref, slot_ref, y_hbm, ybuf, sem)
        xo_ref[...] = x
    h = _modnorm(x, g_ref[...], mod_ref[0, 0:1, :], mod_ref[0, 1:2, :]).astype(BF16)

    p = _dot(h, w_ref[:, W_CONV[0]:W_CONV[1]])
    conv_ref[...] = p[:, :CONV_CH] * _sigmoid(p[:, CONV_CH:])

    p = _dot(h, w_ref[:, W_HG[0]:W_HG[1]])
    hg_ref[:, 0:256] = p[:, 0:256]
    hg_ref[:, 768:1024] = p[:, 768:1024]
    hg_ref[:, 1536:1792] = _silu(p[:, 1024:1280])
    for d in range(2):
        z = p[:, 256 * (d + 1):256 * (d + 2)]
        lb = lb_ref[d:d + 1, :]
        hg_ref[:, 256 * (d + 1):256 * (d + 2)] = (1.0 - lb) * _sigmoid(-z)
        hg_ref[:, 256 * (d + 4):256 * (d + 5)] = jnp.maximum(lb + (1.0 - lb) * _sigmoid(z), TINY)

    p = _dot(h, w_ref[:, W_GLA_QKV[0]:W_GLA_QKV[1]])
    gla_ref[:, 0:128] = p[:, 0:128] * (float(GLA_K // GLA_HEADS) ** -0.5)
    gla_ref[:, 128:256] = p[:, 128:256]
    gla_ref[:, 512:768] = p[:, 256:512]
    gla_ref[:, 768:1024] = _silu(_dot(h, tail_ref[:, T_GLA_GATE[0]:T_GLA_GATE[1]]))
    rank = _dot(h, w_ref[:, W_GLA_RANK[0]:W_GLA_RANK[1]])
    u = _dot_f32(rank, w2_ref[...]) + b2_ref[...]
    log_sig = jnp.minimum(u, 0.0) - jnp.log(1.0 + jnp.exp(-jnp.abs(u)))
    gla_ref[:, 256:512] = jnp.exp(log_sig * (1.0 / GLA_TAU))

    p = _dot(h, tail_ref[:, T_ATT[0]:T_ATT[1]])
    cos = cos_ref[...]
    sin = sin_ref[...]
    def rope(v, width):
        lane = lax.broadcasted_iota(jnp.int32, v.shape, 1)
        partner = jnp.where((lane & 31) < 16,
                            pltpu.roll(v, width - 16, axis=1), pltpu.roll(v, 16, axis=1))
        return v * cos[:, :width] + partner * sin[:, :width]

    att_ref[:, 0:256] = rope(p[:, 0:256], 256) * (float(ATT_HD) ** -0.5)
    att_ref[:, 256:384] = rope(p[:, 256:384], 128)
    att_ref[:, 384:512] = p[:, 384:512]

    p = _dot(h, tail_ref[:, T_BG[0]:T_BG[1]])
    bg_ref[...] = _sigmoid(p).astype(BF16)


def inproj(x, modv, g_mix, w_in_p, lb, w2p, b2p, cos_t, sin_t, n_batch, n_ctx_tiles, moe=None):
    n, d = x.shape
    nt = n // TOK_TILE
    tiles_per_batch = nt // n_batch

    def mod_idx(i):
        b = i // tiles_per_batch
        return b * 2 + ((i % tiles_per_batch) >= n_ctx_tiles).astype(jnp.int32)

    const = lambda i: (0, 0)
    row = lambda i: (i, 0)
    seq = lambda i: (i % tiles_per_batch, 0)
    pre_specs, pre_ops, pre_out_specs, pre_out_shapes, pre_scratch = [], [], [], [], []
    if moe is not None:
        modv_prev, ys, table, slots = moe
        pre_specs = [pl.BlockSpec((1, 1, LANES), lambda i: (i, 0, 0), memory_space=pltpu.SMEM),
                     pl.BlockSpec((1, 1, LANES), lambda i: (jnp.minimum(i + 1, nt - 1), 0, 0),
                                  memory_space=pltpu.SMEM)]
        pre_ops = [table, table]
        pre_out_specs = [pl.BlockSpec((TOK_TILE, d), row)]
        pre_out_shapes = [jax.ShapeDtypeStruct((n, d), F32)]
        pre_scratch = [pltpu.VMEM((2, SORT_ROWS, d), F32), pltpu.SemaphoreType.DMA((2,))]
    mid_specs, mid_ops = [], []
    if moe is not None:
        mid_specs = [pl.BlockSpec((1, N_MOD, d), lambda i: (mod_idx(i), 0, 0)),
                     pl.BlockSpec((TOK_TILE, LANES), row),
                     pl.BlockSpec(memory_space=pl.ANY)]
        mid_ops = [modv_prev, slots, ys]
    outs = pl.pallas_call(
        functools.partial(_inproj_kernel, after_moe=moe is not None),
        grid=(nt,),
        in_specs=pre_specs + [pl.BlockSpec((TOK_TILE, d), row)] + mid_specs + [
                  pl.BlockSpec((1, N_MOD, d), lambda i: (mod_idx(i), 0, 0)),
                  pl.BlockSpec((1, d), const),
                  pl.BlockSpec((d, W_IN_PADDED), const, pipeline_mode=pl.Buffered(1)),
                  pl.BlockSpec((2, HG_W), const),
                  pl.BlockSpec((LANES, 2 * GLA_K), const),
                  pl.BlockSpec((1, 2 * GLA_K), const),
                  pl.BlockSpec((TOK_TILE, 256), seq),
                  pl.BlockSpec((TOK_TILE, 256), seq)],
        out_specs=pre_out_specs + [pl.BlockSpec((TOK_TILE, 256), row),
                                   pl.BlockSpec((TOK_TILE, 1792), row),
                                   pl.BlockSpec((TOK_TILE, 1024), row),
                                   pl.BlockSpec((TOK_TILE, 512), row),
                                   pl.BlockSpec((TOK_TILE, 4096), row)],
        out_shape=pre_out_shapes + [jax.ShapeDtypeStruct((n, 256), F32),
                                    jax.ShapeDtypeStruct((n, 1792), F32),
                                    jax.ShapeDtypeStruct((n, 1024), F32),
                                    jax.ShapeDtypeStruct((n, 512), F32),
                                    jax.ShapeDtypeStruct((n, 4096), BF16)],
        scratch_shapes=[pltpu.VMEM((d, W_IN_PADDED - W_TAIL), BF16)] + pre_scratch,
        compiler_params=_cparams(("arbitrary",)),
        name="combine_inproj" if moe is not None else "inproj",
    )(*pre_ops, x, *mid_ops, modv, g_mix, w_in_p, lb, w2p, b2p, cos_t, sin_t)
    return outs


SCAN_IN = 4
SCAN_SCRATCH = 12


ATTN_IN = 11
ATTN_STRIDE = 3


def _trace_alternately(chains):
    live = list(chains)
    rnd = 0
    while live:
        for entry in list(live):
            gen, stride = entry
            if rnd % stride == 0:
                try:
                    next(gen)
                except StopIteration:
                    live.remove(entry)
        rnd += 1


def _scan_kernel(*refs, attn):
    if attn is not None:
        sink_ref, refs = refs[0], refs[1:]
    n_chain = 2
    n_in = n_chain * SCAN_IN
    ins = refs[:n_in]
    sel = refs[n_in]
    perms = refs[n_in + 1:n_in + 5]
    pos = n_in + 5
    if attn is not None:
        att_in = refs[pos:pos + ATTN_IN]
        pos += ATTN_IN
    outs = refs[pos:pos + n_chain]
    pos += n_chain
    if attn is not None:
        att_out = refs[pos]
        pos += 1
    scratch = refs[pos:]
    chains = []
    for i in range(n_chain):
        rev = i % 2 == 1
        chains.append((_scan_direction(*ins[i * SCAN_IN:(i + 1) * SCAN_IN], sel,
                                       perms[2 * rev], perms[2 * rev + 1], outs[i],
                                       *scratch[i * SCAN_SCRATCH:(i + 1) * SCAN_SCRATCH], rev=rev), 1))
    if attn is not None:
        ctx_blocks, blocks_per_batch = attn
        q_ref, kc_ref, vc_ref = att_in[:3]
        kb, vb = att_in[3:7], att_in[7:11]
        per_tile = q_ref.shape[0] // ATT_BLOCK
        for a in range(per_tile):
            rows = slice(a * ATT_BLOCK, (a + 1) * ATT_BLOCK)

            def write(hk, x, rows=rows):
                att_out[rows, hk * LANES:(hk + 1) * LANES] = x

            gens = _attn_chains(pl.program_id(1) * per_tile + a, sink_ref, q_ref[rows, :],
                                [kc_ref[...]] + [kb[a + t][...] for t in range(3)],
                                [vc_ref[...]] + [vb[a + t][...] for t in range(3)],
                                write, ctx_blocks, blocks_per_batch)
            chains += [(g, ATTN_STRIDE) for g in gens]
    _trace_alternately(chains)


def _scan_direction(q_ref, k_ref, v_ref, f_ref, sel_ref, perm_ref, permt_ref, o_ref,
                    s_ref, qs_ref, ks_ref, vs_ref, fs_ref, qt_ref, kh_ref, term_ref, w_ref, ah_ref, al_ref,
                    kv_ref, *, rev):
    c = SCAN_BLOCK
    tt, kl = q_ref.shape
    nb = tt // c
    assert nb == c

    @pl.when(pl.program_id(1) == 0)
    def _():
        s_ref[...] = jnp.zeros_like(s_ref)

    perm = perm_ref[...]
    vb = v_ref[...].astype(BF16)
    f_hi, f_lo = _split(f_ref[...])
    qs_ref[...] = _dot(perm, q_ref[...].astype(BF16))
    ks_ref[...] = _dot(perm, k_ref[...].astype(BF16))
    vs_ref[...] = _dot(perm, vb)
    fs_ref[...] = _dot(perm, f_hi) + _dot(perm, f_lo)
    yield

    def slab(ref, s):
        return ref[s * nb:(s + 1) * nb, :]

    a = slab(fs_ref, 0)
    qt_ref[0:nb, :] = (slab(qs_ref, 0) * a).astype(BF16)
    for s in range(1, c):
        a = a * slab(fs_ref, s)
        qt_ref[s * nb:(s + 1) * nb, :] = (slab(qs_ref, s) * a).astype(BF16)
    gam_t = a.T
    g = jnp.ones((nb, kl), F32)
    for s in range(c - 1, -1, -1):
        kh_ref[s * nb:(s + 1) * nb, :] = (slab(ks_ref, s) * g).astype(BF16)
        g = g * slab(fs_ref, s)
    yield

    heads = 256 // 64
    dk = kl // heads
    perm_t = permt_ref[...]
    qt = _dot(perm_t, qt_ref[...]).astype(BF16)
    kh = _dot(perm_t, kh_ref[...]).astype(BF16)

    def head_lanes(x, h):
        return x[:, :dk] if h == 0 else pltpu.roll(x, kl - dk * h, axis=1)[:, :dk]

    q_heads = [head_lanes(qt, h) for h in range(heads)]
    k_heads = [head_lanes(kh, h) for h in range(heads)]
    v_head = lax.broadcasted_iota(jnp.int32, (c, 256), 1) // 64
    g_head = lax.broadcasted_iota(jnp.int32, (dk, 256), 1) // 64
    for j in range(nb):
        lo = j * c
        k4 = jnp.concatenate([k_heads[h][lo:lo + c, :] for h in range(heads)], axis=0)
        v4 = jnp.concatenate([jnp.where(v_head == h, vb[lo:lo + c, :], jnp.zeros_like(vb[lo:lo + c, :]))
                              for h in range(heads)], axis=0)
        kv_ref[j] = lax.dot_general(k4, v4, (((0,), (0,)), ((), ())), preferred_element_type=F32)
        yield

    n_pair = 0
    for s in range(c):
        p = slab(qs_ref, s)
        for d in range(s + 1):
            if d > 0:
                p = p * slab(fs_ref, s - d + 1)
            term_ref[n_pair * nb:(n_pair + 1) * nb, :] = (p * slab(ks_ref, s - d)).astype(BF16)
            n_pair += 1
        yield
    w_ref[...] = _dot(term_ref[...], sel_ref[...])
    yield
    n_pair = 0
    for s in range(c):
        acc = jnp.zeros((nb, 256), F32)
        for d in range(s + 1):
            acc = acc + w_ref[n_pair * nb:(n_pair + 1) * nb, :] * slab(vs_ref, s - d)
            n_pair += 1
        hi, lo = _split(acc)
        ah_ref[s * nb:(s + 1) * nb, :] = hi
        al_ref[s * nb:(s + 1) * nb, :] = lo
        yield

    o_ref[...] = _dot(perm_t, ah_ref[...]) + _dot(perm_t, al_ref[...])
    yield
    st = s_ref[...]
    for jj in range(nb):
        j = nb - 1 - jj if rev else jj
        lo = j * c
        q4 = jnp.concatenate([q_heads[h][lo:lo + c, :] for h in range(heads)], axis=0)
        o4 = _dot(q4, st.astype(BF16))
        inter = o4[0:c, :]
        for h in range(1, heads):
            inter = jnp.where(v_head == h, o4[h * c:(h + 1) * c, :], inter)
        o_ref[lo:lo + c, :] += inter
        gam = jnp.broadcast_to(gam_t[0:dk, j:j + 1], (dk, 256))
        for h in range(1, heads):
            gam = jnp.where(g_head == h, gam_t[h * dk:(h + 1) * dk, j:j + 1], gam)
        st = gam * st + kv_ref[j]
        yield
    s_ref[...] = st


def gated_scan(pack, cols_fwd, cols_rev, kl, n_batch, n_ctx_tiles, name, attn=None):
    n = pack.shape[0]
    nt = n // TOK_TILE
    tpb = nt // n_batch

    def rev_tile(i):
        return jnp.where(i < n_ctx_tiles, n_ctx_tiles - 1 - i, tpb - 1 - (i - n_ctx_tiles))

    heads = 4
    c = SCAN_BLOCK
    nb = TOK_TILE // c
    perms = []
    for rev in (False, True):
        perm = np.zeros((TOK_TILE, TOK_TILE), np.float32)
        for j in range(nb):
            for s in range(c):
                perm[s * nb + j, j * c + (c - 1 - s if rev else s)] = 1.0
        perms += [jnp.asarray(perm, BF16), jnp.asarray(perm.T, BF16)]
    n_pairs = c * (c + 1) // 2
    fwd = lambda col: (lambda b, i, *_: (b * tpb + i, col))
    bwd = lambda col: (lambda b, i, *_: (b * tpb + rev_tile(i), col))
    const = lambda b, i, *_: (0, 0)

    def direction_specs(rows, cols, kl):
        qc, kc, vc, fc = cols
        specs = [pl.BlockSpec((TOK_TILE, kl), rows(qc)),
                 pl.BlockSpec((TOK_TILE, kl), rows(kc)),
                 pl.BlockSpec((TOK_TILE, 256), rows(vc)),
                 pl.BlockSpec((TOK_TILE, kl), rows(fc))]
        assert len(specs) == SCAN_IN
        return specs

    def chain_scratch(kl):
        dk = kl // heads
        shapes = [pltpu.VMEM((dk, 256), F32),
                  pltpu.VMEM((TOK_TILE, kl), F32),
                  pltpu.VMEM((TOK_TILE, kl), F32),
                  pltpu.VMEM((TOK_TILE, 256), F32),
                  pltpu.VMEM((TOK_TILE, kl), F32),
                  pltpu.VMEM((TOK_TILE, kl), BF16),
                  pltpu.VMEM((TOK_TILE, kl), BF16),
                  pltpu.VMEM((n_pairs * nb, kl), BF16),
                  pltpu.VMEM((n_pairs * nb, 256), F32),
                  pltpu.VMEM((TOK_TILE, 256), BF16),
                  pltpu.VMEM((TOK_TILE, 256), BF16),
                  pltpu.VMEM((nb, dk, 256), F32)]
        assert len(shapes) == SCAN_SCRATCH
        return shapes

    dk = kl // heads
    sel = jnp.asarray(np.arange(kl)[:, None] // dk == np.arange(256)[None, :] // 64, BF16)
    in_specs = direction_specs(fwd, cols_fwd, kl) + direction_specs(bwd, cols_rev, kl)
    in_specs += [pl.BlockSpec(sel.shape, const)] + [pl.BlockSpec((TOK_TILE, TOK_TILE), const)] * 4
    operands = [pack] * (2 * SCAN_IN) + [sel] + perms
    out_specs = [pl.BlockSpec((TOK_TILE, 256), fwd(0)), pl.BlockSpec((TOK_TILE, 256), bwd(0))]
    prefetch, kernel_attn = [], None
    if attn is not None:
        att_pack, sink, ctx_len = attn
        per_tile = TOK_TILE // ATT_BLOCK
        bpb = tpb * per_tile
        cb = ctx_len // ATT_BLOCK

        def band(off, col):
            def idx(b, i, *_):
                return (b * bpb + jnp.clip(i * per_tile + off, cb, bpb - 1), col)
            return pl.BlockSpec((ATT_BLOCK, LANES), idx)

        def ctx(col):
            return pl.BlockSpec((ctx_len, LANES), lambda b, i, *_: (b * (bpb * ATT_BLOCK // ctx_len), col))

        att_specs = ([pl.BlockSpec((TOK_TILE, 256), fwd(0)), ctx(2), ctx(3)]
                     + [band(off, 2) for off in range(-1, per_tile + 1)]
                     + [band(off, 3) for off in range(-1, per_tile + 1)])
        assert len(att_specs) == ATTN_IN
        in_specs += att_specs
        operands += [att_pack] * ATTN_IN
        out_specs.append(pl.BlockSpec((TOK_TILE, 256), fwd(0)))
        prefetch, kernel_attn = [sink], (cb, bpb)
    grid_spec = pltpu.PrefetchScalarGridSpec(
        num_scalar_prefetch=len(prefetch),
        grid=(n_batch, tpb),
        in_specs=in_specs,
        out_specs=out_specs,
        scratch_shapes=chain_scratch(kl) + chain_scratch(kl),
    )
    return pl.pallas_call(
        functools.partial(_scan_kernel, attn=kernel_attn),
        grid_spec=grid_spec,
        out_shape=[jax.ShapeDtypeStruct((n, 256), F32)] * len(out_specs),
        compiler_params=_cparams(("parallel", "arbitrary")),
        name=name,
    )(*prefetch, *operands)


def _attn_chains(n, sink_ref, q, keys, vals, write, ctx_blocks, blocks_per_batch):
    blk = ATT_BLOCK
    lane = lax.broadcasted_iota(jnp.int32, (blk, LANES), 1)
    low = lane < ATT_HD
    qi = lax.broadcasted_iota(jnp.int32, (blk, blk), 0)
    ki = lax.broadcasted_iota(jnp.int32, (blk, blk), 1)
    is_lat = n >= ctx_blocks
    band_ok = [
        jnp.logical_and(jnp.logical_and(is_lat, n - 1 >= ctx_blocks), qi <= ki),
        jnp.logical_and(is_lat, qi >= 0),
        jnp.logical_and(jnp.logical_and(is_lat, n + 1 < blocks_per_batch), ki <= qi),
    ]

    def dup(x, hk):
        xr = pltpu.roll(x, ATT_HD, axis=1)
        lo_x = lax.broadcasted_iota(jnp.int32, x.shape, 1) < ATT_HD
        return (jnp.where(lo_x, x, xr) if hk == 0 else jnp.where(lo_x, xr, x)).astype(BF16)

    nt_dims = (((1,), (1,)), ((), ()))

    def kv_group(hk):
        kd = [dup(x, hk) for x in keys]
        vd = [dup(x, hk) for x in vals]
        qh = q[:, hk * LANES:(hk + 1) * LANES]
        q2 = jnp.concatenate([jnp.where(low, qh, 0.0), jnp.where(low, 0.0, qh)], axis=0).astype(BF16)
        yield
        s = [lax.dot_general(q2, kx, nt_dims, preferred_element_type=F32) for kx in kd]
        for t in range(3):
            s[t + 1] = jnp.where(jnp.concatenate([band_ok[t]] * 2, axis=0), s[t + 1], NEG)
        yield
        sink = jnp.where(lax.broadcasted_iota(jnp.int32, (2 * blk, 1), 0) < blk,
                         sink_ref[hk * 2], sink_ref[hk * 2 + 1])
        pieces = [s[0][:, :LANES], s[0][:, LANES:], s[1], s[2], s[3]]
        top = pieces[0]
        for piece in pieces[1:]:
            top = jnp.maximum(top, piece)
        m = jnp.maximum(sink, top.max(axis=-1, keepdims=True))
        yield
        e = [jnp.exp(x - m) for x in s]
        tot = e[0][:, :LANES] + e[0][:, LANES:] + e[1] + e[2] + e[3]
        den = jnp.exp(sink - m) + tot.sum(axis=-1, keepdims=True)
        yield
        o = _dot(e[0].astype(BF16), vd[0])
        for t in range(1, 4):
            o = o + _dot(e[t].astype(BF16), vd[t])
        yield
        o = o / den
        write(hk, jnp.where(low, o[:blk, :], o[blk:, :]))

    return [kv_group(hk) for hk in range(ATT_KV_HEADS)]


def _head_norm(o, ones_bd, g):
    sq = o * o
    hi, lo = _split(sq)
    ms = (_dot(hi, ones_bd) + _dot(lo, ones_bd)) * (1.0 / 64.0)
    return o * lax.rsqrt(ms + EPS) * g


def _route_select(aff, bias):
    lane = lax.broadcasted_iota(jnp.int32, aff.shape, 1)
    epg = N_EXPERTS // N_GROUPS
    pos = lane & (epg - 1)
    v = aff + bias

    def nxt(x, o):
        return pltpu.roll(x, LANES - o, axis=1)

    def prv(x, o):
        return pltpu.roll(x, o, axis=1)

    beaten = jnp.zeros(aff.shape, jnp.int32)
    for o in range(1, epg):
        beaten = beaten + jnp.where(jnp.logical_and(pos + o < epg, nxt(v, o) > v), 1, 0)
        beaten = beaten + jnp.where(jnp.logical_and(pos >= o, prv(v, o) >= v), 1, 0)
    top2 = beaten < TOP_K
    t = jnp.where(top2, v, 0.0)
    score = t
    for o in range(1, epg):
        score = score + jnp.where(pos + o < epg, nxt(t, o), 0.0) + jnp.where(pos >= o, prv(t, o), 0.0)
    worse = jnp.zeros(aff.shape, jnp.int32)
    for o in range(epg, N_EXPERTS, epg):
        worse = worse + jnp.where(jnp.logical_and(lane + o < N_EXPERTS, nxt(score, o) > score), 1, 0)
        worse = worse + jnp.where(jnp.logical_and(lane >= o, prv(score, o) >= score), 1, 0)
    sel = jnp.logical_and(jnp.logical_and(lane < N_EXPERTS, worse == 0), top2)
    picked = jnp.where(sel, aff, 0.0)
    gate = picked / jnp.sum(picked, axis=-1, keepdims=True)
    return gate, sel


def _merge_kernel(x_ref, mod_ref, zp_ref, z_ref, zn_ref, hgf_ref, hgb_ref, hgg_ref, glf_ref, glb_ref,
                  glg_ref, att_ref, bg_ref, cw_ref, cb_ref, lng_ref, lnb_ref, hgn_ref, gln_ref,
                  ones_ref, wbc_ref, wbh_ref, wbg_ref, wba_ref, wo_ref, gf_ref, rw_ref, rb_ref, tri_ref,
                  upper_ref, xo_ref, h2_ref, slot_ref, gran_ref, zs_ref, zsh_ref, *, tiles_per_batch,
                  n_ctx_tiles):
    i = pl.program_id(0)
    tt = x_ref.shape[0]
    ti = i % tiles_per_batch
    has_prev = jnp.logical_and(ti != 0, ti != n_ctx_tiles)
    has_next = jnp.logical_and(ti != n_ctx_tiles - 1, ti != tiles_per_batch - 1)

    zs_ref[0:HALO, :] = jnp.where(has_prev, zp_ref[...], 0.0)
    zs_ref[HALO:HALO + tt, :] = z_ref[...]
    zs_ref[HALO + tt:HALO + tt + HALO, :] = jnp.where(has_next, zn_ref[...], 0.0)
    span = tt + 2 * HALO - 8
    for b in range(8):
        zsh_ref[b] = zs_ref[b:b + span, :]
    acc = jnp.zeros((tt, CONV_CH), F32) + cb_ref[...]
    for j in range(CONV_K):
        off = HALO - CONV_K // 2 + j
        acc = acc + zsh_ref[off % 8, off - off % 8:off - off % 8 + tt, :] * cw_ref[j:j + 1, :]
    mu = jnp.mean(acc, axis=-1, keepdims=True)
    cen = acc - mu
    var = jnp.mean(cen * cen, axis=-1, keepdims=True)
    conv_y = _silu(cen * lax.rsqrt(var + EPS) * lng_ref[...] + lnb_ref[...])

    ones_bd = ones_ref[...]
    hg_y = _head_norm(hgf_ref[...] + hgb_ref[...], ones_bd, hgn_ref[...]) * hgg_ref[...]
    gla_y = _head_norm(glf_ref[...] + glb_ref[...], ones_bd, gln_ref[...]) * glg_ref[...]

    d = x_ref.shape[1]
    merged = bg_ref[:, 0:d].astype(F32) * _dot(conv_y.astype(BF16), wbc_ref[...])
    merged = merged + bg_ref[:, d:2 * d].astype(F32) * _dot(hg_y.astype(BF16), wbh_ref[...])
    merged = merged + bg_ref[:, 2 * d:3 * d].astype(F32) * _dot(gla_y.astype(BF16), wbg_ref[...])
    merged = merged + bg_ref[:, 3 * d:4 * d].astype(F32) * _dot(att_ref[...].astype(BF16), wba_ref[...])
    mix = _dot(merged.astype(BF16), wo_ref[...])

    x_new = x_ref[...] + mod_ref[0, 2:3, :] * mix
    xo_ref[...] = x_new
    h2 = _modnorm(x_new, gf_ref[...], mod_ref[0, 3:4, :], mod_ref[0, 4:5, :])
    h2_ref[...] = h2
    gate, sel = _route_select(_sigmoid(_dot_f32(h2, rw_ref[...])), rb_ref[...])

    incl = _dot(tri_ref[...], sel.astype(F32).astype(BF16))
    gran = jnp.floor((incl[tt - 1:tt, :] + (GRANULE - 1.0)) * (1.0 / GRANULE))
    gran8 = jnp.broadcast_to(gran, (8, LANES))
    start = _dot(gran8.astype(BF16), upper_ref[...])[0:1, :]
    pos = GRANULE * start + incl - 1.0
    p0 = jnp.min(jnp.where(sel, pos, 1e9), axis=-1, keepdims=True)
    p1 = jnp.max(jnp.where(sel, pos, -1.0), axis=-1, keepdims=True)
    g0 = jnp.sum(jnp.where(jnp.logical_and(sel, pos == p0), gate, 0.0), axis=-1, keepdims=True)
    g1 = jnp.sum(jnp.where(jnp.logical_and(sel, pos == p1), gate, 0.0), axis=-1, keepdims=True)
    lane = lax.broadcasted_iota(jnp.int32, (tt, LANES), 1)
    slot_ref[...] = jnp.where(lane == 0, p0, jnp.where(lane == 1, p1, jnp.where(lane == 2, g0,
                              jnp.where(lane == 3, g1, 0.0))))
    gran_ref[0] = gran8.astype(jnp.int32)


def merge(x, modv, conv_z, hg_pack, hg_f, hg_b, gla_pack, gla_f, gla_b, att_o, bgate, lw, n_batch,
          n_ctx_tiles):
    n, d = x.shape
    nt = n // TOK_TILE
    tpb = nt // n_batch
    hpt = TOK_TILE // HALO
    n_halo = n // HALO

    def mod_idx(i):
        return (i // tpb) * 2 + ((i % tpb) >= n_ctx_tiles).astype(jnp.int32)

    row = lambda i: (i, 0)
    const = lambda i: (0, 0)
    col = lambda c: (lambda i: (i, c))
    full = lambda a: pl.BlockSpec(a.shape, const)
    weights = [lw["conv_w"], lw["conv_b"], lw["conv_ln_g"], lw["conv_ln_b"], lw["hg_norm_g"],
               lw["gla_norm_g"], lw["ones_bd"], lw["w_br_conv"], lw["w_br_hg"], lw["w_br_gla"],
               lw["w_br_att"], lw["w_out"], lw["g_ffn"], lw["router_w"], lw["router_b"], lw["tri"],
               lw["upper"]]
    return pl.pallas_call(
        functools.partial(_merge_kernel, tiles_per_batch=tpb, n_ctx_tiles=n_ctx_tiles),
        grid=(nt,),
        in_specs=[pl.BlockSpec((TOK_TILE, d), row),
                  pl.BlockSpec((1, N_MOD, d), lambda i: (mod_idx(i), 0, 0)),
                  pl.BlockSpec((HALO, CONV_CH), lambda i: (jnp.maximum(i * hpt - 1, 0), 0)),
                  pl.BlockSpec((TOK_TILE, CONV_CH), row),
                  pl.BlockSpec((HALO, CONV_CH), lambda i: (jnp.minimum((i + 1) * hpt, n_halo - 1), 0)),
                  pl.BlockSpec((TOK_TILE, 256), row),
                  pl.BlockSpec((TOK_TILE, 256), row),
                  pl.BlockSpec((TOK_TILE, 256), col(6)),
                  pl.BlockSpec((TOK_TILE, 256), row),
                  pl.BlockSpec((TOK_TILE, 256), row),
                  pl.BlockSpec((TOK_TILE, 256), col(3)),
                  pl.BlockSpec((TOK_TILE, 256), row),
                  pl.BlockSpec((TOK_TILE, 4 * d), row)] + [full(w) for w in weights],
        out_specs=[pl.BlockSpec((TOK_TILE, d), row),
                   pl.BlockSpec((TOK_TILE, d), row),
                   pl.BlockSpec((TOK_TILE, LANES), row),
                   pl.BlockSpec((1, 8, LANES), lambda i: (i, 0, 0))],
        out_shape=[jax.ShapeDtypeStruct((n, d), F32),
                   jax.ShapeDtypeStruct((n, d), F32),
                   jax.ShapeDtypeStruct((n, LANES), F32),
                   jax.ShapeDtypeStruct((nt, 8, LANES), jnp.int32)],
        scratch_shapes=[pltpu.VMEM((TOK_TILE + 2 * HALO, CONV_CH), F32),
                        pltpu.VMEM((8, TOK_TILE + 2 * HALO - 8, CONV_CH), F32)],
        compiler_params=_cparams(("parallel",)),
        name="merge",
    )(x, modv, conv_z, conv_z, conv_z, hg_f, hg_b, hg_pack, gla_f, gla_b, gla_pack, att_o, bgate,
      *weights)


def _granule_copy(src, src_row, dst, dst_row, sem, granules=1):
    rows = granules * GRANULE
    return pltpu.make_async_copy(src.at[pl.ds(pl.multiple_of(src_row, GRANULE), rows)],
                                 dst.at[pl.ds(pl.multiple_of(dst_row, GRANULE), rows)], sem)


RUN_CHUNKS = (8, 4, 2, 1)
MAX_RUN = TOK_TILE // GRANULE


def _for_each_run_chunk(tab_ref, fn):
    big = RUN_CHUNKS[0]
    for e in range(N_EXPERTS):
        local = tab_ref[0, 0, e]
        glob = tab_ref[0, 0, N_EXPERTS + e]
        n = tab_ref[0, 0, 2 * N_EXPERTS + e]
        for k in range(MAX_RUN // big):
            @pl.when(n >= big * (k + 1))
            def _(k=k):
                fn(local + big * k, glob + big * k, big)
        off = n - n % big
        for size in RUN_CHUNKS[1:]:
            @pl.when((n & size) != 0)
            def _(off=off, size=size):
                fn(local + off, glob + off, size)
            off = off + (n & size)


def _slot_matrix(slot_ref, width, weighted):
    tt = slot_ref.shape[0]
    col = lax.broadcasted_iota(jnp.int32, (tt, width), 1).astype(F32)
    hit0 = col == slot_ref[:, 0:1]
    hit1 = col == slot_ref[:, 1:2]
    if not weighted:
        return jnp.where(jnp.logical_or(hit0, hit1), 1.0, 0.0)
    return jnp.where(hit0, slot_ref[:, 2:3], 0.0) + jnp.where(hit1, slot_ref[:, 3:4], 0.0)


def _dispatch_kernel(tab_ref, tabp_ref, tail_ref, h_ref, slot_ref, xs_hbm, buf, zbuf, sem, zsem, bsem):
    i = pl.program_id(0)
    nb = pl.num_programs(0)
    slot = i % 2

    perm = _slot_matrix(slot_ref, buf.shape[1], weighted=False).astype(BF16)
    buf[slot] = lax.dot_general(perm, h_ref[...].astype(BF16), (((0,), (0,)), ((), ())),
                                preferred_element_type=F32)

    def chunk(s):
        def copy(local, glob, granules):
            return _granule_copy(buf.at[s], local * GRANULE, xs_hbm, glob * GRANULE, sem.at[s], granules)
        return copy

    _for_each_run_chunk(tab_ref, lambda *a: chunk(slot)(*a).start())

    @pl.when(i > 0)
    def _():
        _for_each_run_chunk(tabp_ref, lambda *a: chunk(1 - slot)(*a).wait())

    @pl.when(i == nb - 1)
    def _():
        _for_each_run_chunk(tab_ref, lambda *a: chunk(slot)(*a).wait())

    @pl.when(i == 0)
    def _():
        zbuf[...] = jnp.zeros_like(zbuf)
        n_used = tail_ref[0, 0, 2 * N_EXPERTS]
        n_blocks = xs_hbm.shape[0] // MOE_BLOCK

        def block_copy(b):
            return pltpu.make_async_copy(
                zbuf, xs_hbm.at[pl.ds(pl.multiple_of(b * MOE_BLOCK, MOE_BLOCK), MOE_BLOCK)], bsem)

        for e in range(N_EXPERTS):
            def fill(m, carry, e=e):
                _granule_copy(zbuf, 0, xs_hbm, (tail_ref[0, 0, e] + m) * GRANULE, zsem).start()
                return carry
            lax.fori_loop(0, tail_ref[0, 0, N_EXPERTS + e], fill, 0)

        def fill_block(b, carry):
            block_copy(b).start()
            return carry
        lax.fori_loop(n_used, n_blocks, fill_block, 0)

        for e in range(N_EXPERTS):
            def done(m, carry):
                _granule_copy(zbuf, 0, xs_hbm, 0, zsem).wait()
                return carry
            lax.fori_loop(0, tail_ref[0, 0, N_EXPERTS + e], done, 0)

        def done_block(b, carry):
            block_copy(b).wait()
            return carry
        lax.fori_loop(n_used, n_blocks, done_block, 0)


def dispatch(h2, slots, table, tails, n_rows):
    n, d = h2.shape
    nt = n // TOK_TILE
    return pl.pallas_call(
        _dispatch_kernel,
        grid=(nt,),
        in_specs=[pl.BlockSpec((1, 1, LANES), lambda i: (i, 0, 0), memory_space=pltpu.SMEM),
                  pl.BlockSpec((1, 1, LANES), lambda i: (jnp.maximum(i - 1, 0), 0, 0), memory_space=pltpu.SMEM),
                  pl.BlockSpec((1, 1, LANES), lambda i: (0, 0, 0), memory_space=pltpu.SMEM),
                  pl.BlockSpec((TOK_TILE, d), lambda i: (i, 0)),
                  pl.BlockSpec((TOK_TILE, LANES), lambda i: (i, 0))],
        out_specs=pl.BlockSpec(memory_space=pl.ANY),
        out_shape=jax.ShapeDtypeStruct((n_rows, d), F32),
        scratch_shapes=[pltpu.VMEM((2, SORT_ROWS, d), F32),
                        pltpu.VMEM((MOE_BLOCK, d), F32),
                        pltpu.SemaphoreType.DMA((2,)),
                        pltpu.SemaphoreType.DMA(()),
                        pltpu.SemaphoreType.DMA(())],
        compiler_params=_cparams(("arbitrary",)),
        name="dispatch",
    )(table, table, tails, h2, slots)


def _expert_kernel(be_ref, nu_ref, x_ref, w1_ref, w3_ref, w2_ref, y_ref, w1b, w3b, w2b):
    i = pl.program_id(0)
    n_used = nu_ref[0]

    @pl.when(i < n_used)
    def _():
        first = jnp.logical_or(i == 0, be_ref[i] != be_ref[jnp.maximum(i - 1, 0)])

        @pl.when(first)
        def _():
            w1b[...] = w1_ref[0].astype(BF16)
            w3b[...] = w3_ref[0].astype(BF16)
            w2b[...] = w2_ref[0].astype(BF16)

        x = x_ref[...].astype(BF16)
        hid = _silu(_dot(x, w1b[...])) * _dot(x, w3b[...])
        y_ref[...] = _dot(hid.astype(BF16), w2b[...])

    @pl.when(i >= n_used)
    def _():
        y_ref[...] = jnp.zeros_like(y_ref)


def experts(xs, block_expert, n_used, w1, w3, w2, layer):
    n_rows, d = xs.shape
    n_blocks = n_rows // MOE_BLOCK
    ff = w1.shape[-1]
    grid_spec = pltpu.PrefetchScalarGridSpec(
        num_scalar_prefetch=2,
        grid=(n_blocks,),
        in_specs=[pl.BlockSpec((MOE_BLOCK, d), lambda i, be, nu: (jnp.minimum(i, nu[0] - 1), 0)),
                  pl.BlockSpec((None, 1, d, ff), lambda i, be, nu: (layer, be[i], 0, 0)),
                  pl.BlockSpec((None, 1, d, ff), lambda i, be, nu: (layer, be[i], 0, 0)),
                  pl.BlockSpec((None, 1, ff, d), lambda i, be, nu: (layer, be[i], 0, 0))],
        out_specs=pl.BlockSpec((MOE_BLOCK, d), lambda i, be, nu: (i, 0)),
        scratch_shapes=[pltpu.VMEM((d, ff), BF16),
                        pltpu.VMEM((d, ff), BF16),
                        pltpu.VMEM((ff, d), BF16)],
    )
    return pl.pallas_call(
        _expert_kernel,
        grid_spec=grid_spec,
        out_shape=jax.ShapeDtypeStruct((n_rows, d), F32),
        compiler_params=_cparams(("arbitrary",)),
        name="experts",
    )(block_expert, n_used, xs, w1, w3, w2)


def _combine_kernel(d_ref, dn_ref, x_ref, mod_ref, g_ref, slot_ref, y_hbm, o_ref, ybuf, sem, *, final):
    y = _expert_mix(d_ref, dn_ref, slot_ref, y_hbm, ybuf, sem)
    x = x_ref[...] + mod_ref[0, 5:6, :] * y
    if final:
        x = x * lax.rsqrt(jnp.mean(x * x, axis=-1, keepdims=True) + EPS) * g_ref[...]
    o_ref[...] = x


def _expert_mix(d_ref, dn_ref, slot_ref, y_hbm, ybuf, sem):
    i = pl.program_id(0)
    nb = pl.num_programs(0)
    slot = i % 2

    def chunk(s):
        def copy(local, glob, granules):
            return _granule_copy(y_hbm, glob * GRANULE, ybuf.at[s], local * GRANULE, sem.at[s], granules)
        return copy

    @pl.when(i == 0)
    def _():
        ybuf[...] = jnp.zeros_like(ybuf)
        _for_each_run_chunk(d_ref, lambda *a: chunk(0)(*a).start())

    @pl.when(i + 1 < nb)
    def _():
        _for_each_run_chunk(dn_ref, lambda *a: chunk(1 - slot)(*a).start())

    _for_each_run_chunk(d_ref, lambda *a: chunk(slot)(*a).wait())

    hi, lo = _split(_slot_matrix(slot_ref, ybuf.shape[1], weighted=True))
    yb = ybuf[slot].astype(BF16)
    return _dot(hi, yb) + _dot(lo, yb)


def combine(x, modv, ys, table, slots, final_g, n_batch, n_ctx_tiles, final):
    n, d = x.shape
    nt = n // TOK_TILE
    tpb = nt // n_batch
    if final:
        lat = tpb - n_ctx_tiles
        steps = n_batch * lat
        tile = lambda s: (s // lat) * tpb + n_ctx_tiles + s % lat
        out_rows = steps * TOK_TILE
    else:
        steps = nt
        tile = lambda s: s
        out_rows = n

    def mod_idx(s):
        t = tile(s)
        return (t // tpb) * 2 + ((t % tpb) >= n_ctx_tiles).astype(jnp.int32)

    return pl.pallas_call(
        functools.partial(_combine_kernel, final=final),
        grid=(steps,),
        in_specs=[pl.BlockSpec((1, 1, LANES), lambda s: (tile(s), 0, 0), memory_space=pltpu.SMEM),
                  pl.BlockSpec((1, 1, LANES), lambda s: (tile(jnp.minimum(s + 1, steps - 1)), 0, 0),
                               memory_space=pltpu.SMEM),
                  pl.BlockSpec((TOK_TILE, d), lambda s: (tile(s), 0)),
                  pl.BlockSpec((1, N_MOD, d), lambda s: (mod_idx(s), 0, 0)),
                  pl.BlockSpec((1, d), lambda s: (0, 0)),
                  pl.BlockSpec((TOK_TILE, LANES), lambda s: (tile(s), 0)),
                  pl.BlockSpec(memory_space=pl.ANY)],
        out_specs=pl.BlockSpec((TOK_TILE, d), lambda s: (s, 0)),
        out_shape=jax.ShapeDtypeStruct((out_rows, d), F32),
        scratch_shapes=[pltpu.VMEM((2, SORT_ROWS, d), F32),
                        pltpu.SemaphoreType.DMA((2,))],
        compiler_params=_cparams(("arbitrary",)),
        name="combine_final" if final else "combine",
    )(table, table, x, modv, final_g, slots, ys)


def sorted_rows_bound(n_tok):
    nt = n_tok // TOK_TILE
    rows = n_tok * TOP_K + nt * N_EXPERTS * (GRANULE - 1) + N_EXPERTS * (MOE_BLOCK - GRANULE)
    return -(-rows // MOE_BLOCK) * MOE_BLOCK


def moe_tables(gran, n_rows):
    nt = gran.shape[0]
    per_blk = MOE_BLOCK // GRANULE
    local = jnp.cumsum(gran, axis=1) - gran
    before = jnp.cumsum(gran, axis=0) - gran
    total = jnp.sum(gran, axis=0)
    padded = (total + per_blk - 1) // per_blk * per_blk
    region_end = jnp.cumsum(padded)
    region = region_end - padded
    table = jnp.concatenate([local, region[None, :] + before, gran,
                             jnp.zeros((nt, LANES - 3 * N_EXPERTS), gran.dtype)], axis=1)
    table = table.astype(jnp.int32).reshape(nt, 1, LANES)
    tails = jnp.zeros((LANES,), jnp.int32).at[:N_EXPERTS].set(region + total)
    tails = tails.at[N_EXPERTS:2 * N_EXPERTS].set(padded - total)
    n_blocks = n_rows // MOE_BLOCK
    blk = jnp.arange(n_blocks, dtype=jnp.int32) * per_blk
    block_expert = jnp.minimum(jnp.sum(blk[:, None] >= region_end[None, :], axis=1), N_EXPERTS - 1)
    n_used = (region_end[-1] // per_blk).astype(jnp.int32).reshape(1)
    tails = tails.at[2 * N_EXPERTS].set(n_used[0]).reshape(1, 1, LANES)
    return table, tails, block_expert.astype(jnp.int32), n_used


def _rope_tables(ctx_len, n_lat):
    rows = n_lat // GRID_W
    row = np.repeat(np.arange(rows, dtype=np.float32), GRID_W)
    col = np.tile(np.arange(GRID_W, dtype=np.float32), rows)
    half = ATT_HD // 4
    inv = jnp.asarray(ROPE_BASE, F32) ** (-jnp.arange(half, dtype=F32) / half)
    ang_r = jnp.asarray(row)[:, None] * inv
    ang_c = jnp.asarray(col)[:, None] * inv
    cos64 = jnp.concatenate([jnp.cos(ang_r)] * 2 + [jnp.cos(ang_c)] * 2, axis=1)
    sin64 = jnp.concatenate([-jnp.sin(ang_r), jnp.sin(ang_r), -jnp.sin(ang_c), jnp.sin(ang_c)], axis=1)
    cos64 = jnp.concatenate([jnp.ones((ctx_len, ATT_HD), F32), cos64], axis=0)
    sin64 = jnp.concatenate([jnp.zeros((ctx_len, ATT_HD), F32), sin64], axis=0)
    return jnp.tile(cos64, (1, ATT_HEADS)), jnp.tile(sin64, (1, ATT_HEADS))


def _pack_w_in(w):
    assert w.shape[-1] == W_IN_COLS
    return jnp.pad(w.astype(BF16), ((0, 0), (0, W_IN_PADDED - W_IN_COLS)))


def kernel(x, c, ctx, c_ctx, hg_lb_logits, router_w, router_b, final_g, w_mod, b_mod, g_mix, g_ffn,
           w_in, conv_w, conv_b, conv_ln_g, conv_ln_b, hg_norm_g, gla_w2, gla_b2, gla_norm_g, att_sink,
           w_br_conv, w_br_hg, w_br_gla, w_br_att, w_out, moe_w1, moe_w3, moe_w2):
    n_batch, n_lat, d = x.shape
    ctx_len = ctx.shape[1]
    depth = w_in.shape[0]
    assert ctx_len % TOK_TILE == 0 and n_lat % TOK_TILE == 0 and n_lat % GRID_W == 0
    seq = ctx_len + n_lat
    n_ctx_tiles = ctx_len // TOK_TILE

    xs = jnp.concatenate([ctx, x], axis=1).reshape(n_batch * seq, d)

    c_rows = jnp.zeros((8, d), F32).at[:n_batch].set(c).at[n_batch].set(c_ctx)
    mods = modulation(c_rows, w_mod, b_mod).reshape(depth, 8, N_MOD, d)
    modv = jnp.stack([mods[:, n_batch] if j % 2 == 0 else mods[:, j // 2] for j in range(2 * n_batch)], axis=1)

    lb_sm = jax.nn.softmax(hg_lb_logits.astype(F32), axis=0)
    lower = jnp.cumsum(lb_sm, axis=0) - lb_sm[0]
    cos_t, sin_t = _rope_tables(ctx_len, n_lat)
    ones_bd = jnp.asarray(np.arange(256)[:, None] // 64 == np.arange(256)[None, :] // 64, BF16)
    rw = jnp.zeros((d, LANES), F32).at[:, :N_EXPERTS].set(router_w.astype(F32))
    rb = jnp.zeros((1, LANES), F32).at[0, :N_EXPERTS].set(router_b.astype(F32))
    tri = jnp.asarray(np.tril(np.ones((TOK_TILE, TOK_TILE))), BF16)
    upper = jnp.asarray(np.triu(np.ones((LANES, LANES)), k=1), BF16)

    pending = None
    for l in range(depth):
        w2p = jnp.zeros((LANES, 2 * GLA_K), F32)
        w2p = w2p.at[0:GLA_RANK, 0:GLA_K].set(gla_w2[l, 0]).at[GLA_RANK:2 * GLA_RANK, GLA_K:].set(gla_w2[l, 1])
        b2p = gla_b2[l].reshape(1, 2 * GLA_K)
        proj = inproj(xs, modv[l], g_mix[l].reshape(1, d), _pack_w_in(w_in[l]), lower[l], w2p, b2p, cos_t, sin_t,
                      n_batch, n_ctx_tiles, moe=pending)
        if pending is not None:
            xs, proj = proj[0], proj[1:]
        conv_z, hg_pack, gla_pack, att_pack, bgate = proj

        hg_f, hg_b = gated_scan(hg_pack, (0, 1, 3, 4), (0, 2, 3, 5), HG_W, n_batch, n_ctx_tiles, "scan_hgrn")
        gla_f, gla_b, att_o = gated_scan(gla_pack, (0, 1, 2, 2), (0, 1, 2, 3), GLA_K, n_batch, n_ctx_tiles,
                                         "scan_gla_attn", attn=(att_pack, att_sink[l].astype(F32), ctx_len))

        lw = dict(conv_w=jnp.zeros((32, CONV_CH), F32).at[:CONV_K].set(conv_w[l]),
                  conv_b=conv_b[l].reshape(1, -1), conv_ln_g=conv_ln_g[l].reshape(1, -1),
                  conv_ln_b=conv_ln_b[l].reshape(1, -1), hg_norm_g=hg_norm_g[l].reshape(1, -1),
                  gla_norm_g=gla_norm_g[l].reshape(1, -1), ones_bd=ones_bd,
                  w_br_conv=w_br_conv[l].astype(BF16), w_br_hg=w_br_hg[l].astype(BF16),
                  w_br_gla=w_br_gla[l].astype(BF16), w_br_att=w_br_att[l].astype(BF16),
                  w_out=w_out[l].astype(BF16), g_ffn=g_ffn[l].reshape(1, d), router_w=rw, router_b=rb,
                  tri=tri, upper=upper)
        x_new, h2, slots, gran = merge(xs, modv[l], conv_z, hg_pack, hg_f, hg_b, gla_pack, gla_f, gla_b,
                                       att_o, bgate, lw, n_batch, n_ctx_tiles)

        n_rows = sorted_rows_bound(n_batch * seq)
        table, tails, block_expert, n_used = moe_tables(gran[:, 0, :N_EXPERTS], n_rows)
        x_sorted = dispatch(h2, slots, table, tails, n_rows)
        ys = experts(x_sorted, block_expert, n_used, moe_w1, moe_w3, moe_w2, l)
        if l < depth - 1:
            xs, pending = x_new, (modv[l], ys, table, slots)
        else:
            xs = combine(x_new, modv[l], ys, table, slots, final_g.reshape(1, d), n_batch, n_ctx_tiles, True)

    return xs.reshape(n_batch, n_lat, d)
```

```python
import functools
import itertools

import numpy as np
import jax
import jax.numpy as jnp
from jax import lax
from jax.experimental import pallas as pl
from jax.experimental.pallas import tpu as pltpu

F32 = jnp.float32
BF16 = jnp.bfloat16

EPS = 1e-6
NEG = -1e30
TINY = 1e-30
N_MOD = 6
CONV_CH = 256
CONV_K = 31
HG_HEADS = 4
HG_W = 256
GLA_HEADS = 4
GLA_K = 128
GLA_V = 256
GLA_RANK = 16
GLA_TAU = 16.0
ATT_HEADS = 4
ATT_KV_HEADS = 2
ATT_HD = 64
ATT_BLOCK = 128
GRID_W = 64
ROPE_BASE = 10000.0
N_EXPERTS = 16
N_GROUPS = 4
TOP_K = 2
MOE_BLOCK = 256

LANES = 128
TOK_TILE = 256
SCAN_BLOCK = 16
HALO = 16
GRANULE = 8
SORT_ROWS = -(-(TOP_K * TOK_TILE + N_EXPERTS * (GRANULE - 1)) // LANES) * LANES
V7X_VMEM_BYTES = 64 * 1024 * 1024
VMEM_LIMIT = V7X_VMEM_BYTES * 7 // 8

W_CONV = (0, 512)
W_HG = (512, 1792)
W_GLA_QKV = (1792, 2304)
W_TAIL = 2304
W_IN_COLS = 7200
W_IN_PADDED = 7296
W_GLA_RANK = (W_TAIL, W_TAIL + LANES)
W_SKEW = 2 * GLA_RANK
T_GLA_GATE = (0, 256)
T_ATT = (256, 768)
T_BG = (768, 4864)


def _cparams(sem):
    return pltpu.CompilerParams(dimension_semantics=sem, vmem_limit_bytes=VMEM_LIMIT)


def _dot(a, b):
    return jnp.dot(a, b, preferred_element_type=F32)


def _split(a):
    hi = a.astype(BF16)
    lo = (a - hi.astype(F32)).astype(BF16)
    return hi, lo


def _dot_f32(a, b):
    ah, al = _split(a)
    bh, bl = _split(b)
    return _dot(ah, bh) + _dot(ah, bl) + _dot(al, bh)


def _sigmoid(x):
    return 1.0 / (1.0 + jnp.exp(-x))


def _silu(x):
    return x * _sigmoid(x)


def _mod_kernel(c_ref, w_ref, b_ref, o_ref):
    c = c_ref[...]
    o_ref[0] = _dot_f32(_silu(c), w_ref[0]) + b_ref[0]


def modulation(c_rows, w_mod, b_mod):
    depth, d, six_d = w_mod.shape
    nblk = six_d // d
    return pl.pallas_call(
        _mod_kernel,
        grid=(depth, nblk),
        in_specs=[pl.BlockSpec((8, d), lambda l, j: (0, 0)),
                  pl.BlockSpec((1, d, d), lambda l, j: (l, 0, j)),
                  pl.BlockSpec((1, 1, d), lambda l, j: (l, 0, j))],
        out_specs=pl.BlockSpec((1, 8, d), lambda l, j: (l, 0, j)),
        out_shape=jax.ShapeDtypeStruct((depth, 8, six_d), F32),
        compiler_params=_cparams(("parallel", "parallel")),
        name="modulation",
    )(c_rows, w_mod, b_mod.reshape(depth, 1, six_d))


def _modnorm(x, g, shift, scale):
    y = x * lax.rsqrt(jnp.mean(x * x, axis=-1, keepdims=True) + EPS)
    return (y * g) * (1.0 + scale) + shift


def _inproj_kernel(*refs, after_moe):
    if after_moe:
        (d_ref, dn_ref, x_ref, modp_ref, slot_ref, y_hbm, mod_ref, g_ref, w_ref, lb_ref, w2_ref, b2_ref,
         cos_ref, sin_ref, xo_ref, conv_ref, hg_ref, gla_ref, att_ref, bg_ref, tail_ref, ybuf, sem) = refs
    else:
        (x_ref, mod_ref, g_ref, w_ref, lb_ref, w2_ref, b2_ref, cos_ref, sin_ref,
         conv_ref, hg_ref, gla_ref, att_ref, bg_ref, tail_ref) = refs

    @pl.when(pl.program_id(0) == 0)
    def _():
        width = tail_ref.shape[1]
        tail_ref[...] = pltpu.roll(w_ref[:, W_TAIL:W_TAIL + width], width - W_SKEW, axis=1)

    x = x_ref[...]
    if after_moe:
        x = x + modp_ref[0, 5:6, :] * _expert_mix(d_ref, dn_ref, slot_ref, y_hbm, ybuf, sem)
        xo_ref[...] = x
    h = _modnorm(x, g_ref[...], mod_ref[0, 0:1, :], mod_ref[0, 1:2, :]).astype(BF16)

    p = _dot(h, w_ref[:, W_CONV[0]:W_CONV[1]])
    conv_ref[...] = p[:, :CONV_CH] * _sigmoid(p[:, CONV_CH:])

    p = _dot(h, w_ref[:, W_HG[0]:W_HG[1]])
    hg_ref[:, 0:256] = p[:, 0:256]
    hg_ref[:, 768:1024] = p[:, 768:1024]
    hg_ref[:, 1536:1792] = _silu(p[:, 1024:1280])
    for d in range(2):
        z = p[:, 256 * (d + 1):256 * (d + 2)]
        lb = lb_ref[d:d + 1, :]
        hg_ref[:, 256 * (d + 1):256 * (d + 2)] = (1.0 - lb) * _sigmoid(-z)
        hg_ref[:, 256 * (d + 4):256 * (d + 5)] = jnp.maximum(lb + (1.0 - lb) * _sigmoid(z), TINY)

    p = _dot(h, w_ref[:, W_GLA_QKV[0]:W_GLA_QKV[1]])
    gla_ref[:, 0:128] = p[:, 0:128] * (float(GLA_K // GLA_HEADS) ** -0.5)
    gla_ref[:, 128:256] = p[:, 128:256]
    gla_ref[:, 512:768] = p[:, 256:512]
    gla_ref[:, 768:1024] = _silu(_dot(h, tail_ref[:, T_GLA_GATE[0]:T_GLA_GATE[1]]))
    rank = _dot(h, w_ref[:, W_GLA_RANK[0]:W_GLA_RANK[1]])
    u = _dot_f32(rank, w2_ref[...]) + b2_ref[...]
    log_sig = jnp.minimum(u, 0.0) - jnp.log(1.0 + jnp.exp(-jnp.abs(u)))
    gla_ref[:, 256:512] = jnp.exp(log_sig * (1.0 / GLA_TAU))

    p = _dot(h, tail_ref[:, T_ATT[0]:T_ATT[1]])
    cos = cos_ref[...]
    sin = sin_ref[...]
    def rope(v, width):
        lane = lax.broadcasted_iota(jnp.int32, v.shape, 1)
        partner = jnp.where((lane & 31) < 16,
                            pltpu.roll(v, width - 16, axis=1), pltpu.roll(v, 16, axis=1))
        return v * cos[:, :width] + partner * sin[:, :width]

    att_ref[:, 0:256] = rope(p[:, 0:256], 256) * (float(ATT_HD) ** -0.5)
    att_ref[:, 256:384] = rope(p[:, 256:384], 128)
    att_ref[:, 384:512] = p[:, 384:512]

    p = _dot(h, tail_ref[:, T_BG[0]:T_BG[1]])
    bg_ref[...] = _sigmoid(p).astype(BF16)


def inproj(x, modv, g_mix, w_in_p, layer, lb, w2p, b2p, cos_t, sin_t, n_batch, n_ctx_tiles, moe=None):
    n, d = x.shape
    nt = n // TOK_TILE
    tiles_per_batch = nt // n_batch

    def mod_idx(i):
        b = i // tiles_per_batch
        return b * 2 + ((i % tiles_per_batch) >= n_ctx_tiles).astype(jnp.int32)

    const = lambda i: (0, 0)
    row = lambda i: (i, 0)
    seq = lambda i: (i % tiles_per_batch, 0)
    pre_specs, pre_ops, pre_out_specs, pre_out_shapes, pre_scratch = [], [], [], [], []
    if moe is not None:
        modv_prev, ys, table, slots = moe
        pre_specs = [pl.BlockSpec((1, 1, LANES), lambda i: (i, 0, 0), memory_space=pltpu.SMEM),
                     pl.BlockSpec((1, 1, LANES), lambda i: (jnp.minimum(i + 1, nt - 1), 0, 0),
                                  memory_space=pltpu.SMEM)]
        pre_ops = [table, table]
        pre_out_specs = [pl.BlockSpec((TOK_TILE, d), row)]
        pre_out_shapes = [jax.ShapeDtypeStruct((n, d), F32)]
        pre_scratch = [pltpu.VMEM((2, SORT_ROWS, d), F32), pltpu.SemaphoreType.DMA((2,))]
    mid_specs, mid_ops = [], []
    if moe is not None:
        mid_specs = [pl.BlockSpec((1, N_MOD, d), lambda i: (mod_idx(i), 0, 0)),
                     pl.BlockSpec((TOK_TILE, LANES), row),
                     pl.BlockSpec(memory_space=pl.ANY)]
        mid_ops = [modv_prev, slots, ys]
    outs = pl.pallas_call(
        functools.partial(_inproj_kernel, after_moe=moe is not None),
        grid=(nt,),
        in_specs=pre_specs + [pl.BlockSpec((TOK_TILE, d), row)] + mid_specs + [
                  pl.BlockSpec((1, N_MOD, d), lambda i: (mod_idx(i), 0, 0)),
                  pl.BlockSpec((1, d), const),
                  pl.BlockSpec((None, d, W_IN_PADDED), lambda i: (layer, 0, 0), pipeline_mode=pl.Buffered(1)),
                  pl.BlockSpec((2, HG_W), const),
                  pl.BlockSpec((LANES, 2 * GLA_K), const),
                  pl.BlockSpec((1, 2 * GLA_K), const),
                  pl.BlockSpec((TOK_TILE, 256), seq),
                  pl.BlockSpec((TOK_TILE, 256), seq)],
        out_specs=pre_out_specs + [pl.BlockSpec((TOK_TILE, 256), row),
                                   pl.BlockSpec((TOK_TILE, 1792), row),
                                   pl.BlockSpec((TOK_TILE, 1024), row),
                                   pl.BlockSpec((TOK_TILE, 512), row),
                                   pl.BlockSpec((TOK_TILE, 4096), row)],
        out_shape=pre_out_shapes + [jax.ShapeDtypeStruct((n, 256), F32),
                                    jax.ShapeDtypeStruct((n, 1792), F32),
                                    jax.ShapeDtypeStruct((n, 1024), F32),
                                    jax.ShapeDtypeStruct((n, 512), F32),
                                    jax.ShapeDtypeStruct((n, 4096), BF16)],
        scratch_shapes=[pltpu.VMEM((d, W_IN_PADDED - W_TAIL), BF16)] + pre_scratch,
        compiler_params=_cparams(("arbitrary",)),
        name="combine_inproj" if moe is not None else "inproj",
    )(*pre_ops, x, *mid_ops, modv, g_mix, w_in_p, lb, w2p, b2p, cos_t, sin_t)
    return outs


SCAN_IN = 4
SCAN_SCRATCH = 12


ATTN_IN = 11
ATTN_STRIDE = 3


def _trace_alternately(chains):
    live = list(chains)
    rnd = 0
    while live:
        for entry in list(live):
            gen, stride = entry
            if rnd % stride == 0:
                try:
                    next(gen)
                except StopIteration:
                    live.remove(entry)
        rnd += 1


def _scan_kernel(*refs, attn):
    if attn is not None:
        sink_ref, refs = refs[0], refs[1:]
    n_chain = 2
    n_in = n_chain * SCAN_IN
    ins = refs[:n_in]
    sel = refs[n_in]
    perms = refs[n_in + 1:n_in + 5]
    pos = n_in + 5
    if attn is not None:
        att_in = refs[pos:pos + ATTN_IN]
        pos += ATTN_IN
    outs = refs[pos:pos + n_chain]
    pos += n_chain
    if attn is not None:
        att_out = refs[pos]
        pos += 1
    scratch = refs[pos:]
    chains = []
    for i in range(n_chain):
        rev = i % 2 == 1
        chains.append((_scan_direction(*ins[i * SCAN_IN:(i + 1) * SCAN_IN], sel,
                                       perms[2 * rev], perms[2 * rev + 1], outs[i],
                                       *scratch[i * SCAN_SCRATCH:(i + 1) * SCAN_SCRATCH], rev=rev), 1))
    if attn is not None:
        ctx_blocks, blocks_per_batch = attn
        q_ref, kc_ref, vc_ref = att_in[:3]
        kb, vb = att_in[3:7], att_in[7:11]
        per_tile = q_ref.shape[0] // ATT_BLOCK
        for a in range(per_tile):
            rows = slice(a * ATT_BLOCK, (a + 1) * ATT_BLOCK)

            def write(hk, x, rows=rows):
                att_out[rows, hk * LANES:(hk + 1) * LANES] = x

            gens = _attn_chains(pl.program_id(1) * per_tile + a, sink_ref, q_ref[rows, :],
                                [kc_ref[...]] + [kb[a + t][...] for t in range(3)],
                                [vc_ref[...]] + [vb[a + t][...] for t in range(3)],
                                write, ctx_blocks, blocks_per_batch)
            chains += [(g, ATTN_STRIDE) for g in gens]
    _trace_alternately(chains)


def _scan_direction(q_ref, k_ref, v_ref, f_ref, sel_ref, perm_ref, permt_ref, o_ref,
                    s_ref, qs_ref, ks_ref, vs_ref, fs_ref, qt_ref, kh_ref, term_ref, w_ref, ah_ref, al_ref,
                    kv_ref, *, rev):
    c = SCAN_BLOCK
    tt, kl = q_ref.shape
    nb = tt // c
    assert nb == c

    @pl.when(pl.program_id(1) == 0)
    def _():
        s_ref[...] = jnp.zeros_like(s_ref)

    perm = perm_ref[...]
    vb = v_ref[...].astype(BF16)
    f_hi, f_lo = _split(f_ref[...])
    qs_ref[...] = _dot(perm, q_ref[...].astype(BF16))
    ks_ref[...] = _dot(perm, k_ref[...].astype(BF16))
    vs_ref[...] = _dot(perm, vb)
    fs_ref[...] = _dot(perm, f_hi) + _dot(perm, f_lo)
    yield

    def slab(ref, s):
        return ref[s * nb:(s + 1) * nb, :]

    a = slab(fs_ref, 0)
    qt_ref[0:nb, :] = (slab(qs_ref, 0) * a).astype(BF16)
    for s in range(1, c):
        a = a * slab(fs_ref, s)
        qt_ref[s * nb:(s + 1) * nb, :] = (slab(qs_ref, s) * a).astype(BF16)
    gam_t = a.T
    g = jnp.ones((nb, kl), F32)
    for s in range(c - 1, -1, -1):
        kh_ref[s * nb:(s + 1) * nb, :] = (slab(ks_ref, s) * g).astype(BF16)
        g = g * slab(fs_ref, s)
    yield

    heads = 256 // 64
    dk = kl // heads
    perm_t = permt_ref[...]
    qt = _dot(perm_t, qt_ref[...]).astype(BF16)
    kh = _dot(perm_t, kh_ref[...]).astype(BF16)

    def head_lanes(x, h):
        return x[:, :dk] if h == 0 else pltpu.roll(x, kl - dk * h, axis=1)[:, :dk]

    q_heads = [head_lanes(qt, h) for h in range(heads)]
    k_heads = [head_lanes(kh, h) for h in range(heads)]
    v_head = lax.broadcasted_iota(jnp.int32, (c, 256), 1) // 64
    g_head = lax.broadcasted_iota(jnp.int32, (dk, 256), 1) // 64
    for j in range(nb):
        lo = j * c
        k4 = jnp.concatenate([k_heads[h][lo:lo + c, :] for h in range(heads)], axis=0)
        v4 = jnp.concatenate([jnp.where(v_head == h, vb[lo:lo + c, :], jnp.zeros_like(vb[lo:lo + c, :]))
                              for h in range(heads)], axis=0)
        kv_ref[j] = lax.dot_general(k4, v4, (((0,), (0,)), ((), ())), preferred_element_type=F32)
        yield

    n_pair = 0
    for s in range(c):
        p = slab(qs_ref, s)
        for d in range(s + 1):
            if d > 0:
                p = p * slab(fs_ref, s - d + 1)
            term_ref[n_pair * nb:(n_pair + 1) * nb, :] = (p * slab(ks_ref, s - d)).astype(BF16)
            n_pair += 1
        yield
    w_ref[...] = _dot(term_ref[...], sel_ref[...])
    yield
    n_pair = 0
    for s in range(c):
        acc = jnp.zeros((nb, 256), F32)
        for d in range(s + 1):
            acc = acc + w_ref[n_pair * nb:(n_pair + 1) * nb, :] * slab(vs_ref, s - d)
            n_pair += 1
        hi, lo = _split(acc)
        ah_ref[s * nb:(s + 1) * nb, :] = hi
        al_ref[s * nb:(s + 1) * nb, :] = lo
        yield

    o_ref[...] = _dot(perm_t, ah_ref[...]) + _dot(perm_t, al_ref[...])
    yield
    st = s_ref[...]
    for jj in range(nb):
        j = nb - 1 - jj if rev else jj
        lo = j * c
        q4 = jnp.concatenate([q_heads[h][lo:lo + c, :] for h in range(heads)], axis=0)
        o4 = _dot(q4, st.astype(BF16))
        inter = o4[0:c, :]
        for h in range(1, heads):
            inter = jnp.where(v_head == h, o4[h * c:(h + 1) * c, :], inter)
        o_ref[lo:lo + c, :] += inter
        gam = jnp.broadcast_to(gam_t[0:dk, j:j + 1], (dk, 256))
        for h in range(1, heads):
            gam = jnp.where(g_head == h, gam_t[h * dk:(h + 1) * dk, j:j + 1], gam)
        st = gam * st + kv_ref[j]
        yield
    s_ref[...] = st


def gated_scan(pack, cols_fwd, cols_rev, kl, n_batch, n_ctx_tiles, name, attn=None):
    n = pack.shape[0]
    nt = n // TOK_TILE
    tpb = nt // n_batch

    def rev_tile(i):
        return jnp.where(i < n_ctx_tiles, n_ctx_tiles - 1 - i, tpb - 1 - (i - n_ctx_tiles))

    heads = 4
    c = SCAN_BLOCK
    nb = TOK_TILE // c
    perms = []
    for rev in (False, True):
        perm = np.zeros((TOK_TILE, TOK_TILE), np.float32)
        for j in range(nb):
            for s in range(c):
                perm[s * nb + j, j * c + (c - 1 - s if rev else s)] = 1.0
        perms += [jnp.asarray(perm, BF16), jnp.asarray(perm.T, BF16)]
    n_pairs = c * (c + 1) // 2
    fwd = lambda col: (lambda b, i, *_: (b * tpb + i, col))
    bwd = lambda col: (lambda b, i, *_: (b * tpb + rev_tile(i), col))
    const = lambda b, i, *_: (0, 0)

    def direction_specs(rows, cols, kl):
        qc, kc, vc, fc = cols
        specs = [pl.BlockSpec((TOK_TILE, kl), rows(qc)),
                 pl.BlockSpec((TOK_TILE, kl), rows(kc)),
                 pl.BlockSpec((TOK_TILE, 256), rows(vc)),
                 pl.BlockSpec((TOK_TILE, kl), rows(fc))]
        assert len(specs) == SCAN_IN
        return specs

    def chain_scratch(kl):
        dk = kl // heads
        shapes = [pltpu.VMEM((dk, 256), F32),
                  pltpu.VMEM((TOK_TILE, kl), F32),
                  pltpu.VMEM((TOK_TILE, kl), F32),
                  pltpu.VMEM((TOK_TILE, 256), F32),
                  pltpu.VMEM((TOK_TILE, kl), F32),
                  pltpu.VMEM((TOK_TILE, kl), BF16),
                  pltpu.VMEM((TOK_TILE, kl), BF16),
                  pltpu.VMEM((n_pairs * nb, kl), BF16),
                  pltpu.VMEM((n_pairs * nb, 256), F32),
                  pltpu.VMEM((TOK_TILE, 256), BF16),
                  pltpu.VMEM((TOK_TILE, 256), BF16),
                  pltpu.VMEM((nb, dk, 256), F32)]
        assert len(shapes) == SCAN_SCRATCH
        return shapes

    dk = kl // heads
    sel = jnp.asarray(np.arange(kl)[:, None] // dk == np.arange(256)[None, :] // 64, BF16)
    in_specs = direction_specs(fwd, cols_fwd, kl) + direction_specs(bwd, cols_rev, kl)
    in_specs += [pl.BlockSpec(sel.shape, const)] + [pl.BlockSpec((TOK_TILE, TOK_TILE), const)] * 4
    operands = [pack] * (2 * SCAN_IN) + [sel] + perms
    out_specs = [pl.BlockSpec((TOK_TILE, 256), fwd(0)), pl.BlockSpec((TOK_TILE, 256), bwd(0))]
    prefetch, kernel_attn = [], None
    if attn is not None:
        att_pack, sink, ctx_len = attn
        per_tile = TOK_TILE // ATT_BLOCK
        bpb = tpb * per_tile
        cb = ctx_len // ATT_BLOCK

        def band(off, col):
            def idx(b, i, *_):
                return (b * bpb + jnp.clip(i * per_tile + off, cb, bpb - 1), col)
            return pl.BlockSpec((ATT_BLOCK, LANES), idx)

        def ctx(col):
            return pl.BlockSpec((ctx_len, LANES), lambda b, i, *_: (b * (bpb * ATT_BLOCK // ctx_len), col))

        att_specs = ([pl.BlockSpec((TOK_TILE, 256), fwd(0)), ctx(2), ctx(3)]
                     + [band(off, 2) for off in range(-1, per_tile + 1)]
                     + [band(off, 3) for off in range(-1, per_tile + 1)])
        assert len(att_specs) == ATTN_IN
        in_specs += att_specs
        operands += [att_pack] * ATTN_IN
        out_specs.append(pl.BlockSpec((TOK_TILE, 256), fwd(0)))
        prefetch, kernel_attn = [sink], (cb, bpb)
    grid_spec = pltpu.PrefetchScalarGridSpec(
        num_scalar_prefetch=len(prefetch),
        grid=(n_batch, tpb),
        in_specs=in_specs,
        out_specs=out_specs,
        scratch_shapes=chain_scratch(kl) + chain_scratch(kl),
    )
    return pl.pallas_call(
        functools.partial(_scan_kernel, attn=kernel_attn),
        grid_spec=grid_spec,
        out_shape=[jax.ShapeDtypeStruct((n, 256), F32)] * len(out_specs),
        compiler_params=_cparams(("parallel", "arbitrary")),
        name=name,
    )(*prefetch, *operands)


def _attn_chains(n, sink_ref, q, keys, vals, write, ctx_blocks, blocks_per_batch):
    blk = ATT_BLOCK
    lane = lax.broadcasted_iota(jnp.int32, (blk, LANES), 1)
    low = lane < ATT_HD
    qi = lax.broadcasted_iota(jnp.int32, (blk, blk), 0)
    ki = lax.broadcasted_iota(jnp.int32, (blk, blk), 1)
    is_lat = n >= ctx_blocks
    band_ok = [
        jnp.logical_and(jnp.logical_and(is_lat, n - 1 >= ctx_blocks), qi <= ki),
        jnp.logical_and(is_lat, qi >= 0),
        jnp.logical_and(jnp.logical_and(is_lat, n + 1 < blocks_per_batch), ki <= qi),
    ]

    def dup(x, hk):
        xr = pltpu.roll(x, ATT_HD, axis=1)
        lo_x = lax.broadcasted_iota(jnp.int32, x.shape, 1) < ATT_HD
        return (jnp.where(lo_x, x, xr) if hk == 0 else jnp.where(lo_x, xr, x)).astype(BF16)

    nt_dims = (((1,), (1,)), ((), ()))

    def kv_group(hk):
        kd = [dup(x, hk) for x in keys]
        vd = [dup(x, hk) for x in vals]
        qh = q[:, hk * LANES:(hk + 1) * LANES]
        q2 = jnp.concatenate([jnp.where(low, qh, 0.0), jnp.where(low, 0.0, qh)], axis=0).astype(BF16)
        yield
        s = [lax.dot_general(q2, kx, nt_dims, preferred_element_type=F32) for kx in kd]
        for t in range(3):
            s[t + 1] = jnp.where(jnp.concatenate([band_ok[t]] * 2, axis=0), s[t + 1], NEG)
        yield
        sink = jnp.where(lax.broadcasted_iota(jnp.int32, (2 * blk, 1), 0) < blk,
                         sink_ref[hk * 2], sink_ref[hk * 2 + 1])
        pieces = [s[0][:, :LANES], s[0][:, LANES:], s[1], s[2], s[3]]
        top = pieces[0]
        for piece in pieces[1:]:
            top = jnp.maximum(top, piece)
        m = jnp.maximum(sink, top.max(axis=-1, keepdims=True))
        yield
        e = [jnp.exp(x - m) for x in s]
        tot = e[0][:, :LANES] + e[0][:, LANES:] + e[1] + e[2] + e[3]
        den = jnp.exp(sink - m) + tot.sum(axis=-1, keepdims=True)
        yield
        o = _dot(e[0].astype(BF16), vd[0])
        for t in range(1, 4):
            o = o + _dot(e[t].astype(BF16), vd[t])
        yield
        o = o / den
        write(hk, jnp.where(low, o[:blk, :], o[blk:, :]))

    return [kv_group(hk) for hk in range(ATT_KV_HEADS)]


def _head_norm(o, ones_bd, g):
    sq = o * o
    hi, lo = _split(sq)
    ms = (_dot(hi, ones_bd) + _dot(lo, ones_bd)) * (1.0 / 64.0)
    return o * lax.rsqrt(ms + EPS) * g


def _route_select(aff, bias):
    lane = lax.broadcasted_iota(jnp.int32, aff.shape, 1)
    epg = N_EXPERTS // N_GROUPS
    pos = lane & (epg - 1)
    v = aff + bias

    def nxt(x, o):
        return pltpu.roll(x, LANES - o, axis=1)

    def prv(x, o):
        return pltpu.roll(x, o, axis=1)

    beaten = jnp.zeros(aff.shape, jnp.int32)
    for o in range(1, epg):
        beaten = beaten + jnp.where(jnp.logical_and(pos + o < epg, nxt(v, o) > v), 1, 0)
        beaten = beaten + jnp.where(jnp.logical_and(pos >= o, prv(v, o) >= v), 1, 0)
    top2 = beaten < TOP_K
    t = jnp.where(top2, v, 0.0)
    score = t
    for o in range(1, epg):
        score = score + jnp.where(pos + o < epg, nxt(t, o), 0.0) + jnp.where(pos >= o, prv(t, o), 0.0)
    worse = jnp.zeros(aff.shape, jnp.int32)
    for o in range(epg, N_EXPERTS, epg):
        worse = worse + jnp.where(jnp.logical_and(lane + o < N_EXPERTS, nxt(score, o) > score), 1, 0)
        worse = worse + jnp.where(jnp.logical_and(lane >= o, prv(score, o) >= score), 1, 0)
    sel = jnp.logical_and(jnp.logical_and(lane < N_EXPERTS, worse == 0), top2)
    picked = jnp.where(sel, aff, 0.0)
    gate = picked / jnp.sum(picked, axis=-1, keepdims=True)
    return gate, sel


def _merge_kernel(x_ref, mod_ref, zp_ref, z_ref, zn_ref, hgf_ref, hgb_ref, hgg_ref, glf_ref, glb_ref,
                  glg_ref, att_ref, bg_ref, cw_ref, cb_ref, lng_ref, lnb_ref, hgn_ref, gln_ref,
                  ones_ref, wbc_ref, wbh_ref, wbg_ref, wba_ref, wo_ref, gf_ref, rw_ref, rb_ref, tri_ref,
                  upper_ref, xo_ref, h2_ref, slot_ref, gran_ref, zs_ref, zsh_ref, *, tiles_per_batch,
                  n_ctx_tiles):
    i = pl.program_id(0)
    tt = x_ref.shape[0]
    ti = i % tiles_per_batch
    has_prev = jnp.logical_and(ti != 0, ti != n_ctx_tiles)
    has_next = jnp.logical_and(ti != n_ctx_tiles - 1, ti != tiles_per_batch - 1)

    zs_ref[0:HALO, :] = jnp.where(has_prev, zp_ref[...], 0.0)
    zs_ref[HALO:HALO + tt, :] = z_ref[...]
    zs_ref[HALO + tt:HALO + tt + HALO, :] = jnp.where(has_next, zn_ref[...], 0.0)
    span = tt + 2 * HALO - 8
    for b in range(8):
        zsh_ref[b] = zs_ref[b:b + span, :]
    acc = jnp.zeros((tt, CONV_CH), F32) + cb_ref[...]
    for j in range(CONV_K):
        off = HALO - CONV_K // 2 + j
        acc = acc + zsh_ref[off % 8, off - off % 8:off - off % 8 + tt, :] * cw_ref[j:j + 1, :]
    mu = jnp.mean(acc, axis=-1, keepdims=True)
    cen = acc - mu
    var = jnp.mean(cen * cen, axis=-1, keepdims=True)
    conv_y = _silu(cen * lax.rsqrt(var + EPS) * lng_ref[...] + lnb_ref[...])

    ones_bd = ones_ref[...]
    hg_y = _head_norm(hgf_ref[...] + hgb_ref[...], ones_bd, hgn_ref[...]) * hgg_ref[...]
    gla_y = _head_norm(glf_ref[...] + glb_ref[...], ones_bd, gln_ref[...]) * glg_ref[...]

    d = x_ref.shape[1]
    merged = bg_ref[:, 0:d].astype(F32) * _dot(conv_y.astype(BF16), wbc_ref[...])
    merged = merged + bg_ref[:, d:2 * d].astype(F32) * _dot(hg_y.astype(BF16), wbh_ref[...])
    merged = merged + bg_ref[:, 2 * d:3 * d].astype(F32) * _dot(gla_y.astype(BF16), wbg_ref[...])
    merged = merged + bg_ref[:, 3 * d:4 * d].astype(F32) * _dot(att_ref[...].astype(BF16), wba_ref[...])
    mix = _dot(merged.astype(BF16), wo_ref[...])

    x_new = x_ref[...] + mod_ref[0, 2:3, :] * mix
    xo_ref[...] = x_new
    h2 = _modnorm(x_new, gf_ref[...], mod_ref[0, 3:4, :], mod_ref[0, 4:5, :])
    h2_ref[...] = h2
    gate, sel = _route_select(_sigmoid(_dot_f32(h2, rw_ref[...])), rb_ref[...])

    incl = _dot(tri_ref[...], sel.astype(F32).astype(BF16))
    gran = jnp.floor((incl[tt - 1:tt, :] + (GRANULE - 1.0)) * (1.0 / GRANULE))
    gran8 = jnp.broadcast_to(gran, (8, LANES))
    start = _dot(gran8.astype(BF16), upper_ref[...])[0:1, :]
    pos = GRANULE * start + incl - 1.0
    p0 = jnp.min(jnp.where(sel, pos, 1e9), axis=-1, keepdims=True)
    p1 = jnp.max(jnp.where(sel, pos, -1.0), axis=-1, keepdims=True)
    g0 = jnp.sum(jnp.where(jnp.logical_and(sel, pos == p0), gate, 0.0), axis=-1, keepdims=True)
    g1 = jnp.sum(jnp.where(jnp.logical_and(sel, pos == p1), gate, 0.0), axis=-1, keepdims=True)
    lane = lax.broadcasted_iota(jnp.int32, (tt, LANES), 1)
    slot_ref[...] = jnp.where(lane == 0, p0, jnp.where(lane == 1, p1, jnp.where(lane == 2, g0,
                              jnp.where(lane == 3, g1, 0.0))))
    gran_ref[0] = gran8.astype(jnp.int32)


def merge(x, modv, conv_z, hg_pack, hg_f, hg_b, gla_pack, gla_f, gla_b, att_o, bgate, lw, n_batch,
          n_ctx_tiles):
    n, d = x.shape
    nt = n // TOK_TILE
    tpb = nt // n_batch
    hpt = TOK_TILE // HALO
    n_halo = n // HALO

    def mod_idx(i):
        return (i // tpb) * 2 + ((i % tpb) >= n_ctx_tiles).astype(jnp.int32)

    row = lambda i: (i, 0)
    const = lambda i: (0, 0)
    col = lambda c: (lambda i: (i, c))
    layer = lw["layer"]

    def full(a):
        if a.ndim == 3:
            return pl.BlockSpec((None,) + a.shape[1:], lambda i: (layer, 0, 0))
        return pl.BlockSpec(a.shape, const)

    weights = [lw["conv_w"], lw["conv_b"], lw["conv_ln_g"], lw["conv_ln_b"], lw["hg_norm_g"],
               lw["gla_norm_g"], lw["ones_bd"], lw["w_br_conv"], lw["w_br_hg"], lw["w_br_gla"],
               lw["w_br_att"], lw["w_out"], lw["g_ffn"], lw["router_w"], lw["router_b"], lw["tri"],
               lw["upper"]]
    return pl.pallas_call(
        functools.partial(_merge_kernel, tiles_per_batch=tpb, n_ctx_tiles=n_ctx_tiles),
        grid=(nt,),
        in_specs=[pl.BlockSpec((TOK_TILE, d), row),
                  pl.BlockSpec((1, N_MOD, d), lambda i: (mod_idx(i), 0, 0)),
                  pl.BlockSpec((HALO, CONV_CH), lambda i: (jnp.maximum(i * hpt - 1, 0), 0)),
                  pl.BlockSpec((TOK_TILE, CONV_CH), row),
                  pl.BlockSpec((HALO, CONV_CH), lambda i: (jnp.minimum((i + 1) * hpt, n_halo - 1), 0)),
                  pl.BlockSpec((TOK_TILE, 256), row),
                  pl.BlockSpec((TOK_TILE, 256), row),
                  pl.BlockSpec((TOK_TILE, 256), col(6)),
                  pl.BlockSpec((TOK_TILE, 256), row),
                  pl.BlockSpec((TOK_TILE, 256), row),
                  pl.BlockSpec((TOK_TILE, 256), col(3)),
                  pl.BlockSpec((TOK_TILE, 256), row),
                  pl.BlockSpec((TOK_TILE, 4 * d), row)] + [full(w) for w in weights],
        out_specs=[pl.BlockSpec((TOK_TILE, d), row),
                   pl.BlockSpec((TOK_TILE, d), row),
                   pl.BlockSpec((TOK_TILE, LANES), row),
                   pl.BlockSpec((1, 8, LANES), lambda i: (i, 0, 0))],
        out_shape=[jax.ShapeDtypeStruct((n, d), F32),
                   jax.ShapeDtypeStruct((n, d), F32),
                   jax.ShapeDtypeStruct((n, LANES), F32),
                   jax.ShapeDtypeStruct((nt, 8, LANES), jnp.int32)],
        scratch_shapes=[pltpu.VMEM((TOK_TILE + 2 * HALO, CONV_CH), F32),
                        pltpu.VMEM((8, TOK_TILE + 2 * HALO - 8, CONV_CH), F32)],
        compiler_params=_cparams(("parallel",)),
        name="merge",
    )(x, modv, conv_z, conv_z, conv_z, hg_f, hg_b, hg_pack, gla_f, gla_b, gla_pack, att_o, bgate,
      *weights)


def _granule_copy(src, src_row, dst, dst_row, sem, granules=1):
    rows = granules * GRANULE
    return pltpu.make_async_copy(src.at[pl.ds(pl.multiple_of(src_row, GRANULE), rows)],
                                 dst.at[pl.ds(pl.multiple_of(dst_row, GRANULE), rows)], sem)


RUN_CHUNKS = (8, 4, 2, 1)
MAX_RUN = TOK_TILE // GRANULE


def _for_each_run_chunk(tab_ref, fn):
    big = RUN_CHUNKS[0]
    for e in range(N_EXPERTS):
        local = tab_ref[0, 0, e]
        glob = tab_ref[0, 0, N_EXPERTS + e]
        n = tab_ref[0, 0, 2 * N_EXPERTS + e]
        for k in range(MAX_RUN // big):
            @pl.when(n >= big * (k + 1))
            def _(k=k):
                fn(local + big * k, glob + big * k, big)
        off = n - n % big
        for size in RUN_CHUNKS[1:]:
            @pl.when((n & size) != 0)
            def _(off=off, size=size):
                fn(local + off, glob + off, size)
            off = off + (n & size)


def _slot_matrix(slot_ref, width, weighted):
    tt = slot_ref.shape[0]
    col = lax.broadcasted_iota(jnp.int32, (tt, width), 1).astype(F32)
    hit0 = col == slot_ref[:, 0:1]
    hit1 = col == slot_ref[:, 1:2]
    if not weighted:
        return jnp.where(jnp.logical_or(hit0, hit1), 1.0, 0.0)
    return jnp.where(hit0, slot_ref[:, 2:3], 0.0) + jnp.where(hit1, slot_ref[:, 3:4], 0.0)


def _dispatch_kernel(tab_ref, tabp_ref, tail_ref, h_ref, slot_ref, xs_hbm, buf, zbuf, sem, zsem, bsem):
    i = pl.program_id(0)
    nb = pl.num_programs(0)
    slot = i % 2

    perm = _slot_matrix(slot_ref, buf.shape[1], weighted=False).astype(BF16)
    buf[slot] = lax.dot_general(perm, h_ref[...].astype(BF16), (((0,), (0,)), ((), ())),
                                preferred_element_type=F32)

    def chunk(s):
        def copy(local, glob, granules):
            return _granule_copy(buf.at[s], local * GRANULE, xs_hbm, glob * GRANULE, sem.at[s], granules)
        return copy

    _for_each_run_chunk(tab_ref, lambda *a: chunk(slot)(*a).start())

    @pl.when(i > 0)
    def _():
        _for_each_run_chunk(tabp_ref, lambda *a: chunk(1 - slot)(*a).wait())

    @pl.when(i == nb - 1)
    def _():
        _for_each_run_chunk(tab_ref, lambda *a: chunk(slot)(*a).wait())

    @pl.when(i == 0)
    def _():
        zbuf[...] = jnp.zeros_like(zbuf)
        n_used = tail_ref[0, 0, 2 * N_EXPERTS]
        n_blocks = xs_hbm.shape[0] // MOE_BLOCK

        def block_copy(b):
            return pltpu.make_async_copy(
                zbuf, xs_hbm.at[pl.ds(pl.multiple_of(b * MOE_BLOCK, MOE_BLOCK), MOE_BLOCK)], bsem)

        for e in range(N_EXPERTS):
            def fill(m, carry, e=e):
                _granule_copy(zbuf, 0, xs_hbm, (tail_ref[0, 0, e] + m) * GRANULE, zsem).start()
                return carry
            lax.fori_loop(0, tail_ref[0, 0, N_EXPERTS + e], fill, 0)

        def fill_block(b, carry):
            block_copy(b).start()
            return carry
        lax.fori_loop(n_used, n_blocks, fill_block, 0)

        for e in range(N_EXPERTS):
            def done(m, carry):
                _granule_copy(zbuf, 0, xs_hbm, 0, zsem).wait()
                return carry
            lax.fori_loop(0, tail_ref[0, 0, N_EXPERTS + e], done, 0)

        def done_block(b, carry):
            block_copy(b).wait()
            return carry
        lax.fori_loop(n_used, n_blocks, done_block, 0)


def dispatch(h2, slots, table, tails, n_rows):
    n, d = h2.shape
    nt = n // TOK_TILE
    return pl.pallas_call(
        _dispatch_kernel,
        grid=(nt,),
        in_specs=[pl.BlockSpec((1, 1, LANES), lambda i: (i, 0, 0), memory_space=pltpu.SMEM),
                  pl.BlockSpec((1, 1, LANES), lambda i: (jnp.maximum(i - 1, 0), 0, 0), memory_space=pltpu.SMEM),
                  pl.BlockSpec((1, 1, LANES), lambda i: (0, 0, 0), memory_space=pltpu.SMEM),
                  pl.BlockSpec((TOK_TILE, d), lambda i: (i, 0)),
                  pl.BlockSpec((TOK_TILE, LANES), lambda i: (i, 0))],
        out_specs=pl.BlockSpec(memory_space=pl.ANY),
        out_shape=jax.ShapeDtypeStruct((n_rows, d), F32),
        scratch_shapes=[pltpu.VMEM((2, SORT_ROWS, d), F32),
                        pltpu.VMEM((MOE_BLOCK, d), F32),
                        pltpu.SemaphoreType.DMA((2,)),
                        pltpu.SemaphoreType.DMA(()),
                        pltpu.SemaphoreType.DMA(())],
        compiler_params=_cparams(("arbitrary",)),
        name="dispatch",
    )(table, table, tails, h2, slots)


def _expert_kernel(be_ref, nu_ref, x_ref, w1_ref, w3_ref, w2_ref, y_ref, w1b, w3b, w2b):
    i = pl.program_id(0)
    n_used = nu_ref[0]

    @pl.when(i < n_used)
    def _():
        first = jnp.logical_or(i == 0, be_ref[i] != be_ref[jnp.maximum(i - 1, 0)])

        @pl.when(first)
        def _():
            w1b[...] = w1_ref[0].astype(BF16)
            w3b[...] = w3_ref[0].astype(BF16)
            w2b[...] = w2_ref[0].astype(BF16)

        x = x_ref[...].astype(BF16)
        hid = _silu(_dot(x, w1b[...])) * _dot(x, w3b[...])
        y_ref[...] = _dot(hid.astype(BF16), w2b[...])

    @pl.when(i >= n_used)
    def _():
        y_ref[...] = jnp.zeros_like(y_ref)


def experts(xs, block_expert, n_used, w1, w3, w2, layer):
    n_rows, d = xs.shape
    n_blocks = n_rows // MOE_BLOCK
    ff = w1.shape[-1]
    grid_spec = pltpu.PrefetchScalarGridSpec(
        num_scalar_prefetch=2,
        grid=(n_blocks,),
        in_specs=[pl.BlockSpec((MOE_BLOCK, d), lambda i, be, nu: (jnp.minimum(i, nu[0] - 1), 0)),
                  pl.BlockSpec((None, 1, d, ff), lambda i, be, nu: (layer, be[i], 0, 0)),
                  pl.BlockSpec((None, 1, d, ff), lambda i, be, nu: (layer, be[i], 0, 0)),
                  pl.BlockSpec((None, 1, ff, d), lambda i, be, nu: (layer, be[i], 0, 0))],
        out_specs=pl.BlockSpec((MOE_BLOCK, d), lambda i, be, nu: (i, 0)),
        scratch_shapes=[pltpu.VMEM((d, ff), BF16),
                        pltpu.VMEM((d, ff), BF16),
                        pltpu.VMEM((ff, d), BF16)],
    )
    return pl.pallas_call(
        _expert_kernel,
        grid_spec=grid_spec,
        out_shape=jax.ShapeDtypeStruct((n_rows, d), F32),
        compiler_params=_cparams(("arbitrary",)),
        name="experts",
    )(block_expert, n_used, xs, w1, w3, w2)


def _combine_kernel(d_ref, dn_ref, x_ref, mod_ref, g_ref, slot_ref, y_hbm, o_ref, ybuf, sem, *, final):
    y = _expert_mix(d_ref, dn_ref, slot_ref, y_hbm, ybuf, sem)
    x = x_ref[...] + mod_ref[0, 5:6, :] * y
    if final:
        x = x * lax.rsqrt(jnp.mean(x * x, axis=-1, keepdims=True) + EPS) * g_ref[...]
    o_ref[...] = x


def _expert_mix(d_ref, dn_ref, slot_ref, y_hbm, ybuf, sem):
    i = pl.program_id(0)
    nb = pl.num_programs(0)
    slot = i % 2

    def chunk(s):
        def copy(local, glob, granules):
            return _granule_copy(y_hbm, glob * GRANULE, ybuf.at[s], local * GRANULE, sem.at[s], granules)
        return copy

    @pl.when(i == 0)
    def _():
        ybuf[...] = jnp.zeros_like(ybuf)
        _for_each_run_chunk(d_ref, lambda *a: chunk(0)(*a).start())

    @pl.when(i + 1 < nb)
    def _():
        _for_each_run_chunk(dn_ref, lambda *a: chunk(1 - slot)(*a).start())

    _for_each_run_chunk(d_ref, lambda *a: chunk(slot)(*a).wait())

    hi, lo = _split(_slot_matrix(slot_ref, ybuf.shape[1], weighted=True))
    yb = ybuf[slot].astype(BF16)
    return _dot(hi, yb) + _dot(lo, yb)


def combine(x, modv, ys, table, slots, final_g, n_batch, n_ctx_tiles, final):
    n, d = x.shape
    nt = n // TOK_TILE
    tpb = nt // n_batch
    if final:
        lat = tpb - n_ctx_tiles
        steps = n_batch * lat
        tile = lambda s: (s // lat) * tpb + n_ctx_tiles + s % lat
        out_rows = steps * TOK_TILE
    else:
        steps = nt
        tile = lambda s: s
        out_rows = n

    def mod_idx(s):
        t = tile(s)
        return (t // tpb) * 2 + ((t % tpb) >= n_ctx_tiles).astype(jnp.int32)

    return pl.pallas_call(
        functools.partial(_combine_kernel, final=final),
        grid=(steps,),
        in_specs=[pl.BlockSpec((1, 1, LANES), lambda s: (tile(s), 0, 0), memory_space=pltpu.SMEM),
                  pl.BlockSpec((1, 1, LANES), lambda s: (tile(jnp.minimum(s + 1, steps - 1)), 0, 0),
                               memory_space=pltpu.SMEM),
                  pl.BlockSpec((TOK_TILE, d), lambda s: (tile(s), 0)),
                  pl.BlockSpec((1, N_MOD, d), lambda s: (mod_idx(s), 0, 0)),
                  pl.BlockSpec((1, d), lambda s: (0, 0)),
                  pl.BlockSpec((TOK_TILE, LANES), lambda s: (tile(s), 0)),
                  pl.BlockSpec(memory_space=pl.ANY)],
        out_specs=pl.BlockSpec((TOK_TILE, d), lambda s: (s, 0)),
        out_shape=jax.ShapeDtypeStruct((out_rows, d), F32),
        scratch_shapes=[pltpu.VMEM((2, SORT_ROWS, d), F32),
                        pltpu.SemaphoreType.DMA((2,))],
        compiler_params=_cparams(("arbitrary",)),
        name="combine_final" if final else "combine",
    )(table, table, x, modv, final_g, slots, ys)


def sorted_rows_bound(n_tok):
    nt = n_tok // TOK_TILE
    rows = n_tok * TOP_K + nt * N_EXPERTS * (GRANULE - 1) + N_EXPERTS * (MOE_BLOCK - GRANULE)
    return -(-rows // MOE_BLOCK) * MOE_BLOCK


def moe_tables(gran, n_rows):
    nt = gran.shape[0]
    per_blk = MOE_BLOCK // GRANULE
    local = jnp.cumsum(gran, axis=1) - gran
    before = jnp.cumsum(gran, axis=0) - gran
    total = jnp.sum(gran, axis=0)
    padded = (total + per_blk - 1) // per_blk * per_blk
    region_end = jnp.cumsum(padded)
    region = region_end - padded
    table = jnp.concatenate([local, region[None, :] + before, gran,
                             jnp.zeros((nt, LANES - 3 * N_EXPERTS), gran.dtype)], axis=1)
    table = table.astype(jnp.int32).reshape(nt, 1, LANES)
    tails = jnp.zeros((LANES,), jnp.int32).at[:N_EXPERTS].set(region + total)
    tails = tails.at[N_EXPERTS:2 * N_EXPERTS].set(padded - total)
    n_blocks = n_rows // MOE_BLOCK
    blk = jnp.arange(n_blocks, dtype=jnp.int32) * per_blk
    block_expert = jnp.minimum(jnp.sum(blk[:, None] >= region_end[None, :], axis=1), N_EXPERTS - 1)
    n_used = (region_end[-1] // per_blk).astype(jnp.int32).reshape(1)
    tails = tails.at[2 * N_EXPERTS].set(n_used[0]).reshape(1, 1, LANES)
    return table, tails, block_expert.astype(jnp.int32), n_used


def _rope_tables(ctx_len, n_lat):
    rows = n_lat // GRID_W
    row = np.repeat(np.arange(rows, dtype=np.float32), GRID_W)
    col = np.tile(np.arange(GRID_W, dtype=np.float32), rows)
    half = ATT_HD // 4
    inv = jnp.asarray(ROPE_BASE, F32) ** (-jnp.arange(half, dtype=F32) / half)
    ang_r = jnp.asarray(row)[:, None] * inv
    ang_c = jnp.asarray(col)[:, None] * inv
    cos64 = jnp.concatenate([jnp.cos(ang_r)] * 2 + [jnp.cos(ang_c)] * 2, axis=1)
    sin64 = jnp.concatenate([-jnp.sin(ang_r), jnp.sin(ang_r), -jnp.sin(ang_c), jnp.sin(ang_c)], axis=1)
    cos64 = jnp.concatenate([jnp.ones((ctx_len, ATT_HD), F32), cos64], axis=0)
    sin64 = jnp.concatenate([jnp.zeros((ctx_len, ATT_HD), F32), sin64], axis=0)
    return jnp.tile(cos64, (1, ATT_HEADS)), jnp.tile(sin64, (1, ATT_HEADS))


def _pack_w_in(w):
    assert w.shape[-1] == W_IN_COLS
    return jnp.pad(w.astype(BF16), ((0, 0), (0, 0), (0, W_IN_PADDED - W_IN_COLS)))


def kernel(x, c, ctx, c_ctx, hg_lb_logits, router_w, router_b, final_g, w_mod, b_mod, g_mix, g_ffn,
           w_in, conv_w, conv_b, conv_ln_g, conv_ln_b, hg_norm_g, gla_w2, gla_b2, gla_norm_g, att_sink,
           w_br_conv, w_br_hg, w_br_gla, w_br_att, w_out, moe_w1, moe_w3, moe_w2):
    n_batch, n_lat, d = x.shape
    ctx_len = ctx.shape[1]
    depth = w_in.shape[0]
    assert ctx_len % TOK_TILE == 0 and n_lat % TOK_TILE == 0 and n_lat % GRID_W == 0
    seq = ctx_len + n_lat
    n_ctx_tiles = ctx_len // TOK_TILE

    xs = jnp.concatenate([ctx, x], axis=1).reshape(n_batch * seq, d)

    c_rows = jnp.zeros((8, d), F32).at[:n_batch].set(c).at[n_batch].set(c_ctx)
    mods = modulation(c_rows, w_mod, b_mod).reshape(depth, 8, N_MOD, d)
    modv = jnp.stack([mods[:, n_batch] if j % 2 == 0 else mods[:, j // 2] for j in range(2 * n_batch)], axis=1)

    lb_sm = jax.nn.softmax(hg_lb_logits.astype(F32), axis=0)
    lower = jnp.cumsum(lb_sm, axis=0) - lb_sm[0]
    cos_t, sin_t = _rope_tables(ctx_len, n_lat)
    ones_bd = jnp.asarray(np.arange(256)[:, None] // 64 == np.arange(256)[None, :] // 64, BF16)
    rw = jnp.zeros((d, LANES), F32).at[:, :N_EXPERTS].set(router_w.astype(F32))
    rb = jnp.zeros((1, LANES), F32).at[0, :N_EXPERTS].set(router_b.astype(F32))
    tri = jnp.asarray(np.tril(np.ones((TOK_TILE, TOK_TILE))), BF16)
    upper = jnp.asarray(np.triu(np.ones((LANES, LANES)), k=1), BF16)

    w_in_p = _pack_w_in(w_in)
    wb_conv, wb_hg, wb_gla, wb_att, wb_out = [w.astype(BF16) for w in (w_br_conv, w_br_hg, w_br_gla, w_br_att, w_out)]

    pending = None
    for l in range(depth):
        w2p = jnp.zeros((LANES, 2 * GLA_K), F32)
        w2p = w2p.at[0:GLA_RANK, 0:GLA_K].set(gla_w2[l, 0]).at[GLA_RANK:2 * GLA_RANK, GLA_K:].set(gla_w2[l, 1])
        b2p = gla_b2[l].reshape(1, 2 * GLA_K)
        proj = inproj(xs, modv[l], g_mix[l].reshape(1, d), w_in_p, l, lower[l], w2p, b2p, cos_t, sin_t,
                      n_batch, n_ctx_tiles, moe=pending)
        if pending is not None:
            xs, proj = proj[0], proj[1:]
        conv_z, hg_pack, gla_pack, att_pack, bgate = proj

        hg_f, hg_b = gated_scan(hg_pack, (0, 1, 3, 4), (0, 2, 3, 5), HG_W, n_batch, n_ctx_tiles, "scan_hgrn")
        gla_f, gla_b, att_o = gated_scan(gla_pack, (0, 1, 2, 2), (0, 1, 2, 3), GLA_K, n_batch, n_ctx_tiles,
                                         "scan_gla_attn", attn=(att_pack, att_sink[l].astype(F32), ctx_len))

        lw = dict(conv_w=jnp.zeros((32, CONV_CH), F32).at[:CONV_K].set(conv_w[l]),
                  conv_b=conv_b[l].reshape(1, -1), conv_ln_g=conv_ln_g[l].reshape(1, -1),
                  conv_ln_b=conv_ln_b[l].reshape(1, -1), hg_norm_g=hg_norm_g[l].reshape(1, -1),
                  gla_norm_g=gla_norm_g[l].reshape(1, -1), ones_bd=ones_bd,
                  w_br_conv=wb_conv, w_br_hg=wb_hg, w_br_gla=wb_gla, w_br_att=wb_att, w_out=wb_out,
                  g_ffn=g_ffn[l].reshape(1, d), router_w=rw, router_b=rb, tri=tri, upper=upper, layer=l)
        x_new, h2, slots, gran = merge(xs, modv[l], conv_z, hg_pack, hg_f, hg_b, gla_pack, gla_f, gla_b,
                                       att_o, bgate, lw, n_batch, n_ctx_tiles)

        n_rows = sorted_rows_bound(n_batch * seq)
        table, tails, block_expert, n_used = moe_tables(gran[:, 0, :N_EXPERTS], n_rows)
        x_sorted = dispatch(h2, slots, table, tails, n_rows)
        ys = experts(x_sorted, block_expert, n_used, moe_w1, moe_w3, moe_w2, l)
        if l < depth - 1:
            xs, pending = x_new, (modv[l], ys, table, slots)
        else:
            xs = combine(x_new, modv[l], ys, table, slots, final_g.reshape(1, d), n_batch, n_ctx_tiles, True)

    return xs.reshape(n_batch, n_lat, d)
```

```python
import functools
import itertools

import numpy as np
import jax
import jax.numpy as jnp
from jax import lax
from jax.experimental import pallas as pl
from jax.experimental.pallas import tpu as pltpu

F32 = jnp.float32
BF16 = jnp.bfloat16

EPS = 1e-6
NEG = -1e30
TINY = 1e-30
N_MOD = 6
CONV_CH = 256
CONV_K = 31
HG_HEADS = 4
HG_W = 256
GLA_HEADS = 4
GLA_K = 128
GLA_V = 256
GLA_RANK = 16
GLA_TAU = 16.0
ATT_HEADS = 4
ATT_KV_HEADS = 2
ATT_HD = 64
ATT_BLOCK = 128
GRID_W = 64
ROPE_BASE = 10000.0
N_EXPERTS = 16
N_GROUPS = 4
TOP_K = 2
MOE_BLOCK = 256

LANES = 128
TOK_TILE = 256
SCAN_BLOCK = 16
HALO = 16
GRANULE = 8
SORT_ROWS = -(-(TOP_K * TOK_TILE + N_EXPERTS * (GRANULE - 1)) // LANES) * LANES
V7X_VMEM_BYTES = 64 * 1024 * 1024
VMEM_LIMIT = V7X_VMEM_BYTES * 7 // 8

W_CONV = (0, 512)
W_HG = (512, 1792)
W_GLA_QKV = (1792, 2304)
W_TAIL = 2304
W_IN_COLS = 7200
W_IN_PADDED = 7296
W_GLA_RANK = (W_TAIL, W_TAIL + LANES)
W_SKEW = 2 * GLA_RANK
T_GLA_GATE = (0, 256)
T_ATT = (256, 768)
T_BG = (768, 4864)


def _cparams(sem):
    return pltpu.CompilerParams(dimension_semantics=sem, vmem_limit_bytes=VMEM_LIMIT)


def _dot(a, b):
    return jnp.dot(a, b, preferred_element_type=F32)


def _split(a):
    hi = a.astype(BF16)
    lo = (a - hi.astype(F32)).astype(BF16)
    return hi, lo


def _dot_f32(a, b):
    ah, al = _split(a)
    bh, bl = _split(b)
    return _dot(ah, bh) + _dot(ah, bl) + _dot(al, bh)


def _sigmoid(x):
    return 1.0 / (1.0 + jnp.exp(-x))


def _silu(x):
    return x * _sigmoid(x)


def _mod_kernel(c_ref, w_ref, b_ref, o_ref):
    c = c_ref[...]
    o_ref[0] = _dot_f32(_silu(c), w_ref[0]) + b_ref[0]


def modulation(c_rows, w_mod, b_mod):
    depth, d, six_d = w_mod.shape
    nblk = six_d // d
    return pl.pallas_call(
        _mod_kernel,
        grid=(depth, nblk),
        in_specs=[pl.BlockSpec((8, d), lambda l, j: (0, 0)),
                  pl.BlockSpec((1, d, d), lambda l, j: (l, 0, j)),
                  pl.BlockSpec((1, 1, d), lambda l, j: (l, 0, j))],
        out_specs=pl.BlockSpec((1, 8, d), lambda l, j: (l, 0, j)),
        out_shape=jax.ShapeDtypeStruct((depth, 8, six_d), F32),
        compiler_params=_cparams(("parallel", "parallel")),
        name="modulation",
    )(c_rows, w_mod, b_mod.reshape(depth, 1, six_d))


def _modnorm(x, g, shift, scale):
    y = x * lax.rsqrt(jnp.mean(x * x, axis=-1, keepdims=True) + EPS)
    return (y * g) * (1.0 + scale) + shift


def _inproj_kernel(*refs, after_moe):
    if after_moe:
        (d_ref, dn_ref, x_ref, modp_ref, slot_ref, y_hbm, mod_ref, g_ref, w_ref, lb_ref, w2_ref, b2_ref,
         cos_ref, sin_ref, xo_ref, conv_ref, hg_ref, gla_ref, att_ref, bg_ref, tail_ref, ybuf, sem) = refs
    else:
        (x_ref, mod_ref, g_ref, w_ref, lb_ref, w2_ref, b2_ref, cos_ref, sin_ref,
         conv_ref, hg_ref, gla_ref, att_ref, bg_ref, tail_ref) = refs

    @pl.when(pl.program_id(0) == 0)
    def _():
        width = tail_ref.shape[1]
        tail_ref[...] = pltpu.roll(w_ref[:, W_TAIL:W_TAIL + width], width - W_SKEW, axis=1)

    x = x_ref[...]
    if after_moe:
        x = x + modp_ref[0, 5:6, :] * _expert_mix(d_ref, dn_ref, slot_ref, y_hbm, ybuf, sem)
        xo_ref[...] = x
    h = _modnorm(x, g_ref[...], mod_ref[0, 0:1, :], mod_ref[0, 1:2, :]).astype(BF16)

    p = _dot(h, w_ref[:, W_CONV[0]:W_CONV[1]])
    conv_ref[...] = p[:, :CONV_CH] * _sigmoid(p[:, CONV_CH:])

    p = _dot(h, w_ref[:, W_HG[0]:W_HG[1]])
    hg_ref[:, 0:256] = p[:, 0:256]
    hg_ref[:, 768:1024] = p[:, 768:1024]
    hg_ref[:, 1536:1792] = _silu(p[:, 1024:1280])
    for d in range(2):
        z = p[:, 256 * (d + 1):256 * (d + 2)]
        lb = lb_ref[d:d + 1, :]
        hg_ref[:, 256 * (d + 1):256 * (d + 2)] = (1.0 - lb) * _sigmoid(-z)
        hg_ref[:, 256 * (d + 4):256 * (d + 5)] = jnp.maximum(lb + (1.0 - lb) * _sigmoid(z), TINY)

    p = _dot(h, w_ref[:, W_GLA_QKV[0]:W_GLA_QKV[1]])
    gla_ref[:, 0:128] = p[:, 0:128] * (float(GLA_K // GLA_HEADS) ** -0.5)
    gla_ref[:, 128:256] = p[:, 128:256]
    gla_ref[:, 512:768] = p[:, 256:512]
    gla_ref[:, 768:1024] = _silu(_dot(h, tail_ref[:, T_GLA_GATE[0]:T_GLA_GATE[1]]))
    rank = _dot(h, w_ref[:, W_GLA_RANK[0]:W_GLA_RANK[1]])
    u = _dot_f32(rank, w2_ref[...]) + b2_ref[...]
    log_sig = jnp.minimum(u, 0.0) - jnp.log(1.0 + jnp.exp(-jnp.abs(u)))
    gla_ref[:, 256:512] = jnp.exp(log_sig * (1.0 / GLA_TAU))

    p = _dot(h, tail_ref[:, T_ATT[0]:T_ATT[1]])
    cos = cos_ref[...]
    sin = sin_ref[...]
    def rope(v, width):
        lane = lax.broadcasted_iota(jnp.int32, v.shape, 1)
        partner = jnp.where((lane & 31) < 16,
                            pltpu.roll(v, width - 16, axis=1), pltpu.roll(v, 16, axis=1))
        return v * cos[:, :width] + partner * sin[:, :width]

    att_ref[:, 0:256] = rope(p[:, 0:256], 256) * (float(ATT_HD) ** -0.5)
    att_ref[:, 256:384] = rope(p[:, 256:384], 128)
    att_ref[:, 384:512] = p[:, 384:512]

    p = _dot(h, tail_ref[:, T_BG[0]:T_BG[1]])
    bg_ref[...] = _sigmoid(p).astype(BF16)


def inproj(x, modv, g_mix, w_in_p, layer, lb, w2p, b2p, cos_t, sin_t, n_batch, n_ctx_tiles, moe=None):
    n, d = x.shape
    nt = n // TOK_TILE
    tiles_per_batch = nt // n_batch

    def mod_idx(i):
        b = i // tiles_per_batch
        return b * 2 + ((i % tiles_per_batch) >= n_ctx_tiles).astype(jnp.int32)

    const = lambda i: (0, 0)
    row = lambda i: (i, 0)
    seq = lambda i: (i % tiles_per_batch, 0)
    pre_specs, pre_ops, pre_out_specs, pre_out_shapes, pre_scratch = [], [], [], [], []
    if moe is not None:
        modv_prev, ys, table, slots = moe
        pre_specs = [pl.BlockSpec((1, 1, LANES), lambda i: (i, 0, 0), memory_space=pltpu.SMEM),
                     pl.BlockSpec((1, 1, LANES), lambda i: (jnp.minimum(i + 1, nt - 1), 0, 0),
                                  memory_space=pltpu.SMEM)]
        pre_ops = [table, table]
        pre_out_specs = [pl.BlockSpec((TOK_TILE, d), row)]
        pre_out_shapes = [jax.ShapeDtypeStruct((n, d), F32)]
        pre_scratch = [pltpu.VMEM((2, SORT_ROWS, d), F32), pltpu.SemaphoreType.DMA((2,))]
    mid_specs, mid_ops = [], []
    if moe is not None:
        mid_specs = [pl.BlockSpec((1, N_MOD, d), lambda i: (mod_idx(i), 0, 0)),
                     pl.BlockSpec((TOK_TILE, LANES), row),
                     pl.BlockSpec(memory_space=pl.ANY)]
        mid_ops = [modv_prev, slots, ys]
    outs = pl.pallas_call(
        functools.partial(_inproj_kernel, after_moe=moe is not None),
        grid=(nt,),
        in_specs=pre_specs + [pl.BlockSpec((TOK_TILE, d), row)] + mid_specs + [
                  pl.BlockSpec((1, N_MOD, d), lambda i: (mod_idx(i), 0, 0)),
                  pl.BlockSpec((1, d), const),
                  pl.BlockSpec((None, d, W_IN_PADDED), lambda i: (layer, 0, 0), pipeline_mode=pl.Buffered(1)),
                  pl.BlockSpec((2, HG_W), const),
                  pl.BlockSpec((LANES, 2 * GLA_K), const),
                  pl.BlockSpec((1, 2 * GLA_K), const),
                  pl.BlockSpec((TOK_TILE, 256), seq),
                  pl.BlockSpec((TOK_TILE, 256), seq)],
        out_specs=pre_out_specs + [pl.BlockSpec((TOK_TILE, 256), row),
                                   pl.BlockSpec((TOK_TILE, 1792), row),
                                   pl.BlockSpec((TOK_TILE, 1024), row),
                                   pl.BlockSpec((TOK_TILE, 512), row),
                                   pl.BlockSpec((TOK_TILE, 4096), row)],
        out_shape=pre_out_shapes + [jax.ShapeDtypeStruct((n, 256), F32),
                                    jax.ShapeDtypeStruct((n, 1792), F32),
                                    jax.ShapeDtypeStruct((n, 1024), F32),
                                    jax.ShapeDtypeStruct((n, 512), F32),
                                    jax.ShapeDtypeStruct((n, 4096), BF16)],
        scratch_shapes=[pltpu.VMEM((d, W_IN_PADDED - W_TAIL), BF16)] + pre_scratch,
        compiler_params=_cparams(("arbitrary",)),
        name="combine_inproj" if moe is not None else "inproj",
    )(*pre_ops, x, *mid_ops, modv, g_mix, w_in_p, lb, w2p, b2p, cos_t, sin_t)
    return outs


SCAN_IN = 4
SCAN_SCRATCH = 12


ATTN_IN = 11
ATTN_STRIDE = 3


def _trace_alternately(chains):
    live = list(chains)
    rnd = 0
    while live:
        for entry in list(live):
            gen, stride = entry
            if rnd % stride == 0:
                try:
                    next(gen)
                except StopIteration:
                    live.remove(entry)
        rnd += 1


def _scan_kernel(*refs, attn):
    if attn is not None:
        sink_ref, refs = refs[0], refs[1:]
    n_chain = 2
    n_in = n_chain * SCAN_IN
    ins = refs[:n_in]
    sel = refs[n_in]
    perms = refs[n_in + 1:n_in + 5]
    pos = n_in + 5
    if attn is not None:
        att_in = refs[pos:pos + ATTN_IN]
        pos += ATTN_IN
    outs = refs[pos:pos + n_chain]
    pos += n_chain
    if attn is not None:
        att_out = refs[pos]
        pos += 1
    scratch = refs[pos:]
    chains = []
    for i in range(n_chain):
        rev = i % 2 == 1
        chains.append((_scan_direction(*ins[i * SCAN_IN:(i + 1) * SCAN_IN], sel,
                                       perms[2 * rev], perms[2 * rev + 1], outs[i],
                                       *scratch[i * SCAN_SCRATCH:(i + 1) * SCAN_SCRATCH], rev=rev), 1))
    if attn is not None:
        ctx_blocks, blocks_per_batch = attn
        q_ref, kc_ref, vc_ref = att_in[:3]
        kb, vb = att_in[3:7], att_in[7:11]
        per_tile = q_ref.shape[0] // ATT_BLOCK
        for a in range(per_tile):
            rows = slice(a * ATT_BLOCK, (a + 1) * ATT_BLOCK)

            def write(hk, x, rows=rows):
                att_out[rows, hk * LANES:(hk + 1) * LANES] = x

            gens = _attn_chains(pl.program_id(1) * per_tile + a, sink_ref, q_ref[rows, :],
                                [kc_ref[...]] + [kb[a + t][...] for t in range(3)],
                                [vc_ref[...]] + [vb[a + t][...] for t in range(3)],
                                write, ctx_blocks, blocks_per_batch)
            chains += [(g, ATTN_STRIDE) for g in gens]
    _trace_alternately(chains)


def _scan_direction(q_ref, k_ref, v_ref, f_ref, sel_ref, perm_ref, permt_ref, o_ref,
                    s_ref, qs_ref, ks_ref, vs_ref, fs_ref, qt_ref, kh_ref, term_ref, w_ref, ah_ref, al_ref,
                    kv_ref, *, rev):
    c = SCAN_BLOCK
    tt, kl = q_ref.shape
    nb = tt // c
    assert nb == c

    @pl.when(pl.program_id(1) == 0)
    def _():
        s_ref[...] = jnp.zeros_like(s_ref)

    perm = perm_ref[...]
    vb = v_ref[...].astype(BF16)
    f_hi, f_lo = _split(f_ref[...])
    qs_ref[...] = _dot(perm, q_ref[...].astype(BF16))
    ks_ref[...] = _dot(perm, k_ref[...].astype(BF16))
    vs_ref[...] = _dot(perm, vb)
    fs_ref[...] = _dot(perm, f_hi) + _dot(perm, f_lo)
    yield

    def slab(ref, s):
        return ref[s * nb:(s + 1) * nb, :]

    a = slab(fs_ref, 0)
    qt_ref[0:nb, :] = (slab(qs_ref, 0) * a).astype(BF16)
    for s in range(1, c):
        a = a * slab(fs_ref, s)
        qt_ref[s * nb:(s + 1) * nb, :] = (slab(qs_ref, s) * a).astype(BF16)
    gam_t = a.T
    g = jnp.ones((nb, kl), F32)
    for s in range(c - 1, -1, -1):
        kh_ref[s * nb:(s + 1) * nb, :] = (slab(ks_ref, s) * g).astype(BF16)
        g = g * slab(fs_ref, s)
    yield

    heads = 256 // 64
    dk = kl // heads
    perm_t = permt_ref[...]
    qt = _dot(perm_t, qt_ref[...]).astype(BF16)
    kh = _dot(perm_t, kh_ref[...]).astype(BF16)

    def head_lanes(x, h):
        return x[:, :dk] if h == 0 else pltpu.roll(x, kl - dk * h, axis=1)[:, :dk]

    q_heads = [head_lanes(qt, h) for h in range(heads)]
    k_heads = [head_lanes(kh, h) for h in range(heads)]
    v_head = lax.broadcasted_iota(jnp.int32, (c, 256), 1) // 64
    g_head = lax.broadcasted_iota(jnp.int32, (dk, 256), 1) // 64
    for j in range(nb):
        lo = j * c
        k4 = jnp.concatenate([k_heads[h][lo:lo + c, :] for h in range(heads)], axis=0)
        v4 = jnp.concatenate([jnp.where(v_head == h, vb[lo:lo + c, :], jnp.zeros_like(vb[lo:lo + c, :]))
                              for h in range(heads)], axis=0)
        kv_ref[j] = lax.dot_general(k4, v4, (((0,), (0,)), ((), ())), preferred_element_type=F32)
        yield

    n_pair = 0
    for s in range(c):
        p = slab(qs_ref, s)
        for d in range(s + 1):
            if d > 0:
                p = p * slab(fs_ref, s - d + 1)
            term_ref[n_pair * nb:(n_pair + 1) * nb, :] = (p * slab(ks_ref, s - d)).astype(BF16)
            n_pair += 1
        yield
    w_ref[...] = _dot(term_ref[...], sel_ref[...])
    yield
    n_pair = 0
    for s in range(c):
        acc = jnp.zeros((nb, 256), F32)
        for d in range(s + 1):
            acc = acc + w_ref[n_pair * nb:(n_pair + 1) * nb, :] * slab(vs_ref, s - d)
            n_pair += 1
        hi, lo = _split(acc)
        ah_ref[s * nb:(s + 1) * nb, :] = hi
        al_ref[s * nb:(s + 1) * nb, :] = lo
        yield

    o_ref[...] = _dot(perm_t, ah_ref[...]) + _dot(perm_t, al_ref[...])
    yield
    st = s_ref[...]
    for jj in range(nb):
        j = nb - 1 - jj if rev else jj
        lo = j * c
        q4 = jnp.concatenate([q_heads[h][lo:lo + c, :] for h in range(heads)], axis=0)
        o4 = _dot(q4, st.astype(BF16))
        inter = o4[0:c, :]
        for h in range(1, heads):
            inter = jnp.where(v_head == h, o4[h * c:(h + 1) * c, :], inter)
        o_ref[lo:lo + c, :] += inter
        gam = jnp.broadcast_to(gam_t[0:dk, j:j + 1], (dk, 256))
        for h in range(1, heads):
            gam = jnp.where(g_head == h, gam_t[h * dk:(h + 1) * dk, j:j + 1], gam)
        st = gam * st + kv_ref[j]
        yield
    s_ref[...] = st


def gated_scan(pack, cols_fwd, cols_rev, kl, n_batch, n_ctx_tiles, name, attn=None):
    n = pack.shape[0]
    nt = n // TOK_TILE
    tpb = nt // n_batch

    def rev_tile(i):
        return jnp.where(i < n_ctx_tiles, n_ctx_tiles - 1 - i, tpb - 1 - (i - n_ctx_tiles))

    heads = 4
    c = SCAN_BLOCK
    nb = TOK_TILE // c
    perms = []
    for rev in (False, True):
        perm = np.zeros((TOK_TILE, TOK_TILE), np.float32)
        for j in range(nb):
            for s in range(c):
                perm[s * nb + j, j * c + (c - 1 - s if rev else s)] = 1.0
        perms += [jnp.asarray(perm, BF16), jnp.asarray(perm.T, BF16)]
    n_pairs = c * (c + 1) // 2
    fwd = lambda col: (lambda b, i, *_: (b * tpb + i, col))
    bwd = lambda col: (lambda b, i, *_: (b * tpb + rev_tile(i), col))
    const = lambda b, i, *_: (0, 0)

    def direction_specs(rows, cols, kl):
        qc, kc, vc, fc = cols
        specs = [pl.BlockSpec((TOK_TILE, kl), rows(qc)),
                 pl.BlockSpec((TOK_TILE, kl), rows(kc)),
                 pl.BlockSpec((TOK_TILE, 256), rows(vc)),
                 pl.BlockSpec((TOK_TILE, kl), rows(fc))]
        assert len(specs) == SCAN_IN
        return specs

    def chain_scratch(kl):
        dk = kl // heads
        shapes = [pltpu.VMEM((dk, 256), F32),
                  pltpu.VMEM((TOK_TILE, kl), F32),
                  pltpu.VMEM((TOK_TILE, kl), F32),
                  pltpu.VMEM((TOK_TILE, 256), F32),
                  pltpu.VMEM((TOK_TILE, kl), F32),
                  pltpu.VMEM((TOK_TILE, kl), BF16),
                  pltpu.VMEM((TOK_TILE, kl), BF16),
                  pltpu.VMEM((n_pairs * nb, kl), BF16),
                  pltpu.VMEM((n_pairs * nb, 256), F32),
                  pltpu.VMEM((TOK_TILE, 256), BF16),
                  pltpu.VMEM((TOK_TILE, 256), BF16),
                  pltpu.VMEM((nb, dk, 256), F32)]
        assert len(shapes) == SCAN_SCRATCH
        return shapes

    dk = kl // heads
    sel = jnp.asarray(np.arange(kl)[:, None] // dk == np.arange(256)[None, :] // 64, BF16)
    in_specs = direction_specs(fwd, cols_fwd, kl) + direction_specs(bwd, cols_rev, kl)
    in_specs += [pl.BlockSpec(sel.shape, const)] + [pl.BlockSpec((TOK_TILE, TOK_TILE), const)] * 4
    operands = [pack] * (2 * SCAN_IN) + [sel] + perms
    out_specs = [pl.BlockSpec((TOK_TILE, 256), fwd(0)), pl.BlockSpec((TOK_TILE, 256), bwd(0))]
    prefetch, kernel_attn = [], None
    if attn is not None:
        att_pack, sink, ctx_len = attn
        per_tile = TOK_TILE // ATT_BLOCK
        bpb = tpb * per_tile
        cb = ctx_len // ATT_BLOCK

        def band(off, col):
            def idx(b, i, *_):
                return (b * bpb + jnp.clip(i * per_tile + off, cb, bpb - 1), col)
            return pl.BlockSpec((ATT_BLOCK, LANES), idx)

        def ctx(col):
            return pl.BlockSpec((ctx_len, LANES), lambda b, i, *_: (b * (bpb * ATT_BLOCK // ctx_len), col))

        att_specs = ([pl.BlockSpec((TOK_TILE, 256), fwd(0)), ctx(2), ctx(3)]
                     + [band(off, 2) for off in range(-1, per_tile + 1)]
                     + [band(off, 3) for off in range(-1, per_tile + 1)])
        assert len(att_specs) == ATTN_IN
        in_specs += att_specs
        operands += [att_pack] * ATTN_IN
        out_specs.append(pl.BlockSpec((TOK_TILE, 256), fwd(0)))
        prefetch, kernel_attn = [sink], (cb, bpb)
    grid_spec = pltpu.PrefetchScalarGridSpec(
        num_scalar_prefetch=len(prefetch),
        grid=(n_batch, tpb),
        in_specs=in_specs,
        out_specs=out_specs,
        scratch_shapes=chain_scratch(kl) + chain_scratch(kl),
    )
    return pl.pallas_call(
        functools.partial(_scan_kernel, attn=kernel_attn),
        grid_spec=grid_spec,
        out_shape=[jax.ShapeDtypeStruct((n, 256), F32)] * len(out_specs),
        compiler_params=_cparams(("parallel", "arbitrary")),
        name=name,
    )(*prefetch, *operands)


def _attn_chains(n, sink_ref, q, keys, vals, write, ctx_blocks, blocks_per_batch):
    blk = ATT_BLOCK
    lane = lax.broadcasted_iota(jnp.int32, (blk, LANES), 1)
    low = lane < ATT_HD
    qi = lax.broadcasted_iota(jnp.int32, (blk, blk), 0)
    ki = lax.broadcasted_iota(jnp.int32, (blk, blk), 1)
    is_lat = n >= ctx_blocks
    band_ok = [
        jnp.logical_and(jnp.logical_and(is_lat, n - 1 >= ctx_blocks), qi <= ki),
        jnp.logical_and(is_lat, qi >= 0),
        jnp.logical_and(jnp.logical_and(is_lat, n + 1 < blocks_per_batch), ki <= qi),
    ]

    def dup(x, hk):
        xr = pltpu.roll(x, ATT_HD, axis=1)
        lo_x = lax.broadcasted_iota(jnp.int32, x.shape, 1) < ATT_HD
        return (jnp.where(lo_x, x, xr) if hk == 0 else jnp.where(lo_x, xr, x)).astype(BF16)

    nt_dims = (((1,), (1,)), ((), ()))

    def kv_group(hk):
        kd = [dup(x, hk) for x in keys]
        vd = [dup(x, hk) for x in vals]
        qh = q[:, hk * LANES:(hk + 1) * LANES]
        q2 = jnp.concatenate([jnp.where(low, qh, 0.0), jnp.where(low, 0.0, qh)], axis=0).astype(BF16)
        yield
        s = [lax.dot_general(q2, kx, nt_dims, preferred_element_type=F32) for kx in kd]
        for t in range(3):
            s[t + 1] = jnp.where(jnp.concatenate([band_ok[t]] * 2, axis=0), s[t + 1], NEG)
        yield
        sink = jnp.where(lax.broadcasted_iota(jnp.int32, (2 * blk, 1), 0) < blk,
                         sink_ref[hk * 2], sink_ref[hk * 2 + 1])
        pieces = [s[0][:, :LANES], s[0][:, LANES:], s[1], s[2], s[3]]
        top = pieces[0]
        for piece in pieces[1:]:
            top = jnp.maximum(top, piece)
        m = jnp.maximum(sink, top.max(axis=-1, keepdims=True))
        yield
        e = [jnp.exp(x - m) for x in s]
        tot = e[0][:, :LANES] + e[0][:, LANES:] + e[1] + e[2] + e[3]
        den = jnp.exp(sink - m) + tot.sum(axis=-1, keepdims=True)
        yield
        o = _dot(e[0].astype(BF16), vd[0])
        for t in range(1, 4):
            o = o + _dot(e[t].astype(BF16), vd[t])
        yield
        o = o / den
        write(hk, jnp.where(low, o[:blk, :], o[blk:, :]))

    return [kv_group(hk) for hk in range(ATT_KV_HEADS)]


def _head_norm(o, ones_bd, g):
    sq = o * o
    hi, lo = _split(sq)
    ms = (_dot(hi, ones_bd) + _dot(lo, ones_bd)) * (1.0 / 64.0)
    return o * lax.rsqrt(ms + EPS) * g


def _route_select(aff, bias):
    lane = lax.broadcasted_iota(jnp.int32, aff.shape, 1)
    epg = N_EXPERTS // N_GROUPS
    pos = lane & (epg - 1)
    v = aff + bias

    def nxt(x, o):
        return pltpu.roll(x, LANES - o, axis=1)

    def prv(x, o):
        return pltpu.roll(x, o, axis=1)

    beaten = jnp.zeros(aff.shape, jnp.int32)
    for o in range(1, epg):
        beaten = beaten + jnp.where(jnp.logical_and(pos + o < epg, nxt(v, o) > v), 1, 0)
        beaten = beaten + jnp.where(jnp.logical_and(pos >= o, prv(v, o) >= v), 1, 0)
    top2 = beaten < TOP_K
    t = jnp.where(top2, v, 0.0)
    score = t
    for o in range(1, epg):
        score = score + jnp.where(pos + o < epg, nxt(t, o), 0.0) + jnp.where(pos >= o, prv(t, o), 0.0)
    worse = jnp.zeros(aff.shape, jnp.int32)
    for o in range(epg, N_EXPERTS, epg):
        worse = worse + jnp.where(jnp.logical_and(lane + o < N_EXPERTS, nxt(score, o) > score), 1, 0)
        worse = worse + jnp.where(jnp.logical_and(lane >= o, prv(score, o) >= score), 1, 0)
    sel = jnp.logical_and(jnp.logical_and(lane < N_EXPERTS, worse == 0), top2)
    picked = jnp.where(sel, aff, 0.0)
    gate = picked / jnp.sum(picked, axis=-1, keepdims=True)
    return gate, sel


def _merge_kernel(x_ref, mod_ref, zp_ref, z_ref, zn_ref, hgf_ref, hgb_ref, hgg_ref, glf_ref, glb_ref,
                  glg_ref, att_ref, bg_ref, cw_ref, cb_ref, lng_ref, lnb_ref, hgn_ref, gln_ref,
                  ones_ref, wbc_ref, wbh_ref, wbg_ref, wba_ref, wo_ref, gf_ref, rw_ref, rb_ref, tri_ref,
                  upper_ref, xo_ref, h2_ref, slot_ref, gran_ref, zs_ref, zsh_ref, *, tiles_per_batch,
                  n_ctx_tiles):
    i = pl.program_id(0)
    tt = x_ref.shape[0]
    ti = i % tiles_per_batch
    has_prev = jnp.logical_and(ti != 0, ti != n_ctx_tiles)
    has_next = jnp.logical_and(ti != n_ctx_tiles - 1, ti != tiles_per_batch - 1)

    zs_ref[0:HALO, :] = jnp.where(has_prev, zp_ref[...], 0.0)
    zs_ref[HALO:HALO + tt, :] = z_ref[...]
    zs_ref[HALO + tt:HALO + tt + HALO, :] = jnp.where(has_next, zn_ref[...], 0.0)
    span = tt + 2 * HALO - 8
    for b in range(8):
        zsh_ref[b] = zs_ref[b:b + span, :]
    ones_bd = ones_ref[...]
    d = x_ref.shape[1]
    n_part = 2
    part = tt // n_part
    routed = [None] * n_part

    def rows_chain(k):
        r = slice(k * part, (k + 1) * part)
        acc = jnp.zeros((part, CONV_CH), F32) + cb_ref[...]
        for j in range(CONV_K):
            off = HALO - CONV_K // 2 + j + k * part
            acc = acc + zsh_ref[off % 8, off - off % 8:off - off % 8 + part, :] * cw_ref[j:j + 1, :]
            if j % 8 == 7:
                yield
        mu = jnp.mean(acc, axis=-1, keepdims=True)
        cen = acc - mu
        var = jnp.mean(cen * cen, axis=-1, keepdims=True)
        conv_y = _silu(cen * lax.rsqrt(var + EPS) * lng_ref[...] + lnb_ref[...])
        yield
        hg_y = _head_norm(hgf_ref[r, :] + hgb_ref[r, :], ones_bd, hgn_ref[...]) * hgg_ref[r, :]
        yield
        gla_y = _head_norm(glf_ref[r, :] + glb_ref[r, :], ones_bd, gln_ref[...]) * glg_ref[r, :]
        yield
        merged = bg_ref[r, 0:d].astype(F32) * _dot(conv_y.astype(BF16), wbc_ref[...])
        yield
        merged = merged + bg_ref[r, d:2 * d].astype(F32) * _dot(hg_y.astype(BF16), wbh_ref[...])
        yield
        merged = merged + bg_ref[r, 2 * d:3 * d].astype(F32) * _dot(gla_y.astype(BF16), wbg_ref[...])
        yield
        merged = merged + bg_ref[r, 3 * d:4 * d].astype(F32) * _dot(att_ref[r, :].astype(BF16), wba_ref[...])
        yield
        mix = _dot(merged.astype(BF16), wo_ref[...])
        yield
        x_new = x_ref[r, :] + mod_ref[0, 2:3, :] * mix
        xo_ref[r, :] = x_new
        h2 = _modnorm(x_new, gf_ref[...], mod_ref[0, 3:4, :], mod_ref[0, 4:5, :])
        h2_ref[r, :] = h2
        yield
        routed[k] = _route_select(_sigmoid(_dot_f32(h2, rw_ref[...])), rb_ref[...])

    _trace_alternately([(rows_chain(k), 1) for k in range(n_part)])
    gate = jnp.concatenate([g for g, _ in routed], axis=0)
    sel = jnp.concatenate([s for _, s in routed], axis=0)

    incl = _dot(tri_ref[...], sel.astype(F32).astype(BF16))
    gran = jnp.floor((incl[tt - 1:tt, :] + (GRANULE - 1.0)) * (1.0 / GRANULE))
    gran8 = jnp.broadcast_to(gran, (8, LANES))
    start = _dot(gran8.astype(BF16), upper_ref[...])[0:1, :]
    pos = GRANULE * start + incl - 1.0
    p0 = jnp.min(jnp.where(sel, pos, 1e9), axis=-1, keepdims=True)
    p1 = jnp.max(jnp.where(sel, pos, -1.0), axis=-1, keepdims=True)
    g0 = jnp.sum(jnp.where(jnp.logical_and(sel, pos == p0), gate, 0.0), axis=-1, keepdims=True)
    g1 = jnp.sum(jnp.where(jnp.logical_and(sel, pos == p1), gate, 0.0), axis=-1, keepdims=True)
    lane = lax.broadcasted_iota(jnp.int32, (tt, LANES), 1)
    slot_ref[...] = jnp.where(lane == 0, p0, jnp.where(lane == 1, p1, jnp.where(lane == 2, g0,
                              jnp.where(lane == 3, g1, 0.0))))
    gran_ref[0] = gran8.astype(jnp.int32)


def merge(x, modv, conv_z, hg_pack, hg_f, hg_b, gla_pack, gla_f, gla_b, att_o, bgate, lw, n_batch,
          n_ctx_tiles):
    n, d = x.shape
    nt = n // TOK_TILE
    tpb = nt // n_batch
    hpt = TOK_TILE // HALO
    n_halo = n // HALO

    def mod_idx(i):
        return (i // tpb) * 2 + ((i % tpb) >= n_ctx_tiles).astype(jnp.int32)

    row = lambda i: (i, 0)
    const = lambda i: (0, 0)
    col = lambda c: (lambda i: (i, c))
    layer = lw["layer"]

    def full(a):
        if a.ndim == 3:
            return pl.BlockSpec((None,) + a.shape[1:], lambda i: (layer, 0, 0))
        return pl.BlockSpec(a.shape, const)

    weights = [lw["conv_w"], lw["conv_b"], lw["conv_ln_g"], lw["conv_ln_b"], lw["hg_norm_g"],
               lw["gla_norm_g"], lw["ones_bd"], lw["w_br_conv"], lw["w_br_hg"], lw["w_br_gla"],
               lw["w_br_att"], lw["w_out"], lw["g_ffn"], lw["router_w"], lw["router_b"], lw["tri"],
               lw["upper"]]
    return pl.pallas_call(
        functools.partial(_merge_kernel, tiles_per_batch=tpb, n_ctx_tiles=n_ctx_tiles),
        grid=(nt,),
        in_specs=[pl.BlockSpec((TOK_TILE, d), row),
                  pl.BlockSpec((1, N_MOD, d), lambda i: (mod_idx(i), 0, 0)),
                  pl.BlockSpec((HALO, CONV_CH), lambda i: (jnp.maximum(i * hpt - 1, 0), 0)),
                  pl.BlockSpec((TOK_TILE, CONV_CH), row),
                  pl.BlockSpec((HALO, CONV_CH), lambda i: (jnp.minimum((i + 1) * hpt, n_halo - 1), 0)),
                  pl.BlockSpec((TOK_TILE, 256), row),
                  pl.BlockSpec((TOK_TILE, 256), row),
                  pl.BlockSpec((TOK_TILE, 256), col(6)),
                  pl.BlockSpec((TOK_TILE, 256), row),
                  pl.BlockSpec((TOK_TILE, 256), row),
                  pl.BlockSpec((TOK_TILE, 256), col(3)),
                  pl.BlockSpec((TOK_TILE, 256), row),
                  pl.BlockSpec((TOK_TILE, 4 * d), row)] + [full(w) for w in weights],
        out_specs=[pl.BlockSpec((TOK_TILE, d), row),
                   pl.BlockSpec((TOK_TILE, d), row),
                   pl.BlockSpec((TOK_TILE, LANES), row),
                   pl.BlockSpec((1, 8, LANES), lambda i: (i, 0, 0))],
        out_shape=[jax.ShapeDtypeStruct((n, d), F32),
                   jax.ShapeDtypeStruct((n, d), F32),
                   jax.ShapeDtypeStruct((n, LANES), F32),
                   jax.ShapeDtypeStruct((nt, 8, LANES), jnp.int32)],
        scratch_shapes=[pltpu.VMEM((TOK_TILE + 2 * HALO, CONV_CH), F32),
                        pltpu.VMEM((8, TOK_TILE + 2 * HALO - 8, CONV_CH), F32)],
        compiler_params=_cparams(("parallel",)),
        name="merge",
    )(x, modv, conv_z, conv_z, conv_z, hg_f, hg_b, hg_pack, gla_f, gla_b, gla_pack, att_o, bgate,
      *weights)


def _granule_copy(src, src_row, dst, dst_row, sem, granules=1):
    rows = granules * GRANULE
    return pltpu.make_async_copy(src.at[pl.ds(pl.multiple_of(src_row, GRANULE), rows)],
                                 dst.at[pl.ds(pl.multiple_of(dst_row, GRANULE), rows)], sem)


RUN_CHUNKS = (8, 4, 2, 1)
MAX_RUN = TOK_TILE // GRANULE


def _for_each_run_chunk(tab_ref, fn):
    big = RUN_CHUNKS[0]
    for e in range(N_EXPERTS):
        local = tab_ref[0, 0, e]
        glob = tab_ref[0, 0, N_EXPERTS + e]
        n = tab_ref[0, 0, 2 * N_EXPERTS + e]
        for k in range(MAX_RUN // big):
            @pl.when(n >= big * (k + 1))
            def _(k=k):
                fn(local + big * k, glob + big * k, big)
        off = n - n % big
        for size in RUN_CHUNKS[1:]:
            @pl.when((n & size) != 0)
            def _(off=off, size=size):
                fn(local + off, glob + off, size)
            off = off + (n & size)


def _slot_matrix(slot_ref, width, weighted):
    tt = slot_ref.shape[0]
    col = lax.broadcasted_iota(jnp.int32, (tt, width), 1).astype(F32)
    hit0 = col == slot_ref[:, 0:1]
    hit1 = col == slot_ref[:, 1:2]
    if not weighted:
        return jnp.where(jnp.logical_or(hit0, hit1), 1.0, 0.0)
    return jnp.where(hit0, slot_ref[:, 2:3], 0.0) + jnp.where(hit1, slot_ref[:, 3:4], 0.0)


def _dispatch_kernel(tab_ref, tabp_ref, tail_ref, h_ref, slot_ref, xs_hbm, buf, zbuf, sem, zsem, bsem):
    i = pl.program_id(0)
    nb = pl.num_programs(0)
    slot = i % 2

    perm = _slot_matrix(slot_ref, buf.shape[1], weighted=False).astype(BF16)
    buf[slot] = lax.dot_general(perm, h_ref[...].astype(BF16), (((0,), (0,)), ((), ())),
                                preferred_element_type=F32)

    def chunk(s):
        def copy(local, glob, granules):
            return _granule_copy(buf.at[s], local * GRANULE, xs_hbm, glob * GRANULE, sem.at[s], granules)
        return copy

    _for_each_run_chunk(tab_ref, lambda *a: chunk(slot)(*a).start())

    @pl.when(i > 0)
    def _():
        _for_each_run_chunk(tabp_ref, lambda *a: chunk(1 - slot)(*a).wait())

    @pl.when(i == nb - 1)
    def _():
        _for_each_run_chunk(tab_ref, lambda *a: chunk(slot)(*a).wait())

    @pl.when(i == 0)
    def _():
        zbuf[...] = jnp.zeros_like(zbuf)
        n_used = tail_ref[0, 0, 2 * N_EXPERTS]
        n_blocks = xs_hbm.shape[0] // MOE_BLOCK

        def block_copy(b):
            return pltpu.make_async_copy(
                zbuf, xs_hbm.at[pl.ds(pl.multiple_of(b * MOE_BLOCK, MOE_BLOCK), MOE_BLOCK)], bsem)

        for e in range(N_EXPERTS):
            def fill(m, carry, e=e):
                _granule_copy(zbuf, 0, xs_hbm, (tail_ref[0, 0, e] + m) * GRANULE, zsem).start()
                return carry
            lax.fori_loop(0, tail_ref[0, 0, N_EXPERTS + e], fill, 0)

        def fill_block(b, carry):
            block_copy(b).start()
            return carry
        lax.fori_loop(n_used, n_blocks, fill_block, 0)

        for e in range(N_EXPERTS):
            def done(m, carry):
                _granule_copy(zbuf, 0, xs_hbm, 0, zsem).wait()
                return carry
            lax.fori_loop(0, tail_ref[0, 0, N_EXPERTS + e], done, 0)

        def done_block(b, carry):
            block_copy(b).wait()
            return carry
        lax.fori_loop(n_used, n_blocks, done_block, 0)


def dispatch(h2, slots, table, tails, n_rows):
    n, d = h2.shape
    nt = n // TOK_TILE
    return pl.pallas_call(
        _dispatch_kernel,
        grid=(nt,),
        in_specs=[pl.BlockSpec((1, 1, LANES), lambda i: (i, 0, 0), memory_space=pltpu.SMEM),
                  pl.BlockSpec((1, 1, LANES), lambda i: (jnp.maximum(i - 1, 0), 0, 0), memory_space=pltpu.SMEM),
                  pl.BlockSpec((1, 1, LANES), lambda i: (0, 0, 0), memory_space=pltpu.SMEM),
                  pl.BlockSpec((TOK_TILE, d), lambda i: (i, 0)),
                  pl.BlockSpec((TOK_TILE, LANES), lambda i: (i, 0))],
        out_specs=pl.BlockSpec(memory_space=pl.ANY),
        out_shape=jax.ShapeDtypeStruct((n_rows, d), F32),
        scratch_shapes=[pltpu.VMEM((2, SORT_ROWS, d), F32),
                        pltpu.VMEM((MOE_BLOCK, d), F32),
                        pltpu.SemaphoreType.DMA((2,)),
                        pltpu.SemaphoreType.DMA(()),
                        pltpu.SemaphoreType.DMA(())],
        compiler_params=_cparams(("arbitrary",)),
        name="dispatch",
    )(table, table, tails, h2, slots)


def _expert_kernel(be_ref, nu_ref, x_ref, w1_ref, w3_ref, w2_ref, y_ref, w1b, w3b, w2b):
    i = pl.program_id(0)
    n_used = nu_ref[0]

    @pl.when(i < n_used)
    def _():
        first = jnp.logical_or(i == 0, be_ref[i] != be_ref[jnp.maximum(i - 1, 0)])

        @pl.when(first)
        def _():
            w1b[...] = w1_ref[0].astype(BF16)
            w3b[...] = w3_ref[0].astype(BF16)
            w2b[...] = w2_ref[0].astype(BF16)

        x = x_ref[...].astype(BF16)
        hid = _silu(_dot(x, w1b[...])) * _dot(x, w3b[...])
        y_ref[...] = _dot(hid.astype(BF16), w2b[...])

    @pl.when(i >= n_used)
    def _():
        y_ref[...] = jnp.zeros_like(y_ref)


def experts(xs, block_expert, n_used, w1, w3, w2, layer):
    n_rows, d = xs.shape
    n_blocks = n_rows // MOE_BLOCK
    ff = w1.shape[-1]
    grid_spec = pltpu.PrefetchScalarGridSpec(
        num_scalar_prefetch=2,
        grid=(n_blocks,),
        in_specs=[pl.BlockSpec((MOE_BLOCK, d), lambda i, be, nu: (jnp.minimum(i, nu[0] - 1), 0)),
                  pl.BlockSpec((None, 1, d, ff), lambda i, be, nu: (layer, be[i], 0, 0)),
                  pl.BlockSpec((None, 1, d, ff), lambda i, be, nu: (layer, be[i], 0, 0)),
                  pl.BlockSpec((None, 1, ff, d), lambda i, be, nu: (layer, be[i], 0, 0))],
        out_specs=pl.BlockSpec((MOE_BLOCK, d), lambda i, be, nu: (i, 0)),
        scratch_shapes=[pltpu.VMEM((d, ff), BF16),
                        pltpu.VMEM((d, ff), BF16),
                        pltpu.VMEM((ff, d), BF16)],
    )
    return pl.pallas_call(
        _expert_kernel,
        grid_spec=grid_spec,
        out_shape=jax.ShapeDtypeStruct((n_rows, d), F32),
        compiler_params=_cparams(("arbitrary",)),
        name="experts",
    )(block_expert, n_used, xs, w1, w3, w2)


def _combine_kernel(d_ref, dn_ref, x_ref, mod_ref, g_ref, slot_ref, y_hbm, o_ref, ybuf, sem, *, final):
    y = _expert_mix(d_ref, dn_ref, slot_ref, y_hbm, ybuf, sem)
    x = x_ref[...] + mod_ref[0, 5:6, :] * y
    if final:
        x = x * lax.rsqrt(jnp.mean(x * x, axis=-1, keepdims=True) + EPS) * g_ref[...]
    o_ref[...] = x


def _expert_mix(d_ref, dn_ref, slot_ref, y_hbm, ybuf, sem):
    i = pl.program_id(0)
    nb = pl.num_programs(0)
    slot = i % 2

    def chunk(s):
        def copy(local, glob, granules):
            return _granule_copy(y_hbm, glob * GRANULE, ybuf.at[s], local * GRANULE, sem.at[s], granules)
        return copy

    @pl.when(i == 0)
    def _():
        ybuf[...] = jnp.zeros_like(ybuf)
        _for_each_run_chunk(d_ref, lambda *a: chunk(0)(*a).start())

    @pl.when(i + 1 < nb)
    def _():
        _for_each_run_chunk(dn_ref, lambda *a: chunk(1 - slot)(*a).start())

    _for_each_run_chunk(d_ref, lambda *a: chunk(slot)(*a).wait())

    hi, lo = _split(_slot_matrix(slot_ref, ybuf.shape[1], weighted=True))
    yb = ybuf[slot].astype(BF16)
    return _dot(hi, yb) + _dot(lo, yb)


def combine(x, modv, ys, table, slots, final_g, n_batch, n_ctx_tiles, final):
    n, d = x.shape
    nt = n // TOK_TILE
    tpb = nt // n_batch
    if final:
        lat = tpb - n_ctx_tiles
        steps = n_batch * lat
        tile = lambda s: (s // lat) * tpb + n_ctx_tiles + s % lat
        out_rows = steps * TOK_TILE
    else:
        steps = nt
        tile = lambda s: s
        out_rows = n

    def mod_idx(s):
        t = tile(s)
        return (t // tpb) * 2 + ((t % tpb) >= n_ctx_tiles).astype(jnp.int32)

    return pl.pallas_call(
        functools.partial(_combine_kernel, final=final),
        grid=(steps,),
        in_specs=[pl.BlockSpec((1, 1, LANES), lambda s: (tile(s), 0, 0), memory_space=pltpu.SMEM),
                  pl.BlockSpec((1, 1, LANES), lambda s: (tile(jnp.minimum(s + 1, steps - 1)), 0, 0),
                               memory_space=pltpu.SMEM),
                  pl.BlockSpec((TOK_TILE, d), lambda s: (tile(s), 0)),
                  pl.BlockSpec((1, N_MOD, d), lambda s: (mod_idx(s), 0, 0)),
                  pl.BlockSpec((1, d), lambda s: (0, 0)),
                  pl.BlockSpec((TOK_TILE, LANES), lambda s: (tile(s), 0)),
                  pl.BlockSpec(memory_space=pl.ANY)],
        out_specs=pl.BlockSpec((TOK_TILE, d), lambda s: (s, 0)),
        out_shape=jax.ShapeDtypeStruct((out_rows, d), F32),
        scratch_shapes=[pltpu.VMEM((2, SORT_ROWS, d), F32),
                        pltpu.SemaphoreType.DMA((2,))],
        compiler_params=_cparams(("arbitrary",)),
        name="combine_final" if final else "combine",
    )(table, table, x, modv, final_g, slots, ys)


def sorted_rows_bound(n_tok):
    nt = n_tok // TOK_TILE
    rows = n_tok * TOP_K + nt * N_EXPERTS * (GRANULE - 1) + N_EXPERTS * (MOE_BLOCK - GRANULE)
    return -(-rows // MOE_BLOCK) * MOE_BLOCK


def moe_tables(gran, n_rows):
    nt = gran.shape[0]
    per_blk = MOE_BLOCK // GRANULE
    local = jnp.cumsum(gran, axis=1) - gran
    before = jnp.cumsum(gran, axis=0) - gran
    total = jnp.sum(gran, axis=0)
    padded = (total + per_blk - 1) // per_blk * per_blk
    region_end = jnp.cumsum(padded)
    region = region_end - padded
    table = jnp.concatenate([local, region[None, :] + before, gran,
                             jnp.zeros((nt, LANES - 3 * N_EXPERTS), gran.dtype)], axis=1)
    table = table.astype(jnp.int32).reshape(nt, 1, LANES)
    tails = jnp.zeros((LANES,), jnp.int32).at[:N_EXPERTS].set(region + total)
    tails = tails.at[N_EXPERTS:2 * N_EXPERTS].set(padded - total)
    n_blocks = n_rows // MOE_BLOCK
    blk = jnp.arange(n_blocks, dtype=jnp.int32) * per_blk
    block_expert = jnp.minimum(jnp.sum(blk[:, None] >= region_end[None, :], axis=1), N_EXPERTS - 1)
    n_used = (region_end[-1] // per_blk).astype(jnp.int32).reshape(1)
    tails = tails.at[2 * N_EXPERTS].set(n_used[0]).reshape(1, 1, LANES)
    return table, tails, block_expert.astype(jnp.int32), n_used


def _rope_tables(ctx_len, n_lat):
    rows = n_lat // GRID_W
    row = np.repeat(np.arange(rows, dtype=np.float32), GRID_W)
    col = np.tile(np.arange(GRID_W, dtype=np.float32), rows)
    half = ATT_HD // 4
    inv = jnp.asarray(ROPE_BASE, F32) ** (-jnp.arange(half, dtype=F32) / half)
    ang_r = jnp.asarray(row)[:, None] * inv
    ang_c = jnp.asarray(col)[:, None] * inv
    cos64 = jnp.concatenate([jnp.cos(ang_r)] * 2 + [jnp.cos(ang_c)] * 2, axis=1)
    sin64 = jnp.concatenate([-jnp.sin(ang_r), jnp.sin(ang_r), -jnp.sin(ang_c), jnp.sin(ang_c)], axis=1)
    cos64 = jnp.concatenate([jnp.ones((ctx_len, ATT_HD), F32), cos64], axis=0)
    sin64 = jnp.concatenate([jnp.zeros((ctx_len, ATT_HD), F32), sin64], axis=0)
    return jnp.tile(cos64, (1, ATT_HEADS)), jnp.tile(sin64, (1, ATT_HEADS))


def _pack_w_in(w):
    assert w.shape[-1] == W_IN_COLS
    return jnp.pad(w.astype(BF16), ((0, 0), (0, 0), (0, W_IN_PADDED - W_IN_COLS)))


def kernel(x, c, ctx, c_ctx, hg_lb_logits, router_w, router_b, final_g, w_mod, b_mod, g_mix, g_ffn,
           w_in, conv_w, conv_b, conv_ln_g, conv_ln_b, hg_norm_g, gla_w2, gla_b2, gla_norm_g, att_sink,
           w_br_conv, w_br_hg, w_br_gla, w_br_att, w_out, moe_w1, moe_w3, moe_w2):
    n_batch, n_lat, d = x.shape
    ctx_len = ctx.shape[1]
    depth = w_in.shape[0]
    assert ctx_len % TOK_TILE == 0 and n_lat % TOK_TILE == 0 and n_lat % GRID_W == 0
    seq = ctx_len + n_lat
    n_ctx_tiles = ctx_len // TOK_TILE

    xs = jnp.concatenate([ctx, x], axis=1).reshape(n_batch * seq, d)

    c_rows = jnp.zeros((8, d), F32).at[:n_batch].set(c).at[n_batch].set(c_ctx)
    mods = modulation(c_rows, w_mod, b_mod).reshape(depth, 8, N_MOD, d)
    modv = jnp.stack([mods[:, n_batch] if j % 2 == 0 else mods[:, j // 2] for j in range(2 * n_batch)], axis=1)

    lb_sm = jax.nn.softmax(hg_lb_logits.astype(F32), axis=0)
    lower = jnp.cumsum(lb_sm, axis=0) - lb_sm[0]
    cos_t, sin_t = _rope_tables(ctx_len, n_lat)
    ones_bd = jnp.asarray(np.arange(256)[:, None] // 64 == np.arange(256)[None, :] // 64, BF16)
    rw = jnp.zeros((d, LANES), F32).at[:, :N_EXPERTS].set(router_w.astype(F32))
    rb = jnp.zeros((1, LANES), F32).at[0, :N_EXPERTS].set(router_b.astype(F32))
    tri = jnp.asarray(np.tril(np.ones((TOK_TILE, TOK_TILE))), BF16)
    upper = jnp.asarray(np.triu(np.ones((LANES, LANES)), k=1), BF16)

    w_in_p = _pack_w_in(w_in)
    wb_conv, wb_hg, wb_gla, wb_att, wb_out = [w.astype(BF16) for w in (w_br_conv, w_br_hg, w_br_gla, w_br_att, w_out)]

    pending = None
    for l in range(depth):
        w2p = jnp.zeros((LANES, 2 * GLA_K), F32)
        w2p = w2p.at[0:GLA_RANK, 0:GLA_K].set(gla_w2[l, 0]).at[GLA_RANK:2 * GLA_RANK, GLA_K:].set(gla_w2[l, 1])
        b2p = gla_b2[l].reshape(1, 2 * GLA_K)
        proj = inproj(xs, modv[l], g_mix[l].reshape(1, d), w_in_p, l, lower[l], w2p, b2p, cos_t, sin_t,
                      n_batch, n_ctx_tiles, moe=pending)
        if pending is not None:
            xs, proj = proj[0], proj[1:]
        conv_z, hg_pack, gla_pack, att_pack, bgate = proj

        hg_f, hg_b = gated_scan(hg_pack, (0, 1, 3, 4), (0, 2, 3, 5), HG_W, n_batch, n_ctx_tiles, "scan_hgrn")
        gla_f, gla_b, att_o = gated_scan(gla_pack, (0, 1, 2, 2), (0, 1, 2, 3), GLA_K, n_batch, n_ctx_tiles,
                                         "scan_gla_attn", attn=(att_pack, att_sink[l].astype(F32), ctx_len))

        lw = dict(conv_w=jnp.zeros((32, CONV_CH), F32).at[:CONV_K].set(conv_w[l]),
                  conv_b=conv_b[l].reshape(1, -1), conv_ln_g=conv_ln_g[l].reshape(1, -1),
                  conv_ln_b=conv_ln_b[l].reshape(1, -1), hg_norm_g=hg_norm_g[l].reshape(1, -1),
                  gla_norm_g=gla_norm_g[l].reshape(1, -1), ones_bd=ones_bd,
                  w_br_conv=wb_conv, w_br_hg=wb_hg, w_br_gla=wb_gla, w_br_att=wb_att, w_out=wb_out,
                  g_ffn=g_ffn[l].reshape(1, d), router_w=rw, router_b=rb, tri=tri, upper=upper, layer=l)
        x_new, h2, slots, gran = merge(xs, modv[l], conv_z, hg_pack, hg_f, hg_b, gla_pack, gla_f, gla_b,
                                       att_o, bgate, lw, n_batch, n_ctx_tiles)

        n_rows = sorted_rows_bound(n_batch * seq)
        table, tails, block_expert, n_used = moe_tables(gran[:, 0, :N_EXPERTS], n_rows)
        x_sorted = dispatch(h2, slots, table, tails, n_rows)
        ys = experts(x_sorted, block_expert, n_used, moe_w1, moe_w3, moe_w2, l)
        if l < depth - 1:
            xs, pending = x_new, (modv[l], ys, table, slots)
        else:
            xs = combine(x_new, modv[l], ys, table, slots, final_g.reshape(1, d), n_batch, n_ctx_tiles, True)

    return xs.reshape(n_batch, n_lat, d)
```

```python
import functools
import itertools

import numpy as np
import jax
import jax.numpy as jnp
from jax import lax
from jax.experimental import pallas as pl
from jax.experimental.pallas import tpu as pltpu

F32 = jnp.float32
BF16 = jnp.bfloat16

EPS = 1e-6
NEG = -1e30
TINY = 1e-30
N_MOD = 6
CONV_CH = 256
CONV_K = 31
HG_HEADS = 4
HG_W = 256
GLA_HEADS = 4
GLA_K = 128
GLA_V = 256
GLA_RANK = 16
GLA_TAU = 16.0
ATT_HEADS = 4
ATT_KV_HEADS = 2
ATT_HD = 64
ATT_BLOCK = 128
GRID_W = 64
ROPE_BASE = 10000.0
N_EXPERTS = 16
N_GROUPS = 4
TOP_K = 2
MOE_BLOCK = 256

LANES = 128
TOK_TILE = 256
SCAN_BLOCK = 16
HALO = 16
GRANULE = 8
SORT_ROWS = -(-(TOP_K * TOK_TILE + N_EXPERTS * (GRANULE - 1)) // LANES) * LANES
V7X_VMEM_BYTES = 64 * 1024 * 1024
VMEM_LIMIT = V7X_VMEM_BYTES * 7 // 8

W_CONV = (0, 512)
W_HG = (512, 1792)
W_GLA_QKV = (1792, 2304)
W_TAIL = 2304
W_IN_COLS = 7200
W_IN_PADDED = 7296
W_GLA_RANK = (W_TAIL, W_TAIL + LANES)
W_SKEW = 2 * GLA_RANK
T_GLA_GATE = (0, 256)
T_ATT = (256, 768)
T_BG = (768, 4864)


def _cparams(sem):
    return pltpu.CompilerParams(dimension_semantics=sem, vmem_limit_bytes=VMEM_LIMIT)


def _dot(a, b):
    return jnp.dot(a, b, preferred_element_type=F32)


def _split(a):
    hi = a.astype(BF16)
    lo = (a - hi.astype(F32)).astype(BF16)
    return hi, lo


def _dot_f32(a, b):
    ah, al = _split(a)
    bh, bl = _split(b)
    return _dot(ah, bh) + _dot(ah, bl) + _dot(al, bh)


def _sigmoid(x):
    return 1.0 / (1.0 + jnp.exp(-x))


def _silu(x):
    return x * _sigmoid(x)


def _mod_kernel(c_ref, w_ref, b_ref, o_ref):
    c = c_ref[...]
    o_ref[0] = _dot_f32(_silu(c), w_ref[0]) + b_ref[0]


def modulation(c_rows, w_mod, b_mod):
    depth, d, six_d = w_mod.shape
    nblk = six_d // d
    return pl.pallas_call(
        _mod_kernel,
        grid=(depth, nblk),
        in_specs=[pl.BlockSpec((8, d), lambda l, j: (0, 0)),
                  pl.BlockSpec((1, d, d), lambda l, j: (l, 0, j)),
                  pl.BlockSpec((1, 1, d), lambda l, j: (l, 0, j))],
        out_specs=pl.BlockSpec((1, 8, d), lambda l, j: (l, 0, j)),
        out_shape=jax.ShapeDtypeStruct((depth, 8, six_d), F32),
        compiler_params=_cparams(("parallel", "parallel")),
        name="modulation",
    )(c_rows, w_mod, b_mod.reshape(depth, 1, six_d))


def _modnorm(x, g, shift, scale):
    y = x * lax.rsqrt(jnp.mean(x * x, axis=-1, keepdims=True) + EPS)
    return (y * g) * (1.0 + scale) + shift


def _inproj_kernel(*refs, after_moe):
    if after_moe:
        (d_ref, dn_ref, x_ref, modp_ref, slot_ref, y_hbm, mod_ref, g_ref, w_ref, lb_ref, w2_ref, b2_ref,
         cos_ref, sin_ref, xo_ref, conv_ref, hg_ref, gla_ref, att_ref, bg_ref, tail_ref, ybuf, sem) = refs
    else:
        (x_ref, mod_ref, g_ref, w_ref, lb_ref, w2_ref, b2_ref, cos_ref, sin_ref,
         conv_ref, hg_ref, gla_ref, att_ref, bg_ref, tail_ref) = refs

    @pl.when(pl.program_id(0) == 0)
    def _():
        width = tail_ref.shape[1]
        tail_ref[...] = pltpu.roll(w_ref[:, W_TAIL:W_TAIL + width], width - W_SKEW, axis=1)

    x = x_ref[...]
    if after_moe:
        x = x + modp_ref[0, 5:6, :] * _expert_mix(d_ref, dn_ref, slot_ref, y_hbm, ybuf, sem)
        xo_ref[...] = x
    h = _modnorm(x, g_ref[...], mod_ref[0, 0:1, :], mod_ref[0, 1:2, :]).astype(BF16)

    p = _dot(h, w_ref[:, W_CONV[0]:W_CONV[1]])
    conv_ref[...] = p[:, :CONV_CH] * _sigmoid(p[:, CONV_CH:])

    p = _dot(h, w_ref[:, W_HG[0]:W_HG[1]])
    hg_ref[:, 0:256] = p[:, 0:256]
    hg_ref[:, 768:1024] = p[:, 768:1024]
    hg_ref[:, 1536:1792] = _silu(p[:, 1024:1280])
    for d in range(2):
        z = p[:, 256 * (d + 1):256 * (d + 2)]
        lb = lb_ref[d:d + 1, :]
        hg_ref[:, 256 * (d + 1):256 * (d + 2)] = (1.0 - lb) * _sigmoid(-z)
        hg_ref[:, 256 * (d + 4):256 * (d + 5)] = jnp.maximum(lb + (1.0 - lb) * _sigmoid(z), TINY)

    p = _dot(h, w_ref[:, W_GLA_QKV[0]:W_GLA_QKV[1]])
    gla_ref[:, 0:128] = p[:, 0:128] * (float(GLA_K // GLA_HEADS) ** -0.5)
    gla_ref[:, 128:256] = p[:, 128:256]
    gla_ref[:, 512:768] = p[:, 256:512]
    gla_ref[:, 768:1024] = _silu(_dot(h, tail_ref[:, T_GLA_GATE[0]:T_GLA_GATE[1]]))
    rank = _dot(h, w_ref[:, W_GLA_RANK[0]:W_GLA_RANK[1]])
    u = _dot_f32(rank, w2_ref[...]) + b2_ref[...]
    log_sig = jnp.minimum(u, 0.0) - jnp.log(1.0 + jnp.exp(-jnp.abs(u)))
    gla_ref[:, 256:512] = jnp.exp(log_sig * (1.0 / GLA_TAU))

    p = _dot(h, tail_ref[:, T_ATT[0]:T_ATT[1]])
    cos = cos_ref[...]
    sin = sin_ref[...]
    def rope(v, width):
        lane = lax.broadcasted_iota(jnp.int32, v.shape, 1)
        partner = jnp.where((lane & 31) < 16,
                            pltpu.roll(v, width - 16, axis=1), pltpu.roll(v, 16, axis=1))
        return v * cos[:, :width] + partner * sin[:, :width]

    att_ref[:, 0:256] = rope(p[:, 0:256], 256) * (float(ATT_HD) ** -0.5)
    att_ref[:, 256:384] = rope(p[:, 256:384], 128)
    att_ref[:, 384:512] = p[:, 384:512]

    p = _dot(h, tail_ref[:, T_BG[0]:T_BG[1]])
    bg_ref[...] = _sigmoid(p).astype(BF16)


def inproj(x, modv, g_mix, w_in_p, layer, lb, w2p, b2p, cos_t, sin_t, n_batch, n_ctx_tiles, moe=None):
    n, d = x.shape
    nt = n // TOK_TILE
    tiles_per_batch = nt // n_batch

    def mod_idx(i):
        b = i // tiles_per_batch
        return b * 2 + ((i % tiles_per_batch) >= n_ctx_tiles).astype(jnp.int32)

    const = lambda i: (0, 0)
    row = lambda i: (i, 0)
    seq = lambda i: (i % tiles_per_batch, 0)
    pre_specs, pre_ops, pre_out_specs, pre_out_shapes, pre_scratch = [], [], [], [], []
    if moe is not None:
        modv_prev, ys, table, slots = moe
        pre_specs = [pl.BlockSpec((1, 1, LANES), lambda i: (i, 0, 0), memory_space=pltpu.SMEM),
                     pl.BlockSpec((1, 1, LANES), lambda i: (jnp.minimum(i + 1, nt - 1), 0, 0),
                                  memory_space=pltpu.SMEM)]
        pre_ops = [table, table]
        pre_out_specs = [pl.BlockSpec((TOK_TILE, d), row)]
        pre_out_shapes = [jax.ShapeDtypeStruct((n, d), F32)]
        pre_scratch = [pltpu.VMEM((2, SORT_ROWS, d), F32), pltpu.SemaphoreType.DMA((2,))]
    mid_specs, mid_ops = [], []
    if moe is not None:
        mid_specs = [pl.BlockSpec((1, N_MOD, d), lambda i: (mod_idx(i), 0, 0)),
                     pl.BlockSpec((TOK_TILE, LANES), row),
                     pl.BlockSpec(memory_space=pl.ANY)]
        mid_ops = [modv_prev, slots, ys]
    outs = pl.pallas_call(
        functools.partial(_inproj_kernel, after_moe=moe is not None),
        grid=(nt,),
        in_specs=pre_specs + [pl.BlockSpec((TOK_TILE, d), row)] + mid_specs + [
                  pl.BlockSpec((1, N_MOD, d), lambda i: (mod_idx(i), 0, 0)),
                  pl.BlockSpec((1, d), const),
                  pl.BlockSpec((None, d, W_IN_PADDED), lambda i: (layer, 0, 0), pipeline_mode=pl.Buffered(1)),
                  pl.BlockSpec((2, HG_W), const),
                  pl.BlockSpec((LANES, 2 * GLA_K), const),
                  pl.BlockSpec((1, 2 * GLA_K), const),
                  pl.BlockSpec((TOK_TILE, 256), seq),
                  pl.BlockSpec((TOK_TILE, 256), seq)],
        out_specs=pre_out_specs + [pl.BlockSpec((TOK_TILE, 256), row),
                                   pl.BlockSpec((TOK_TILE, 1792), row),
                                   pl.BlockSpec((TOK_TILE, 1024), row),
                                   pl.BlockSpec((TOK_TILE, 512), row),
                                   pl.BlockSpec((TOK_TILE, 4096), row)],
        out_shape=pre_out_shapes + [jax.ShapeDtypeStruct((n, 256), F32),
                                    jax.ShapeDtypeStruct((n, 1792), F32),
                                    jax.ShapeDtypeStruct((n, 1024), F32),
                                    jax.ShapeDtypeStruct((n, 512), F32),
                                    jax.ShapeDtypeStruct((n, 4096), BF16)],
        scratch_shapes=[pltpu.VMEM((d, W_IN_PADDED - W_TAIL), BF16)] + pre_scratch,
        compiler_params=_cparams(("arbitrary",)),
        name="combine_inproj" if moe is not None else "inproj",
    )(*pre_ops, x, *mid_ops, modv, g_mix, w_in_p, lb, w2p, b2p, cos_t, sin_t)
    return outs


SCAN_IN = 4
SCAN_SCRATCH = 12


ATTN_IN = 11
ATTN_STRIDE = 3


def _trace_alternately(chains):
    live = list(chains)
    rnd = 0
    while live:
        for entry in list(live):
            gen, stride = entry
            if rnd % stride == 0:
                try:
                    next(gen)
                except StopIteration:
                    live.remove(entry)
        rnd += 1


def _scan_kernel(*refs, attn):
    if attn is not None:
        sink_ref, refs = refs[0], refs[1:]
    n_chain = 2
    n_in = n_chain * SCAN_IN
    ins = refs[:n_in]
    sel = refs[n_in]
    perms = refs[n_in + 1:n_in + 5]
    pos = n_in + 5
    if attn is not None:
        att_in = refs[pos:pos + ATTN_IN]
        pos += ATTN_IN
    outs = refs[pos:pos + n_chain]
    pos += n_chain
    if attn is not None:
        att_out = refs[pos]
        pos += 1
    scratch = refs[pos:]
    chains = []
    for i in range(n_chain):
        rev = i % 2 == 1
        chains.append((_scan_direction(*ins[i * SCAN_IN:(i + 1) * SCAN_IN], sel,
                                       perms[2 * rev], perms[2 * rev + 1], outs[i],
                                       *scratch[i * SCAN_SCRATCH:(i + 1) * SCAN_SCRATCH], rev=rev), 1))
    if attn is not None:
        ctx_blocks, blocks_per_batch = attn
        q_ref, kc_ref, vc_ref = att_in[:3]
        kb, vb = att_in[3:7], att_in[7:11]
        per_tile = q_ref.shape[0] // ATT_BLOCK
        for a in range(per_tile):
            rows = slice(a * ATT_BLOCK, (a + 1) * ATT_BLOCK)

            def write(hk, x, rows=rows):
                att_out[rows, hk * LANES:(hk + 1) * LANES] = x

            gens = _attn_chains(pl.program_id(1) * per_tile + a, sink_ref, q_ref[rows, :],
                                [kc_ref[...]] + [kb[a + t][...] for t in range(3)],
                                [vc_ref[...]] + [vb[a + t][...] for t in range(3)],
                                write, ctx_blocks, blocks_per_batch)
            chains += [(g, ATTN_STRIDE) for g in gens]
    _trace_alternately(chains)


def _scan_direction(q_ref, k_ref, v_ref, f_ref, sel_ref, perm_ref, permt_ref, o_ref,
                    s_ref, qs_ref, ks_ref, vs_ref, fs_ref, qt_ref, kh_ref, term_ref, w_ref, ah_ref, al_ref,
                    kv_ref, *, rev):
    c = SCAN_BLOCK
    tt, kl = q_ref.shape
    nb = tt // c
    assert nb == c

    @pl.when(pl.program_id(1) == 0)
    def _():
        s_ref[...] = jnp.zeros_like(s_ref)

    perm = perm_ref[...]
    vb = v_ref[...].astype(BF16)
    f_hi, f_lo = _split(f_ref[...])
    qs_ref[...] = _dot(perm, q_ref[...].astype(BF16))
    ks_ref[...] = _dot(perm, k_ref[...].astype(BF16))
    vs_ref[...] = _dot(perm, vb)
    fs_ref[...] = _dot(perm, f_hi) + _dot(perm, f_lo)
    yield

    def slab(ref, s):
        return ref[s * nb:(s + 1) * nb, :]

    a = slab(fs_ref, 0)
    qt_ref[0:nb, :] = (slab(qs_ref, 0) * a).astype(BF16)
    for s in range(1, c):
        a = a * slab(fs_ref, s)
        qt_ref[s * nb:(s + 1) * nb, :] = (slab(qs_ref, s) * a).astype(BF16)
    gam_t = a.T
    g = jnp.ones((nb, kl), F32)
    for s in range(c - 1, -1, -1):
        kh_ref[s * nb:(s + 1) * nb, :] = (slab(ks_ref, s) * g).astype(BF16)
        g = g * slab(fs_ref, s)
    yield

    heads = 256 // 64
    dk = kl // heads
    perm_t = permt_ref[...]
    qt = _dot(perm_t, qt_ref[...]).astype(BF16)
    kh = _dot(perm_t, kh_ref[...]).astype(BF16)

    def head_lanes(x, h):
        return x[:, :dk] if h == 0 else pltpu.roll(x, kl - dk * h, axis=1)[:, :dk]

    q_heads = [head_lanes(qt, h) for h in range(heads)]
    k_heads = [head_lanes(kh, h) for h in range(heads)]
    v_head = lax.broadcasted_iota(jnp.int32, (c, 256), 1) // 64
    g_head = lax.broadcasted_iota(jnp.int32, (dk, 256), 1) // 64
    for j in range(nb):
        lo = j * c
        k4 = jnp.concatenate([k_heads[h][lo:lo + c, :] for h in range(heads)], axis=0)
        v4 = jnp.concatenate([jnp.where(v_head == h, vb[lo:lo + c, :], jnp.zeros_like(vb[lo:lo + c, :]))
                              for h in range(heads)], axis=0)
        kv_ref[j] = lax.dot_general(k4, v4, (((0,), (0,)), ((), ())), preferred_element_type=F32)
        yield

    n_pair = 0
    for s in range(c):
        p = slab(qs_ref, s)
        for d in range(s + 1):
            if d > 0:
                p = p * slab(fs_ref, s - d + 1)
            term_ref[n_pair * nb:(n_pair + 1) * nb, :] = (p * slab(ks_ref, s - d)).astype(BF16)
            n_pair += 1
        yield
    w_ref[...] = _dot(term_ref[...], sel_ref[...])
    yield
    n_pair = 0
    for s in range(c):
        acc = jnp.zeros((nb, 256), F32)
        for d in range(s + 1):
            acc = acc + w_ref[n_pair * nb:(n_pair + 1) * nb, :] * slab(vs_ref, s - d)
            n_pair += 1
        hi, lo = _split(acc)
        ah_ref[s * nb:(s + 1) * nb, :] = hi
        al_ref[s * nb:(s + 1) * nb, :] = lo
        yield

    o_ref[...] = _dot(perm_t, ah_ref[...]) + _dot(perm_t, al_ref[...])
    yield
    st = s_ref[...]
    for jj in range(nb):
        j = nb - 1 - jj if rev else jj
        lo = j * c
        q4 = jnp.concatenate([q_heads[h][lo:lo + c, :] for h in range(heads)], axis=0)
        o4 = _dot(q4, st.astype(BF16))
        inter = o4[0:c, :]
        for h in range(1, heads):
            inter = jnp.where(v_head == h, o4[h * c:(h + 1) * c, :], inter)
        o_ref[lo:lo + c, :] += inter
        gam = jnp.broadcast_to(gam_t[0:dk, j:j + 1], (dk, 256))
        for h in range(1, heads):
            gam = jnp.where(g_head == h, gam_t[h * dk:(h + 1) * dk, j:j + 1], gam)
        st = gam * st + kv_ref[j]
        yield
    s_ref[...] = st


def gated_scan(pack, cols_fwd, cols_rev, kl, n_batch, n_ctx_tiles, name, attn=None):
    n = pack.shape[0]
    nt = n // TOK_TILE
    tpb = nt // n_batch

    def rev_tile(i):
        return jnp.where(i < n_ctx_tiles, n_ctx_tiles - 1 - i, tpb - 1 - (i - n_ctx_tiles))

    heads = 4
    c = SCAN_BLOCK
    nb = TOK_TILE // c
    perms = []
    for rev in (False, True):
        perm = np.zeros((TOK_TILE, TOK_TILE), np.float32)
        for j in range(nb):
            for s in range(c):
                perm[s * nb + j, j * c + (c - 1 - s if rev else s)] = 1.0
        perms += [jnp.asarray(perm, BF16), jnp.asarray(perm.T, BF16)]
    n_pairs = c * (c + 1) // 2
    fwd = lambda col: (lambda b, i, *_: (b * tpb + i, col))
    bwd = lambda col: (lambda b, i, *_: (b * tpb + rev_tile(i), col))
    const = lambda b, i, *_: (0, 0)

    def direction_specs(rows, cols, kl):
        qc, kc, vc, fc = cols
        specs = [pl.BlockSpec((TOK_TILE, kl), rows(qc)),
                 pl.BlockSpec((TOK_TILE, kl), rows(kc)),
                 pl.BlockSpec((TOK_TILE, 256), rows(vc)),
                 pl.BlockSpec((TOK_TILE, kl), rows(fc))]
        assert len(specs) == SCAN_IN
        return specs

    def chain_scratch(kl):
        dk = kl // heads
        shapes = [pltpu.VMEM((dk, 256), F32),
                  pltpu.VMEM((TOK_TILE, kl), F32),
                  pltpu.VMEM((TOK_TILE, kl), F32),
                  pltpu.VMEM((TOK_TILE, 256), F32),
                  pltpu.VMEM((TOK_TILE, kl), F32),
                  pltpu.VMEM((TOK_TILE, kl), BF16),
                  pltpu.VMEM((TOK_TILE, kl), BF16),
                  pltpu.VMEM((n_pairs * nb, kl), BF16),
                  pltpu.VMEM((n_pairs * nb, 256), F32),
                  pltpu.VMEM((TOK_TILE, 256), BF16),
                  pltpu.VMEM((TOK_TILE, 256), BF16),
                  pltpu.VMEM((nb, dk, 256), F32)]
        assert len(shapes) == SCAN_SCRATCH
        return shapes

    dk = kl // heads
    sel = jnp.asarray(np.arange(kl)[:, None] // dk == np.arange(256)[None, :] // 64, BF16)
    in_specs = direction_specs(fwd, cols_fwd, kl) + direction_specs(bwd, cols_rev, kl)
    in_specs += [pl.BlockSpec(sel.shape, const)] + [pl.BlockSpec((TOK_TILE, TOK_TILE), const)] * 4
    operands = [pack] * (2 * SCAN_IN) + [sel] + perms
    out_specs = [pl.BlockSpec((TOK_TILE, 256), fwd(0)), pl.BlockSpec((TOK_TILE, 256), bwd(0))]
    prefetch, kernel_attn = [], None
    if attn is not None:
        att_pack, sink, ctx_len = attn
        per_tile = TOK_TILE // ATT_BLOCK
        bpb = tpb * per_tile
        cb = ctx_len // ATT_BLOCK

        def band(off, col):
            def idx(b, i, *_):
                return (b * bpb + jnp.clip(i * per_tile + off, cb, bpb - 1), col)
            return pl.BlockSpec((ATT_BLOCK, LANES), idx)

        def ctx(col):
            return pl.BlockSpec((ctx_len, LANES), lambda b, i, *_: (b * (bpb * ATT_BLOCK // ctx_len), col))

        att_specs = ([pl.BlockSpec((TOK_TILE, 256), fwd(0)), ctx(2), ctx(3)]
                     + [band(off, 2) for off in range(-1, per_tile + 1)]
                     + [band(off, 3) for off in range(-1, per_tile + 1)])
        assert len(att_specs) == ATTN_IN
        in_specs += att_specs
        operands += [att_pack] * ATTN_IN
        out_specs.append(pl.BlockSpec((TOK_TILE, 256), fwd(0)))
        prefetch, kernel_attn = [sink], (cb, bpb)
    grid_spec = pltpu.PrefetchScalarGridSpec(
        num_scalar_prefetch=len(prefetch),
        grid=(n_batch, tpb),
        in_specs=in_specs,
        out_specs=out_specs,
        scratch_shapes=chain_scratch(kl) + chain_scratch(kl),
    )
    return pl.pallas_call(
        functools.partial(_scan_kernel, attn=kernel_attn),
        grid_spec=grid_spec,
        out_shape=[jax.ShapeDtypeStruct((n, 256), F32)] * len(out_specs),
        compiler_params=_cparams(("parallel", "arbitrary")),
        name=name,
    )(*prefetch, *operands)


def _attn_chains(n, sink_ref, q, keys, vals, write, ctx_blocks, blocks_per_batch):
    blk = ATT_BLOCK
    lane = lax.broadcasted_iota(jnp.int32, (blk, LANES), 1)
    low = lane < ATT_HD
    qi = lax.broadcasted_iota(jnp.int32, (blk, blk), 0)
    ki = lax.broadcasted_iota(jnp.int32, (blk, blk), 1)
    is_lat = n >= ctx_blocks
    band_ok = [
        jnp.logical_and(jnp.logical_and(is_lat, n - 1 >= ctx_blocks), qi <= ki),
        jnp.logical_and(is_lat, qi >= 0),
        jnp.logical_and(jnp.logical_and(is_lat, n + 1 < blocks_per_batch), ki <= qi),
    ]

    def dup(x, hk):
        xr = pltpu.roll(x, ATT_HD, axis=1)
        lo_x = lax.broadcasted_iota(jnp.int32, x.shape, 1) < ATT_HD
        return (jnp.where(lo_x, x, xr) if hk == 0 else jnp.where(lo_x, xr, x)).astype(BF16)

    nt_dims = (((1,), (1,)), ((), ()))

    def kv_group(hk):
        kd = [dup(x, hk) for x in keys]
        vd = [dup(x, hk) for x in vals]
        qh = q[:, hk * LANES:(hk + 1) * LANES]
        q2 = jnp.concatenate([jnp.where(low, qh, 0.0), jnp.where(low, 0.0, qh)], axis=0).astype(BF16)
        yield
        s = [lax.dot_general(q2, kx, nt_dims, preferred_element_type=F32) for kx in kd]
        for t in range(3):
            s[t + 1] = jnp.where(jnp.concatenate([band_ok[t]] * 2, axis=0), s[t + 1], NEG)
        yield
        sink = jnp.where(lax.broadcasted_iota(jnp.int32, (2 * blk, 1), 0) < blk,
                         sink_ref[hk * 2], sink_ref[hk * 2 + 1])
        pieces = [s[0][:, :LANES], s[0][:, LANES:], s[1], s[2], s[3]]
        top = pieces[0]
        for piece in pieces[1:]:
            top = jnp.maximum(top, piece)
        m = jnp.maximum(sink, top.max(axis=-1, keepdims=True))
        yield
        e = [jnp.exp(x - m) for x in s]
        tot = e[0][:, :LANES] + e[0][:, LANES:] + e[1] + e[2] + e[3]
        den = jnp.exp(sink - m) + tot.sum(axis=-1, keepdims=True)
        yield
        o = _dot(e[0].astype(BF16), vd[0])
        for t in range(1, 4):
            o = o + _dot(e[t].astype(BF16), vd[t])
        yield
        o = o / den
        write(hk, jnp.where(low, o[:blk, :], o[blk:, :]))

    return [kv_group(hk) for hk in range(ATT_KV_HEADS)]


def _head_norm(o, ones_bd, g):
    sq = o * o
    hi, lo = _split(sq)
    ms = (_dot(hi, ones_bd) + _dot(lo, ones_bd)) * (1.0 / 64.0)
    return o * lax.rsqrt(ms + EPS) * g


def _route_select(aff, bias):
    lane = lax.broadcasted_iota(jnp.int32, aff.shape, 1)
    epg = N_EXPERTS // N_GROUPS
    pos = lane & (epg - 1)
    v = aff + bias

    def nxt(x, o):
        return pltpu.roll(x, LANES - o, axis=1)

    def prv(x, o):
        return pltpu.roll(x, o, axis=1)

    beaten = jnp.zeros(aff.shape, jnp.int32)
    for o in range(1, epg):
        beaten = beaten + jnp.where(jnp.logical_and(pos + o < epg, nxt(v, o) > v), 1, 0)
        beaten = beaten + jnp.where(jnp.logical_and(pos >= o, prv(v, o) >= v), 1, 0)
    top2 = beaten < TOP_K
    t = jnp.where(top2, v, 0.0)
    score = t
    for o in range(1, epg):
        score = score + jnp.where(pos + o < epg, nxt(t, o), 0.0) + jnp.where(pos >= o, prv(t, o), 0.0)
    worse = jnp.zeros(aff.shape, jnp.int32)
    for o in range(epg, N_EXPERTS, epg):
        worse = worse + jnp.where(jnp.logical_and(lane + o < N_EXPERTS, nxt(score, o) > score), 1, 0)
        worse = worse + jnp.where(jnp.logical_and(lane >= o, prv(score, o) >= score), 1, 0)
    sel = jnp.logical_and(jnp.logical_and(lane < N_EXPERTS, worse == 0), top2)
    picked = jnp.where(sel, aff, 0.0)
    gate = picked / jnp.sum(picked, axis=-1, keepdims=True)
    return gate, sel


def _merge_kernel(x_ref, mod_ref, zp_ref, z_ref, zn_ref, hgf_ref, hgb_ref, hgg_ref, glf_ref, glb_ref,
                  glg_ref, att_ref, bg_ref, cw_ref, cb_ref, lng_ref, lnb_ref, hgn_ref, gln_ref,
                  ones_ref, wbc_ref, wbh_ref, wbg_ref, wba_ref, wo_ref, gf_ref, rw_ref, rb_ref, tri_ref,
                  upper_ref, xo_ref, h2_ref, slot_ref, gran_ref, zs_ref, zsh_ref, *, tiles_per_batch,
                  n_ctx_tiles):
    i = pl.program_id(0)
    tt = x_ref.shape[0]
    ti = i % tiles_per_batch
    has_prev = jnp.logical_and(ti != 0, ti != n_ctx_tiles)
    has_next = jnp.logical_and(ti != n_ctx_tiles - 1, ti != tiles_per_batch - 1)

    zs_ref[0:HALO, :] = jnp.where(has_prev, zp_ref[...], 0.0)
    zs_ref[HALO:HALO + tt, :] = z_ref[...]
    zs_ref[HALO + tt:HALO + tt + HALO, :] = jnp.where(has_next, zn_ref[...], 0.0)
    span = tt + 2 * HALO - 8
    for b in range(8):
        zsh_ref[b] = zs_ref[b:b + span, :]
    ones_bd = ones_ref[...]
    d = x_ref.shape[1]
    n_part = 2
    part = tt // n_part
    routed = [None] * n_part

    def rows_chain(k):
        r = slice(k * part, (k + 1) * part)
        acc = jnp.zeros((part, CONV_CH), F32) + cb_ref[...]
        for j in range(CONV_K):
            off = HALO - CONV_K // 2 + j + k * part
            acc = acc + zsh_ref[off % 8, off - off % 8:off - off % 8 + part, :] * cw_ref[j:j + 1, :]
            if j % 8 == 7:
                yield
        mu = jnp.mean(acc, axis=-1, keepdims=True)
        cen = acc - mu
        var = jnp.mean(cen * cen, axis=-1, keepdims=True)
        conv_y = _silu(cen * lax.rsqrt(var + EPS) * lng_ref[...] + lnb_ref[...])
        yield
        hg_y = _head_norm(hgf_ref[r, :] + hgb_ref[r, :], ones_bd, hgn_ref[...]) * hgg_ref[r, :]
        yield
        gla_y = _head_norm(glf_ref[r, :] + glb_ref[r, :], ones_bd, gln_ref[...]) * glg_ref[r, :]
        yield
        merged = bg_ref[r, 0:d].astype(F32) * _dot(conv_y.astype(BF16), wbc_ref[...])
        yield
        merged = merged + bg_ref[r, d:2 * d].astype(F32) * _dot(hg_y.astype(BF16), wbh_ref[...])
        yield
        merged = merged + bg_ref[r, 2 * d:3 * d].astype(F32) * _dot(gla_y.astype(BF16), wbg_ref[...])
        yield
        merged = merged + bg_ref[r, 3 * d:4 * d].astype(F32) * _dot(att_ref[r, :].astype(BF16), wba_ref[...])
        yield
        mix = _dot(merged.astype(BF16), wo_ref[...])
        yield
        x_new = x_ref[r, :] + mod_ref[0, 2:3, :] * mix
        xo_ref[r, :] = x_new
        h2 = _modnorm(x_new, gf_ref[...], mod_ref[0, 3:4, :], mod_ref[0, 4:5, :])
        h2_ref[r, :] = h2
        yield
        routed[k] = _route_select(_sigmoid(_dot_f32(h2, rw_ref[...])), rb_ref[...])

    _trace_alternately([(rows_chain(k), 1) for k in range(n_part)])
    gate = jnp.concatenate([g for g, _ in routed], axis=0)
    sel = jnp.concatenate([s for _, s in routed], axis=0)

    incl = _dot(tri_ref[...], sel.astype(F32).astype(BF16))
    gran = jnp.floor((incl[tt - 1:tt, :] + (GRANULE - 1.0)) * (1.0 / GRANULE))
    gran8 = jnp.broadcast_to(gran, (8, LANES))
    start = _dot(gran8.astype(BF16), upper_ref[...])[0:1, :]
    pos = GRANULE * start + incl - 1.0
    p0 = jnp.min(jnp.where(sel, pos, 1e9), axis=-1, keepdims=True)
    p1 = jnp.max(jnp.where(sel, pos, -1.0), axis=-1, keepdims=True)
    g0 = jnp.sum(jnp.where(jnp.logical_and(sel, pos == p0), gate, 0.0), axis=-1, keepdims=True)
    g1 = jnp.sum(jnp.where(jnp.logical_and(sel, pos == p1), gate, 0.0), axis=-1, keepdims=True)
    lane = lax.broadcasted_iota(jnp.int32, (tt, LANES), 1)
    slot_ref[...] = jnp.where(lane == 0, p0, jnp.where(lane == 1, p1, jnp.where(lane == 2, g0,
                              jnp.where(lane == 3, g1, 0.0))))
    gran_ref[0] = gran8.astype(jnp.int32)


def merge(x, modv, conv_z, hg_pack, hg_f, hg_b, gla_pack, gla_f, gla_b, att_o, bgate, lw, n_batch,
          n_ctx_tiles):
    n, d = x.shape
    nt = n // TOK_TILE
    tpb = nt // n_batch
    hpt = TOK_TILE // HALO
    n_halo = n // HALO

    def mod_idx(i):
        return (i // tpb) * 2 + ((i % tpb) >= n_ctx_tiles).astype(jnp.int32)

    row = lambda i: (i, 0)
    const = lambda i: (0, 0)
    col = lambda c: (lambda i: (i, c))
    layer = lw["layer"]

    def full(a):
        if a.ndim == 3:
            return pl.BlockSpec((None,) + a.shape[1:], lambda i: (layer, 0, 0))
        return pl.BlockSpec(a.shape, const)

    weights = [lw["conv_w"], lw["conv_b"], lw["conv_ln_g"], lw["conv_ln_b"], lw["hg_norm_g"],
               lw["gla_norm_g"], lw["ones_bd"], lw["w_br_conv"], lw["w_br_hg"], lw["w_br_gla"],
               lw["w_br_att"], lw["w_out"], lw["g_ffn"], lw["router_w"], lw["router_b"], lw["tri"],
               lw["upper"]]
    return pl.pallas_call(
        functools.partial(_merge_kernel, tiles_per_batch=tpb, n_ctx_tiles=n_ctx_tiles),
        grid=(nt,),
        in_specs=[pl.BlockSpec((TOK_TILE, d), row),
                  pl.BlockSpec((1, N_MOD, d), lambda i: (mod_idx(i), 0, 0)),
                  pl.BlockSpec((HALO, CONV_CH), lambda i: (jnp.maximum(i * hpt - 1, 0), 0)),
                  pl.BlockSpec((TOK_TILE, CONV_CH), row),
                  pl.BlockSpec((HALO, CONV_CH), lambda i: (jnp.minimum((i + 1) * hpt, n_halo - 1), 0)),
                  pl.BlockSpec((TOK_TILE, 256), row),
                  pl.BlockSpec((TOK_TILE, 256), row),
                  pl.BlockSpec((TOK_TILE, 256), col(6)),
                  pl.BlockSpec((TOK_TILE, 256), row),
                  pl.BlockSpec((TOK_TILE, 256), row),
                  pl.BlockSpec((TOK_TILE, 256), col(3)),
                  pl.BlockSpec((TOK_TILE, 256), row),
                  pl.BlockSpec((TOK_TILE, 4 * d), row)] + [full(w) for w in weights],
        out_specs=[pl.BlockSpec((TOK_TILE, d), row),
                   pl.BlockSpec((TOK_TILE, d), row),
                   pl.BlockSpec((TOK_TILE, LANES), row),
                   pl.BlockSpec((1, 8, LANES), lambda i: (i, 0, 0))],
        out_shape=[jax.ShapeDtypeStruct((n, d), F32),
                   jax.ShapeDtypeStruct((n, d), F32),
                   jax.ShapeDtypeStruct((n, LANES), F32),
                   jax.ShapeDtypeStruct((nt, 8, LANES), jnp.int32)],
        scratch_shapes=[pltpu.VMEM((TOK_TILE + 2 * HALO, CONV_CH), F32),
                        pltpu.VMEM((8, TOK_TILE + 2 * HALO - 8, CONV_CH), F32)],
        compiler_params=_cparams(("parallel",)),
        name="merge",
    )(x, modv, conv_z, conv_z, conv_z, hg_f, hg_b, hg_pack, gla_f, gla_b, gla_pack, att_o, bgate,
      *weights)


def _granule_copy(src, src_row, dst, dst_row, sem, granules=1):
    rows = granules * GRANULE
    return pltpu.make_async_copy(src.at[pl.ds(pl.multiple_of(src_row, GRANULE), rows)],
                                 dst.at[pl.ds(pl.multiple_of(dst_row, GRANULE), rows)], sem)


RUN_CHUNKS = (8, 4, 2, 1)
MAX_RUN = TOK_TILE // GRANULE


def _for_each_run_chunk(tab_ref, fn):
    big = RUN_CHUNKS[0]
    for e in range(N_EXPERTS):
        local = tab_ref[0, 0, e]
        glob = tab_ref[0, 0, N_EXPERTS + e]
        n = tab_ref[0, 0, 2 * N_EXPERTS + e]
        queue = e % 2
        for k in range(MAX_RUN // big):
            @pl.when(n >= big * (k + 1))
            def _(k=k):
                fn(local + big * k, glob + big * k, big, queue)
        off = n - n % big
        for size in RUN_CHUNKS[1:]:
            @pl.when((n & size) != 0)
            def _(off=off, size=size):
                fn(local + off, glob + off, size, queue)
            off = off + (n & size)


def _slot_matrix(slot_ref, width, weighted):
    tt = slot_ref.shape[0]
    col = lax.broadcasted_iota(jnp.int32, (tt, width), 1).astype(F32)
    hit0 = col == slot_ref[:, 0:1]
    hit1 = col == slot_ref[:, 1:2]
    if not weighted:
        return jnp.where(jnp.logical_or(hit0, hit1), 1.0, 0.0)
    return jnp.where(hit0, slot_ref[:, 2:3], 0.0) + jnp.where(hit1, slot_ref[:, 3:4], 0.0)


def _dispatch_kernel(tab_ref, tabp_ref, tail_ref, h_ref, slot_ref, xs_hbm, buf, zbuf, sem, zsem, bsem):
    i = pl.program_id(0)
    nb = pl.num_programs(0)
    slot = i % 2

    perm = _slot_matrix(slot_ref, buf.shape[1], weighted=False).astype(BF16)
    buf[slot] = lax.dot_general(perm, h_ref[...].astype(BF16), (((0,), (0,)), ((), ())),
                                preferred_element_type=F32)

    def chunk(s):
        def copy(local, glob, granules):
            return _granule_copy(buf.at[s], local * GRANULE, xs_hbm, glob * GRANULE, sem.at[s], granules)
        return copy

    _for_each_run_chunk(tab_ref, lambda l, g, n, q: chunk(slot)(l, g, n).start(priority=q))

    @pl.when(i > 0)
    def _():
        _for_each_run_chunk(tabp_ref, lambda l, g, n, q: chunk(1 - slot)(l, g, n).wait())

    @pl.when(i == nb - 1)
    def _():
        _for_each_run_chunk(tab_ref, lambda l, g, n, q: chunk(slot)(l, g, n).wait())

    @pl.when(i == 0)
    def _():
        zbuf[...] = jnp.zeros_like(zbuf)
        n_used = tail_ref[0, 0, 2 * N_EXPERTS]
        n_blocks = xs_hbm.shape[0] // MOE_BLOCK

        def block_copy(b):
            return pltpu.make_async_copy(
                zbuf, xs_hbm.at[pl.ds(pl.multiple_of(b * MOE_BLOCK, MOE_BLOCK), MOE_BLOCK)], bsem)

        for e in range(N_EXPERTS):
            def fill(m, carry, e=e):
                _granule_copy(zbuf, 0, xs_hbm, (tail_ref[0, 0, e] + m) * GRANULE, zsem).start()
                return carry
            lax.fori_loop(0, tail_ref[0, 0, N_EXPERTS + e], fill, 0)

        def fill_block(b, carry):
            block_copy(b).start()
            return carry
        lax.fori_loop(n_used, n_blocks, fill_block, 0)

        for e in range(N_EXPERTS):
            def done(m, carry):
                _granule_copy(zbuf, 0, xs_hbm, 0, zsem).wait()
                return carry
            lax.fori_loop(0, tail_ref[0, 0, N_EXPERTS + e], done, 0)

        def done_block(b, carry):
            block_copy(b).wait()
            return carry
        lax.fori_loop(n_used, n_blocks, done_block, 0)


def dispatch(h2, slots, table, tails, n_rows):
    n, d = h2.shape
    nt = n // TOK_TILE
    return pl.pallas_call(
        _dispatch_kernel,
        grid=(nt,),
        in_specs=[pl.BlockSpec((1, 1, LANES), lambda i: (i, 0, 0), memory_space=pltpu.SMEM),
                  pl.BlockSpec((1, 1, LANES), lambda i: (jnp.maximum(i - 1, 0), 0, 0), memory_space=pltpu.SMEM),
                  pl.BlockSpec((1, 1, LANES), lambda i: (0, 0, 0), memory_space=pltpu.SMEM),
                  pl.BlockSpec((TOK_TILE, d), lambda i: (i, 0)),
                  pl.BlockSpec((TOK_TILE, LANES), lambda i: (i, 0))],
        out_specs=pl.BlockSpec(memory_space=pl.ANY),
        out_shape=jax.ShapeDtypeStruct((n_rows, d), F32),
        scratch_shapes=[pltpu.VMEM((2, SORT_ROWS, d), F32),
                        pltpu.VMEM((MOE_BLOCK, d), F32),
                        pltpu.SemaphoreType.DMA((2,)),
                        pltpu.SemaphoreType.DMA(()),
                        pltpu.SemaphoreType.DMA(())],
        compiler_params=_cparams(("arbitrary",)),
        name="dispatch",
    )(table, table, tails, h2, slots)


def _expert_kernel(be_ref, nu_ref, x_ref, w1_ref, w3_ref, w2_ref, y_ref, w1b, w3b, w2b):
    i = pl.program_id(0)
    n_used = nu_ref[0]

    @pl.when(i < n_used)
    def _():
        first = jnp.logical_or(i == 0, be_ref[i] != be_ref[jnp.maximum(i - 1, 0)])

        @pl.when(first)
        def _():
            w1b[...] = w1_ref[0].astype(BF16)
            w3b[...] = w3_ref[0].astype(BF16)
            w2b[...] = w2_ref[0].astype(BF16)

        x = x_ref[...].astype(BF16)
        hid = _silu(_dot(x, w1b[...])) * _dot(x, w3b[...])
        y_ref[...] = _dot(hid.astype(BF16), w2b[...])

    @pl.when(i >= n_used)
    def _():
        y_ref[...] = jnp.zeros_like(y_ref)


def experts(xs, block_expert, n_used, w1, w3, w2, layer):
    n_rows, d = xs.shape
    n_blocks = n_rows // MOE_BLOCK
    ff = w1.shape[-1]
    grid_spec = pltpu.PrefetchScalarGridSpec(
        num_scalar_prefetch=2,
        grid=(n_blocks,),
        in_specs=[pl.BlockSpec((MOE_BLOCK, d), lambda i, be, nu: (jnp.minimum(i, nu[0] - 1), 0)),
                  pl.BlockSpec((None, 1, d, ff), lambda i, be, nu: (layer, be[i], 0, 0)),
                  pl.BlockSpec((None, 1, d, ff), lambda i, be, nu: (layer, be[i], 0, 0)),
                  pl.BlockSpec((None, 1, ff, d), lambda i, be, nu: (layer, be[i], 0, 0))],
        out_specs=pl.BlockSpec((MOE_BLOCK, d), lambda i, be, nu: (i, 0)),
        scratch_shapes=[pltpu.VMEM((d, ff), BF16),
                        pltpu.VMEM((d, ff), BF16),
                        pltpu.VMEM((ff, d), BF16)],
    )
    return pl.pallas_call(
        _expert_kernel,
        grid_spec=grid_spec,
        out_shape=jax.ShapeDtypeStruct((n_rows, d), F32),
        compiler_params=_cparams(("arbitrary",)),
        name="experts",
    )(block_expert, n_used, xs, w1, w3, w2)


def _combine_kernel(d_ref, dn_ref, x_ref, mod_ref, g_ref, slot_ref, y_hbm, o_ref, ybuf, sem, *, final):
    y = _expert_mix(d_ref, dn_ref, slot_ref, y_hbm, ybuf, sem)
    x = x_ref[...] + mod_ref[0, 5:6, :] * y
    if final:
        x = x * lax.rsqrt(jnp.mean(x * x, axis=-1, keepdims=True) + EPS) * g_ref[...]
    o_ref[...] = x


def _expert_mix(d_ref, dn_ref, slot_ref, y_hbm, ybuf, sem):
    i = pl.program_id(0)
    nb = pl.num_programs(0)
    slot = i % 2

    def chunk(s):
        def copy(local, glob, granules):
            return _granule_copy(y_hbm, glob * GRANULE, ybuf.at[s], local * GRANULE, sem.at[s], granules)
        return copy

    @pl.when(i == 0)
    def _():
        ybuf[...] = jnp.zeros_like(ybuf)
        _for_each_run_chunk(d_ref, lambda l, g, n, q: chunk(0)(l, g, n).start(priority=q))

    @pl.when(i + 1 < nb)
    def _():
        _for_each_run_chunk(dn_ref, lambda l, g, n, q: chunk(1 - slot)(l, g, n).start(priority=q))

    _for_each_run_chunk(d_ref, lambda l, g, n, q: chunk(slot)(l, g, n).wait())

    hi, lo = _split(_slot_matrix(slot_ref, ybuf.shape[1], weighted=True))
    yb = ybuf[slot].astype(BF16)
    return _dot(hi, yb) + _dot(lo, yb)


def combine(x, modv, ys, table, slots, final_g, n_batch, n_ctx_tiles, final):
    n, d = x.shape
    nt = n // TOK_TILE
    tpb = nt // n_batch
    if final:
        lat = tpb - n_ctx_tiles
        steps = n_batch * lat
        tile = lambda s: (s // lat) * tpb + n_ctx_tiles + s % lat
        out_rows = steps * TOK_TILE
    else:
        steps = nt
        tile = lambda s: s
        out_rows = n

    def mod_idx(s):
        t = tile(s)
        return (t // tpb) * 2 + ((t % tpb) >= n_ctx_tiles).astype(jnp.int32)

    return pl.pallas_call(
        functools.partial(_combine_kernel, final=final),
        grid=(steps,),
        in_specs=[pl.BlockSpec((1, 1, LANES), lambda s: (tile(s), 0, 0), memory_space=pltpu.SMEM),
                  pl.BlockSpec((1, 1, LANES), lambda s: (tile(jnp.minimum(s + 1, steps - 1)), 0, 0),
                               memory_space=pltpu.SMEM),
                  pl.BlockSpec((TOK_TILE, d), lambda s: (tile(s), 0)),
                  pl.BlockSpec((1, N_MOD, d), lambda s: (mod_idx(s), 0, 0)),
                  pl.BlockSpec((1, d), lambda s: (0, 0)),
                  pl.BlockSpec((TOK_TILE, LANES), lambda s: (tile(s), 0)),
                  pl.BlockSpec(memory_space=pl.ANY)],
        out_specs=pl.BlockSpec((TOK_TILE, d), lambda s: (s, 0)),
        out_shape=jax.ShapeDtypeStruct((out_rows, d), F32),
        scratch_shapes=[pltpu.VMEM((2, SORT_ROWS, d), F32),
                        pltpu.SemaphoreType.DMA((2,))],
        compiler_params=_cparams(("arbitrary",)),
        name="combine_final" if final else "combine",
    )(table, table, x, modv, final_g, slots, ys)


def sorted_rows_bound(n_tok):
    nt = n_tok // TOK_TILE
    rows = n_tok * TOP_K + nt * N_EXPERTS * (GRANULE - 1) + N_EXPERTS * (MOE_BLOCK - GRANULE)
    return -(-rows // MOE_BLOCK) * MOE_BLOCK


def moe_tables(gran, n_rows):
    nt = gran.shape[0]
    per_blk = MOE_BLOCK // GRANULE
    local = jnp.cumsum(gran, axis=1) - gran
    before = jnp.cumsum(gran, axis=0) - gran
    total = jnp.sum(gran, axis=0)
    padded = (total + per_blk - 1) // per_blk * per_blk
    region_end = jnp.cumsum(padded)
    region = region_end - padded
    table = jnp.concatenate([local, region[None, :] + before, gran,
                             jnp.zeros((nt, LANES - 3 * N_EXPERTS), gran.dtype)], axis=1)
    table = table.astype(jnp.int32).reshape(nt, 1, LANES)
    tails = jnp.zeros((LANES,), jnp.int32).at[:N_EXPERTS].set(region + total)
    tails = tails.at[N_EXPERTS:2 * N_EXPERTS].set(padded - total)
    n_blocks = n_rows // MOE_BLOCK
    blk = jnp.arange(n_blocks, dtype=jnp.int32) * per_blk
    block_expert = jnp.minimum(jnp.sum(blk[:, None] >= region_end[None, :], axis=1), N_EXPERTS - 1)
    n_used = (region_end[-1] // per_blk).astype(jnp.int32).reshape(1)
    tails = tails.at[2 * N_EXPERTS].set(n_used[0]).reshape(1, 1, LANES)
    return table, tails, block_expert.astype(jnp.int32), n_used


def _rope_tables(ctx_len, n_lat):
    rows = n_lat // GRID_W
    row = np.repeat(np.arange(rows, dtype=np.float32), GRID_W)
    col = np.tile(np.arange(GRID_W, dtype=np.float32), rows)
    half = ATT_HD // 4
    inv = jnp.asarray(ROPE_BASE, F32) ** (-jnp.arange(half, dtype=F32) / half)
    ang_r = jnp.asarray(row)[:, None] * inv
    ang_c = jnp.asarray(col)[:, None] * inv
    cos64 = jnp.concatenate([jnp.cos(ang_r)] * 2 + [jnp.cos(ang_c)] * 2, axis=1)
    sin64 = jnp.concatenate([-jnp.sin(ang_r), jnp.sin(ang_r), -jnp.sin(ang_c), jnp.sin(ang_c)], axis=1)
    cos64 = jnp.concatenate([jnp.ones((ctx_len, ATT_HD), F32), cos64], axis=0)
    sin64 = jnp.concatenate([jnp.zeros((ctx_len, ATT_HD), F32), sin64], axis=0)
    return jnp.tile(cos64, (1, ATT_HEADS)), jnp.tile(sin64, (1, ATT_HEADS))


def _pack_w_in(w):
    assert w.shape[-1] == W_IN_COLS
    return jnp.pad(w.astype(BF16), ((0, 0), (0, 0), (0, W_IN_PADDED - W_IN_COLS)))


def kernel(x, c, ctx, c_ctx, hg_lb_logits, router_w, router_b, final_g, w_mod, b_mod, g_mix, g_ffn,
           w_in, conv_w, conv_b, conv_ln_g, conv_ln_b, hg_norm_g, gla_w2, gla_b2, gla_norm_g, att_sink,
           w_br_conv, w_br_hg, w_br_gla, w_br_att, w_out, moe_w1, moe_w3, moe_w2):
    n_batch, n_lat, d = x.shape
    ctx_len = ctx.shape[1]
    depth = w_in.shape[0]
    assert ctx_len % TOK_TILE == 0 and n_lat % TOK_TILE == 0 and n_lat % GRID_W == 0
    seq = ctx_len + n_lat
    n_ctx_tiles = ctx_len // TOK_TILE

    xs = jnp.concatenate([ctx, x], axis=1).reshape(n_batch * seq, d)

    c_rows = jnp.zeros((8, d), F32).at[:n_batch].set(c).at[n_batch].set(c_ctx)
    mods = modulation(c_rows, w_mod, b_mod).reshape(depth, 8, N_MOD, d)
    modv = jnp.stack([mods[:, n_batch] if j % 2 == 0 else mods[:, j // 2] for j in range(2 * n_batch)], axis=1)

    lb_sm = jax.nn.softmax(hg_lb_logits.astype(F32), axis=0)
    lower = jnp.cumsum(lb_sm, axis=0) - lb_sm[0]
    cos_t, sin_t = _rope_tables(ctx_len, n_lat)
    ones_bd = jnp.asarray(np.arange(256)[:, None] // 64 == np.arange(256)[None, :] // 64, BF16)
    rw = jnp.zeros((d, LANES), F32).at[:, :N_EXPERTS].set(router_w.astype(F32))
    rb = jnp.zeros((1, LANES), F32).at[0, :N_EXPERTS].set(router_b.astype(F32))
    tri = jnp.asarray(np.tril(np.ones((TOK_TILE, TOK_TILE))), BF16)
    upper = jnp.asarray(np.triu(np.ones((LANES, LANES)), k=1), BF16)

    w_in_p = _pack_w_in(w_in)
    wb_conv, wb_hg, wb_gla, wb_att, wb_out = [w.astype(BF16) for w in (w_br_conv, w_br_hg, w_br_gla, w_br_att, w_out)]

    pending = None
    for l in range(depth):
        w2p = jnp.zeros((LANES, 2 * GLA_K), F32)
        w2p = w2p.at[0:GLA_RANK, 0:GLA_K].set(gla_w2[l, 0]).at[GLA_RANK:2 * GLA_RANK, GLA_K:].set(gla_w2[l, 1])
        b2p = gla_b2[l].reshape(1, 2 * GLA_K)
        proj = inproj(xs, modv[l], g_mix[l].reshape(1, d), w_in_p, l, lower[l], w2p, b2p, cos_t, sin_t,
                      n_batch, n_ctx_tiles, moe=pending)
        if pending is not None:
            xs, proj = proj[0], proj[1:]
        conv_z, hg_pack, gla_pack, att_pack, bgate = proj

        hg_f, hg_b = gated_scan(hg_pack, (0, 1, 3, 4), (0, 2, 3, 5), HG_W, n_batch, n_ctx_tiles, "scan_hgrn")
        gla_f, gla_b, att_o = gated_scan(gla_pack, (0, 1, 2, 2), (0, 1, 2, 3), GLA_K, n_batch, n_ctx_tiles,
                                         "scan_gla_attn", attn=(att_pack, att_sink[l].astype(F32), ctx_len))

        lw = dict(conv_w=jnp.zeros((32, CONV_CH), F32).at[:CONV_K].set(conv_w[l]),
                  conv_b=conv_b[l].reshape(1, -1), conv_ln_g=conv_ln_g[l].reshape(1, -1),
                  conv_ln_b=conv_ln_b[l].reshape(1, -1), hg_norm_g=hg_norm_g[l].reshape(1, -1),
                  gla_norm_g=gla_norm_g[l].reshape(1, -1), ones_bd=ones_bd,
                  w_br_conv=wb_conv, w_br_hg=wb_hg, w_br_gla=wb_gla, w_br_att=wb_att, w_out=wb_out,
                  g_ffn=g_ffn[l].reshape(1, d), router_w=rw, router_b=rb, tri=tri, upper=upper, layer=l)
        x_new, h2, slots, gran = merge(xs, modv[l], conv_z, hg_pack, hg_f, hg_b, gla_pack, gla_f, gla_b,
                                       att_o, bgate, lw, n_batch, n_ctx_tiles)

        n_rows = sorted_rows_bound(n_batch * seq)
        table, tails, block_expert, n_used = moe_tables(gran[:, 0, :N_EXPERTS], n_rows)
        x_sorted = dispatch(h2, slots, table, tails, n_rows)
        ys = experts(x_sorted, block_expert, n_used, moe_w1, moe_w3, moe_w2, l)
        if l < depth - 1:
            xs, pending = x_new, (modv[l], ys, table, slots)
        else:
            xs = combine(x_new, modv[l], ys, table, slots, final_g.reshape(1, d), n_batch, n_ctx_tiles, True)

    return xs.reshape(n_batch, n_lat, d)
```
